```python
import jax, jax.numpy as jnp
from jax import lax
import numpy as np

D_MODEL = 1024
BATCH = 2
SEQ = 8192
DEPTH = 1

GRID_W = 64
CTX_LEN = 256
CHUNK = 128
MIX_WIDTH = D_MODEL
A_WIDTH = MIX_WIDTH // 2
A_GROUPS = 4
A_GROUP_DIM = A_WIDTH // A_GROUPS
R_WIDTH = MIX_WIDTH - A_WIDTH
R_HEADS = 4
R_HEAD_DIM = R_WIDTH // R_HEADS
IN_COLS = 2 * A_WIDTH + 5 * R_WIDTH
D_FF = -(-8 * D_MODEL // (3 * 256)) * 256
RMS_EPS = 1e-6
ROPE_BASE = 10000.0

kernel_name = "hymba_gmlp_retention_prefix_dit_block"


def rmsnorm(x, g):
    x32 = x.astype(jnp.float32)
    y = x32 * lax.rsqrt(jnp.mean(x32 * x32, axis=-1, keepdims=True) + RMS_EPS)
    return (y * g.astype(jnp.float32)).astype(x.dtype)


def head_norm(o):
    return o * lax.rsqrt(jnp.mean(o * o, axis=-1, keepdims=True) + RMS_EPS)


def modulate(h, shift, scale):
    return h * (1.0 + scale[:, None, :]) + shift[:, None, :]


def rope_2d(t, rows, cols):
    half = t.shape[-1] // 2
    n_freq = half // 2
    inv = ROPE_BASE ** (-jnp.arange(n_freq, dtype=jnp.float32) / n_freq)

    def rot(u, pos):
        ang = pos[:, None] * inv[None, :]
        cos = jnp.cos(ang)[None, :, None, :]
        sin = jnp.sin(ang)[None, :, None, :]
        u1, u2 = u[..., :n_freq], u[..., n_freq:]
        return jnp.concatenate([u1 * cos - u2 * sin, u1 * sin + u2 * cos], axis=-1)

    return jnp.concatenate([rot(t[..., :half], rows), rot(t[..., half:], cols)], axis=-1)


def chunk_states(k, v, log_gamma, s0):
    B, L, H, dk = k.shape
    n = L // CHUNK
    kc = k.reshape(B, n, CHUNK, H, dk)
    vc = v.reshape(B, n, CHUNK, H, v.shape[-1])
    pos = jnp.arange(CHUNK, dtype=jnp.float32)
    zeta = jnp.exp(log_gamma[:, None] * (CHUNK - 1 - pos)[None, :])
    upd = jnp.einsum('bnjhd,hj,bnjhe->nbhde', kc, zeta, vc)
    decay = jnp.exp(log_gamma * CHUNK)[None, :, None, None]

    def step(s, u):
        return decay * s + u, s

    s_final, s_prev = lax.scan(step, s0, upd)
    return s_prev, s_final


def retention_out(q, k, v, log_gamma, s_prev):
    B, L, H, dk = q.shape
    n = L // CHUNK
    qc = q.reshape(B, n, CHUNK, H, dk)
    kc = k.reshape(B, n, CHUNK, H, dk)
    vc = v.reshape(B, n, CHUNK, H, v.shape[-1])
    pos = jnp.arange(CHUNK, dtype=jnp.float32)
    diff = pos[:, None] - pos[None, :]
    dmask = jnp.where(diff[None] >= 0,
                      jnp.exp(log_gamma[:, None, None] * jnp.maximum(diff, 0.0)[None]), 0.0)
    scores = jnp.einsum('bnihd,bnjhd->bnhij', qc, kc) * dmask[None, None]
    inner = jnp.einsum('bnhij,bnjhe->bnihe', scores, vc)
    xi = jnp.exp(log_gamma[:, None] * (pos + 1.0)[None, :])
    cross = jnp.einsum('bnihd,hi,nbhde->bnihe', qc, xi, s_prev)
    return (inner + cross).reshape(B, L, H, v.shape[-1])


def spatial_gating(zu, zv, sg_gain, sg_w, sg_b):
    B, L, _ = zu.shape
    n = L // CHUNK
    u = jax.nn.gelu(zu)
    v = jax.nn.gelu(zv).reshape(B, n, CHUNK, A_GROUPS, A_GROUP_DIM)
    v32 = v.astype(jnp.float32)
    v = (v32 * lax.rsqrt(jnp.mean(v32 * v32, axis=-1, keepdims=True) + RMS_EPS)
         * sg_gain.reshape(A_GROUPS, A_GROUP_DIM)).astype(zu.dtype)
    mixed = jnp.einsum('gpq,bnqgc->bnpgc', sg_w, v) + jnp.transpose(sg_b)[None, None, :, :, None]
    return u * mixed.reshape(B, L, A_WIDTH)


def split_proj(z):
    cuts = [A_WIDTH, 2 * A_WIDTH] + [2 * A_WIDTH + i * R_WIDTH for i in range(1, 5)]
    return jnp.split(z, cuts, axis=-1)


def heads(t):
    B, L, _ = t.shape
    return t.astype(jnp.float32).reshape(B, L, R_HEADS, R_HEAD_DIM)


def ffn(h, w_gate, w_up, w_down):
    return (jax.nn.silu(h @ w_gate) * (h @ w_up)) @ w_down


def setup_inputs(seed: int = 0) -> dict:
    key = jax.random.key(seed)
    ks = jax.random.split(key, 24)
    f32 = jnp.float32
    nrm = lambda k, s: jax.random.normal(k, s, dtype=f32)
    gamma0 = 1.0 - 2.0 ** (-5.0 - np.arange(R_HEADS, dtype=np.float32))
    logit0 = jnp.asarray(np.log(gamma0) - np.log1p(-gamma0), dtype=f32)
    return {
        "x": nrm(ks[0], (BATCH, SEQ, D_MODEL)),
        "c": nrm(ks[1], (BATCH, D_MODEL)),
        "ctx": nrm(ks[2], (BATCH, CTX_LEN, D_MODEL)),
        "c_ctx": nrm(ks[3], (D_MODEL,)),
        "w_mod": nrm(ks[4], (DEPTH, D_MODEL, 6 * D_MODEL)) * (0.5 * D_MODEL ** -0.5),
        "b_mod": nrm(ks[5], (DEPTH, 6 * D_MODEL)) * 0.02,
        "norm1": 1.0 + 0.02 * nrm(ks[6], (DEPTH, D_MODEL)),
        "w_in": nrm(ks[7], (DEPTH, D_MODEL, IN_COLS)) * D_MODEL ** -0.5,
        "sg_gain": 1.0 + 0.02 * nrm(ks[8], (DEPTH, A_WIDTH)),
        "sg_w": nrm(ks[9], (DEPTH, A_GROUPS, CHUNK, CHUNK)) * CHUNK ** -0.5,
        "sg_b": 1.0 + 0.02 * nrm(ks[10], (DEPTH, A_GROUPS, CHUNK)),
        "ret_logit_f": logit0[None, :] + 0.05 * nrm(ks[11], (DEPTH, R_HEADS)),
        "ret_logit_b": logit0[None, :] + 0.05 * nrm(ks[12], (DEPTH, R_HEADS)),
        "w_out": nrm(ks[13], (DEPTH, MIX_WIDTH, D_MODEL)) * MIX_WIDTH ** -0.5,
        "norm2": 1.0 + 0.02 * nrm(ks[14], (DEPTH, D_MODEL)),
        "w_gate": nrm(ks[15], (DEPTH, D_MODEL, D_FF)) * D_MODEL ** -0.5,
        "w_up": nrm(ks[16], (DEPTH, D_MODEL, D_FF)) * D_MODEL ** -0.5,
        "w_down": nrm(ks[17], (DEPTH, D_FF, D_MODEL)) * D_FF ** -0.5,
        "norm_f": 1.0 + 0.02 * nrm(ks[18], (D_MODEL,)),
    }


def reference(x, c, ctx, c_ctx, w_mod, b_mod, norm1, w_in, sg_gain, sg_w, sg_b,
              ret_logit_f, ret_logit_b, w_out, norm2, w_gate, w_up, w_down, norm_f):
    B, L, _ = x.shape
    rows_n = L // GRID_W
    rows = jnp.repeat(jnp.arange(rows_n, dtype=jnp.float32), GRID_W)
    cols = jnp.tile(jnp.arange(GRID_W, dtype=jnp.float32), rows_n)
    s_zero = jnp.zeros((B, R_HEADS, R_HEAD_DIM, R_HEAD_DIM), jnp.float32)
    k_scale = R_HEAD_DIM ** -0.5
    kv_lo = 2 * A_WIDTH + R_WIDTH
    kv_hi = 2 * A_WIDTH + 3 * R_WIDTH

    for l in range(DEPTH):
        last = l == DEPTH - 1
        sh1, sc1, g1, sh2, sc2, g2 = jnp.split(jax.nn.silu(c) @ w_mod[l] + b_mod[l], 6, axis=-1)
        csh1, csc1, cg1, csh2, csc2, cg2 = jnp.split(
            (jax.nn.silu(c_ctx) @ w_mod[l] + b_mod[l])[None, :], 6, axis=-1)
        lg_f = jax.nn.log_sigmoid(ret_logit_f[l].astype(jnp.float32))
        lg_b = jax.nn.log_sigmoid(ret_logit_b[l].astype(jnp.float32))

        hc = modulate(rmsnorm(ctx, norm1[l]), csh1, csc1)
        if last:
            kc_r, vc_r = jnp.split(hc @ w_in[l][:, kv_lo:kv_hi], 2, axis=-1)
        else:
            cu, cv, cq_r, kc_r, vc_r, cgf, cgb = split_proj(hc @ w_in[l])
        kc = heads(kc_r) * k_scale
        vc = heads(vc_r)
        kc_rev, vc_rev = jnp.flip(kc, 1), jnp.flip(vc, 1)
        sp_cf, s_cf = chunk_states(kc, vc, lg_f, s_zero)
        sp_cb, s_cb = chunk_states(kc_rev, vc_rev, lg_b, s_zero)
        if not last:
            qc = heads(cq_r)
            o_cf = retention_out(qc, kc, vc, lg_f, sp_cf)
            o_cb = jnp.flip(retention_out(jnp.flip(qc, 1), kc_rev, vc_rev, lg_b, sp_cb), 1)
            Lc = ctx.shape[1]
            yc_r = (jax.nn.silu(cgf.astype(jnp.float32)) * head_norm(o_cf).reshape(B, Lc, R_WIDTH)
                    + jax.nn.silu(cgb.astype(jnp.float32)) * head_norm(o_cb).reshape(B, Lc, R_WIDTH))
            yc_a = spatial_gating(cu, cv, sg_gain[l], sg_w[l], sg_b[l])
            yc = jnp.concatenate([yc_a, yc_r.astype(ctx.dtype)], axis=-1) @ w_out[l]
            ctx_new = ctx + cg1[:, None, :] * yc
            hc2 = modulate(rmsnorm(ctx_new, norm2[l]), csh2, csc2)
            ctx_next = ctx_new + cg2[:, None, :] * ffn(hc2, w_gate[l], w_up[l], w_down[l])

        hx = modulate(rmsnorm(x, norm1[l]), sh1, sc1)
        u, v, q_r, k_r, v_r, gf, gb = split_proj(hx @ w_in[l])
        y_a = spatial_gating(u, v, sg_gain[l], sg_w[l], sg_b[l])
        q = rope_2d(heads(q_r), rows, cols)
        k = rope_2d(heads(k_r), rows, cols) * k_scale
        vv = heads(v_r)
        sp_f, _ = chunk_states(k, vv, lg_f, s_cf)
        o_f = retention_out(q, k, vv, lg_f, sp_f)
        q_rev, k_rev, v_rev = jnp.flip(q, 1), jnp.flip(k, 1), jnp.flip(vv, 1)
        sp_b, _ = chunk_states(k_rev, v_rev, lg_b, s_cb)
        o_b = jnp.flip(retention_out(q_rev, k_rev, v_rev, lg_b, sp_b), 1)
        y_r = (jax.nn.silu(gf.astype(jnp.float32)) * head_norm(o_f).reshape(B, L, R_WIDTH)
               + jax.nn.silu(gb.astype(jnp.float32)) * head_norm(o_b).reshape(B, L, R_WIDTH))
        y = jnp.concatenate([y_a, y_r.astype(x.dtype)], axis=-1) @ w_out[l]
        x = x + g1[:, None, :] * y
        h2 = modulate(rmsnorm(x, norm2[l]), sh2, sc2)
        x = x + g2[:, None, :] * ffn(h2, w_gate[l], w_up[l], w_down[l])
        if not last:
            ctx = ctx_next

    return rmsnorm(x, norm_f)
```

```python
import functools

import numpy as np
import jax
import jax.numpy as jnp
from jax import lax
from jax.experimental import pallas as pl
from jax.experimental.pallas import tpu as pltpu

CHUNK = 128
GRID_W = 64
GROUPS = 4
HEADS = 4
HEAD_DIM = 128
RMS_EPS = 1e-6
ROPE_BASE = 10000.0
LANES = 128
MOD_ROWS = 8
VMEM_LIMIT_BYTES = 56 * 1024 * 1024

T_DMASK_F, T_DMASK_B, T_XI_F, T_XI_B, T_ZETA_F, T_ZETA_B, T_DECAY_F, T_DECAY_B = range(8)

_BF16 = jnp.bfloat16
_F32 = jnp.float32


def _dot(a, b):
    return jnp.dot(a, b, preferred_element_type=_F32)


def _dot_nt(a, b):
    return lax.dot_general(a, b, (((1,), (1,)), ((), ())), preferred_element_type=_F32)


def _dot_tn(a, b):
    return lax.dot_general(a, b, (((0,), (0,)), ((), ())), preferred_element_type=_F32)


def _silu(x):
    return x * (1.0 / (1.0 + jnp.exp(-x)))


def _gelu_tanh(x):
    c = np.float32(np.sqrt(2.0 / np.pi))
    return x * (0.5 * (1.0 + jnp.tanh(c * (x + 0.044715 * (x * x * x)))))


def _rms(x):
    return x * lax.rsqrt(jnp.mean(x * x, axis=-1, keepdims=True) + RMS_EPS)


def _head(a, h):
    return a[:, h * HEAD_DIM:(h + 1) * HEAD_DIM]


def _mod_kernel(c_ref, w_ref, b_ref, o_ref):
    o_ref[...] = _dot(_silu(c_ref[...]), w_ref[...]) + b_ref[...]


def _modulation(cc, w_mod, b_mod, block_n):
    d, n = w_mod.shape
    return pl.pallas_call(
        _mod_kernel,
        grid=(n // block_n,),
        in_specs=[pl.BlockSpec((MOD_ROWS, d), lambda j: (0, 0)),
                  pl.BlockSpec((d, block_n), lambda j: (0, j)),
                  pl.BlockSpec((1, block_n), lambda j: (0, j))],
        out_specs=pl.BlockSpec((MOD_ROWS, block_n), lambda j: (0, j)),
        out_shape=jax.ShapeDtypeStruct((MOD_ROWS, n), _F32),
        name="mod",
    )(cc, w_mod, b_mod)


def _tables_kernel(lf_ref, lb_ref, o_ref):
    def log_sigmoid(x):
        return -(jnp.maximum(-x, 0.0) + jnp.log1p(jnp.exp(-jnp.abs(x))))

    shape = (CHUNK, HEADS * HEAD_DIM)
    lg_f = jnp.broadcast_to(log_sigmoid(lf_ref[...]), shape)
    lg_b = jnp.broadcast_to(log_sigmoid(lb_ref[...]), shape)
    i = lax.broadcasted_iota(jnp.int32, shape, 0).astype(_F32)
    j = (lax.broadcasted_iota(jnp.int32, shape, 1) & (HEAD_DIM - 1)).astype(_F32)
    o_ref[T_DMASK_F] = jnp.where(i >= j, jnp.exp(lg_f * jnp.maximum(i - j, 0.0)), 0.0)
    o_ref[T_DMASK_B] = jnp.where(j >= i, jnp.exp(lg_b * jnp.maximum(j - i, 0.0)), 0.0)
    o_ref[T_XI_F] = jnp.exp(lg_f * (i + 1.0))
    o_ref[T_XI_B] = jnp.exp(lg_b * (CHUNK - i))
    o_ref[T_ZETA_F] = jnp.exp(lg_f * (CHUNK - 1.0 - i))
    o_ref[T_ZETA_B] = jnp.exp(lg_b * i)
    o_ref[T_DECAY_F] = jnp.exp(lg_f * CHUNK)
    o_ref[T_DECAY_B] = jnp.exp(lg_b * CHUNK)


def _retention_tables(logit_f, logit_b):
    rep = lambda l: jnp.repeat(l.astype(_F32), HEAD_DIM)[None, :]
    return pl.pallas_call(
        _tables_kernel,
        out_shape=jax.ShapeDtypeStruct((8, CHUNK, HEADS * HEAD_DIM), _F32),
        name="tables",
    )(rep(logit_f), rep(logit_b))


def _ctx_kernel(ctx_ref, sh_ref, sc_ref, n1_ref, wkv_ref, tab_ref, sf_ref, sb_ref, *, n_chunks, k_scale):
    r_width = HEADS * HEAD_DIM
    hc = (_rms(ctx_ref[...]) * n1_ref[...]) * (1.0 + sc_ref[...]) + sh_ref[...]
    kv = _dot(hc.astype(_BF16), wkv_ref[...])
    k = (kv[:, :r_width] * k_scale).astype(_BF16)
    v = kv[:, r_width:]
    for h in range(HEADS):
        s_f = jnp.zeros((HEAD_DIM, HEAD_DIM), _F32)
        s_b = jnp.zeros((HEAD_DIM, HEAD_DIM), _F32)
        for n in range(n_chunks):
            rows = slice(n * CHUNK, (n + 1) * CHUNK)
            upd = _dot_tn(_head(k[rows], h), (_head(v[rows], h) * _head(tab_ref[T_ZETA_F], h)).astype(_BF16))
            s_f = _head(tab_ref[T_DECAY_F], h) * s_f + upd
        for n in reversed(range(n_chunks)):
            rows = slice(n * CHUNK, (n + 1) * CHUNK)
            upd = _dot_tn(_head(k[rows], h), (_head(v[rows], h) * _head(tab_ref[T_ZETA_B], h)).astype(_BF16))
            s_b = _head(tab_ref[T_DECAY_B], h) * s_b + upd
        sf_ref[h] = s_f
        sb_ref[h] = s_b


def _context_states(ctx, mod3, norm1, w_kv, tables, k_scale):
    b, lc, d = ctx.shape
    state = jax.ShapeDtypeStruct((b, HEADS, HEAD_DIM, HEAD_DIM), _F32)
    ctx_row = b
    return pl.pallas_call(
        functools.partial(_ctx_kernel, n_chunks=lc // CHUNK, k_scale=k_scale),
        grid=(b,),
        in_specs=[pl.BlockSpec((None, lc, d), lambda i: (i, 0, 0)),
                  pl.BlockSpec((None, 1, d), lambda i: (ctx_row, 0, 0)),
                  pl.BlockSpec((None, 1, d), lambda i: (ctx_row, 0, 1)),
                  pl.BlockSpec((1, d), lambda i: (0, 0)),
                  pl.BlockSpec(w_kv.shape, lambda i: (0, 0)),
                  pl.BlockSpec(tables.shape, lambda i: (0, 0, 0))],
        out_specs=[pl.BlockSpec((None, HEADS, HEAD_DIM, HEAD_DIM), lambda i: (i, 0, 0, 0))] * 2,
        out_shape=[state, state],
        compiler_params=pltpu.CompilerParams(dimension_semantics=("arbitrary",),
                                             vmem_limit_bytes=VMEM_LIMIT_BYTES),
        name="ctx",
    )(ctx, mod3, mod3, norm1, w_kv, tables)


def _retention_chunk(q, k, v, gate, s_ref, tab_ref, t_dmask, t_xi, t_zeta, t_decay):
    outs = []
    for h in range(HEADS):
        qh, kh, vh = _head(q, h), _head(k, h), _head(v, h)
        scores = _dot_nt(qh, kh) * _head(tab_ref[t_dmask], h)
        s_in = s_ref[h]
        xq = (qh.astype(_F32) * _head(tab_ref[t_xi], h)).astype(_BF16)
        o = _dot(scores.astype(_BF16), vh.astype(_BF16)) + _dot(xq, s_in.astype(_BF16))
        upd = _dot_tn(kh, (vh * _head(tab_ref[t_zeta], h)).astype(_BF16))
        s_ref[h] = _head(tab_ref[t_decay], h) * s_in + upd
        outs.append(_silu(_head(gate, h)) * _rms(o))
    return jnp.concatenate(outs, axis=-1)


def _fwd_kernel(x_ref, sh_ref, sc_ref, n1_ref, win_ref, gain_ref, sgw_ref, sgb_ref, rc_ref, rs_ref,
                tab_ref, s0_ref, ya_ref, qkvg_ref, s_ref, *, n_chunks, k_scale):
    a_width = GROUPS * LANES
    r_width = HEADS * HEAD_DIM

    @pl.when(pl.program_id(1) == 0)
    def _():
        s_ref[...] = s0_ref[...]

    hx = ((_rms(x_ref[...]) * n1_ref[...]) * (1.0 + sc_ref[...]) + sh_ref[...]).astype(_BF16)

    def proj(lo, width):
        return _dot(hx, win_ref[:, lo:lo + width])

    u = _gelu_tanh(proj(0, a_width))
    v = _gelu_tanh(proj(a_width, a_width))
    for g in range(GROUPS):
        cols = slice(g * LANES, (g + 1) * LANES)
        vg = (_rms(v[:, cols]) * gain_ref[:, cols]).astype(_BF16)
        for n in range(n_chunks):
            rows = slice(n * CHUNK, (n + 1) * CHUNK)
            mixed = _dot(sgw_ref[g], vg[rows]) + sgb_ref[:, cols]
            ya_ref[rows, cols] = (u[rows, cols] * mixed).astype(_BF16)

    def rope(t):
        return jnp.concatenate(
            [_head(t, h) * rc_ref[...] + pltpu.roll(_head(t, h), HEAD_DIM // 2, 1) * rs_ref[...]
             for h in range(HEADS)], axis=-1)

    base = 2 * a_width
    q = rope(proj(base, r_width)).astype(_BF16)
    k = (rope(proj(base + r_width, r_width)) * k_scale).astype(_BF16)
    vr = proj(base + 2 * r_width, r_width)
    gf = proj(base + 3 * r_width, r_width)
    gb = proj(base + 4 * r_width, r_width)
    qkvg_ref[:, 0:r_width] = q
    qkvg_ref[:, r_width:2 * r_width] = k
    qkvg_ref[:, 2 * r_width:3 * r_width] = vr.astype(_BF16)
    qkvg_ref[:, 3 * r_width:4 * r_width] = gb.astype(_BF16)

    for n in range(n_chunks):
        rows = slice(n * CHUNK, (n + 1) * CHUNK)
        y = _retention_chunk(q[rows], k[rows], vr[rows], gf[rows], s_ref, tab_ref,
                             T_DMASK_F, T_XI_F, T_ZETA_F, T_DECAY_F)
        ya_ref[rows, a_width:a_width + r_width] = y.astype(_BF16)


def _forward_pass(x, mod3, norm1, w_in, sg_gain, sg_w, sg_bias, rope_c, rope_s, tables, s_cf, tile, k_scale):
    b, l, d = x.shape
    a_width, r_width = GROUPS * LANES, HEADS * HEAD_DIM
    const2 = lambda i, t: (0, 0)
    resident = dict(pipeline_mode=pl.Buffered(1))
    return pl.pallas_call(
        functools.partial(_fwd_kernel, n_chunks=tile // CHUNK, k_scale=k_scale),
        grid=(b, l // tile),
        in_specs=[pl.BlockSpec((None, tile, d), lambda i, t: (i, t, 0)),
                  pl.BlockSpec((None, 1, d), lambda i, t: (i, 0, 0)),
                  pl.BlockSpec((None, 1, d), lambda i, t: (i, 0, 1)),
                  pl.BlockSpec((1, d), const2),
                  pl.BlockSpec(w_in.shape, const2, **resident),
                  pl.BlockSpec((1, a_width), const2),
                  pl.BlockSpec(sg_w.shape, lambda i, t: (0, 0, 0)),
                  pl.BlockSpec(sg_bias.shape, const2),
                  pl.BlockSpec((tile, HEAD_DIM), lambda i, t: (t, 0)),
                  pl.BlockSpec((tile, HEAD_DIM), lambda i, t: (t, 0)),
                  pl.BlockSpec(tables.shape, lambda i, t: (0, 0, 0)),
                  pl.BlockSpec((None, HEADS, HEAD_DIM, HEAD_DIM), lambda i, t: (i, 0, 0, 0))],
        out_specs=[pl.BlockSpec((None, tile, a_width + r_width), lambda i, t: (i, t, 0)),
                   pl.BlockSpec((None, tile, 4 * r_width), lambda i, t: (i, t, 0))],
        out_shape=[jax.ShapeDtypeStruct((b, l, a_width + r_width), _BF16),
                   jax.ShapeDtypeStruct((b, l, 4 * r_width), _BF16)],
        scratch_shapes=[pltpu.VMEM((HEADS, HEAD_DIM, HEAD_DIM), _F32)],
        compiler_params=pltpu.CompilerParams(dimension_semantics=("arbitrary", "arbitrary"),
                                             vmem_limit_bytes=VMEM_LIMIT_BYTES),
        name="fwd",
    )(x, mod3, mod3, norm1, w_in, sg_gain, sg_w, sg_bias, rope_c, rope_s, tables, s_cf)


def _bwd_kernel(x_ref, ya_ref, qkvg_ref, g1_ref, sh_ref, sc_ref, g2_ref, n2_ref, nf_ref,
                wout_ref, wg_ref, wu_ref, wd_ref, tab_ref, s0_ref, o_ref, s_ref, *, n_chunks):
    a_width = GROUPS * LANES
    r_width = HEADS * HEAD_DIM

    @pl.when(pl.program_id(1) == 0)
    def _():
        s_ref[...] = s0_ref[...]

    yb = [None] * n_chunks
    for n in reversed(range(n_chunks)):
        rows = slice(n * CHUNK, (n + 1) * CHUNK)
        yb[n] = _retention_chunk(qkvg_ref[rows, 0:r_width], qkvg_ref[rows, r_width:2 * r_width],
                                 qkvg_ref[rows, 2 * r_width:3 * r_width].astype(_F32),
                                 qkvg_ref[rows, 3 * r_width:4 * r_width].astype(_F32),
                                 s_ref, tab_ref, T_DMASK_B, T_XI_B, T_ZETA_B, T_DECAY_B)
    y_r = ya_ref[:, a_width:].astype(_F32) + jnp.concatenate(yb, axis=0)
    y = _dot(ya_ref[:, :a_width], wout_ref[:a_width, :]) + _dot(y_r.astype(_BF16), wout_ref[a_width:, :])
    x1 = x_ref[...] + g1_ref[...] * y
    h2 = ((_rms(x1) * n2_ref[...]) * (1.0 + sc_ref[...]) + sh_ref[...]).astype(_BF16)
    act = (_silu(_dot(h2, wg_ref[...])) * _dot(h2, wu_ref[...])).astype(_BF16)
    x2 = x1 + g2_ref[...] * _dot(act, wd_ref[...])
    o_ref[...] = _rms(x2) * nf_ref[...]


def _backward_pass(x, ya, qkvg, mod3, norm2, norm_f, w_out, w_gate, w_up, w_down, tables, s_cb, tile):
    b, l, d = x.shape
    nt = l // tile
    const2 = lambda i, t: (0, 0)
    rev = lambda i, t: (i, nt - 1 - t, 0)
    resident = dict(pipeline_mode=pl.Buffered(1))
    mod_spec = lambda j: pl.BlockSpec((None, 1, d), lambda i, t: (i, 0, j))
    return pl.pallas_call(
        functools.partial(_bwd_kernel, n_chunks=tile // CHUNK),
        grid=(b, nt),
        in_specs=[pl.BlockSpec((None, tile, d), rev),
                  pl.BlockSpec((None, tile, ya.shape[-1]), rev),
                  pl.BlockSpec((None, tile, qkvg.shape[-1]), rev),
                  mod_spec(2), mod_spec(3), mod_spec(4), mod_spec(5),
                  pl.BlockSpec((1, d), const2),
                  pl.BlockSpec((1, d), const2),
                  pl.BlockSpec(w_out.shape, const2, **resident),
                  pl.BlockSpec(w_gate.shape, const2, **resident),
                  pl.BlockSpec(w_up.shape, const2, **resident),
                  pl.BlockSpec(w_down.shape, const2, **resident),
                  pl.BlockSpec(tables.shape, lambda i, t: (0, 0, 0)),
                  pl.BlockSpec((None, HEADS, HEAD_DIM, HEAD_DIM), lambda i, t: (i, 0, 0, 0))],
        out_specs=pl.BlockSpec((None, tile, d), rev),
        out_shape=jax.ShapeDtypeStruct((b, l, d), x.dtype),
        scratch_shapes=[pltpu.VMEM((HEADS, HEAD_DIM, HEAD_DIM), _F32)],
        compiler_params=pltpu.CompilerParams(dimension_semantics=("arbitrary", "arbitrary"),
                                             vmem_limit_bytes=VMEM_LIMIT_BYTES),
        name="bwd",
    )(x, ya, qkvg, mod3, mod3, mod3, mod3, norm2, norm_f, w_out, w_gate, w_up, w_down, tables, s_cb)


def _rope_tables(l):
    n_freq = HEAD_DIM // 4
    inv = ROPE_BASE ** (-jnp.arange(n_freq, dtype=_F32) / n_freq)
    rows = jnp.repeat(jnp.arange(l // GRID_W, dtype=_F32), GRID_W)
    cols = jnp.tile(jnp.arange(GRID_W, dtype=_F32), l // GRID_W)
    ang_r, ang_c = rows[:, None] * inv[None, :], cols[:, None] * inv[None, :]
    cos = jnp.concatenate([jnp.cos(ang_r), jnp.cos(ang_c)], axis=-1)
    sin = jnp.concatenate([jnp.sin(ang_r), jnp.sin(ang_c)], axis=-1)
    return jnp.concatenate([cos, cos], axis=-1), jnp.concatenate([-sin, sin], axis=-1)


def _head_dim_order():
    q = HEAD_DIM // 4
    return np.concatenate([np.arange(0, q), np.arange(2 * q, 3 * q), np.arange(q, 2 * q), np.arange(3 * q, 4 * q)])


def kernel(x, c, ctx, c_ctx, w_mod, b_mod, norm1, w_in, sg_gain, sg_w, sg_b, ret_logit_f, ret_logit_b,
           w_out, norm2, w_gate, w_up, w_down, norm_f):
    b, l, d = x.shape
    a_width, r_width = GROUPS * LANES, HEADS * HEAD_DIM
    assert w_mod.shape[0] == 1 and b + 1 <= MOD_ROWS and l % 256 == 0 and ctx.shape[1] % CHUNK == 0
    assert w_in.shape[-1] == 2 * a_width + 5 * r_width and d == a_width + r_width
    k_scale = HEAD_DIM ** -0.5
    tile = 256

    cc = jnp.zeros((MOD_ROWS, d), _F32).at[:b].set(c).at[b].set(c_ctx)
    mod = _modulation(cc, w_mod[0], b_mod[0][None, :], block_n=1536)
    mod3 = mod.reshape(MOD_ROWS, 1, 6 * d)

    tables = _retention_tables(ret_logit_f[0], ret_logit_b[0])

    order = _head_dim_order()
    qk_cols = np.concatenate([2 * a_width + blk * r_width + h * HEAD_DIM + order
                              for blk in range(2) for h in range(HEADS)])
    cols = np.concatenate([np.arange(2 * a_width), qk_cols, np.arange(2 * a_width + 2 * r_width, w_in.shape[-1])])
    w_in_b = w_in[0][:, cols].astype(_BF16)
    w_kv = w_in_b[:, 2 * a_width + r_width:2 * a_width + 3 * r_width]

    s_cf, s_cb = _context_states(ctx, mod3, norm1, w_kv, tables, k_scale)

    rope_c, rope_s = _rope_tables(l)
    sg_bias = jnp.broadcast_to(jnp.transpose(sg_b[0])[:, :, None], (CHUNK, GROUPS, LANES)).reshape(CHUNK, a_width)
    ya, qkvg = _forward_pass(x, mod3, norm1, w_in_b, sg_gain, sg_w[0].astype(_BF16), sg_bias, rope_c, rope_s,
                             tables, s_cf, tile, k_scale)
    return _backward_pass(x, ya, qkvg, mod3, norm2, norm_f[None, :], w_out[0].astype(_BF16),
                          w_gate[0].astype(_BF16), w_up[0].astype(_BF16), w_down[0].astype(_BF16),
                          tables, s_cb, tile)
```

```python
import functools

import numpy as np
import jax
import jax.numpy as jnp
from jax import lax
from jax.experimental import pallas as pl
from jax.experimental.pallas import tpu as pltpu

CHUNK = 128
GRID_W = 64
GROUPS = 4
HEADS = 4
HEAD_DIM = 128
RMS_EPS = 1e-6
ROPE_BASE = 10000.0
LANES = 128
MOD_ROWS = 8
VMEM_LIMIT_BYTES = 56 * 1024 * 1024

T_DMASK_F, T_DMASK_B, T_XI_F, T_XI_B, T_ZETA_F, T_ZETA_B, T_DECAY_F, T_DECAY_B = range(8)

_BF16 = jnp.bfloat16
_F32 = jnp.float32


def _dot(a, b):
    return jnp.dot(a, b, preferred_element_type=_F32)


def _dot_nt(a, b):
    return lax.dot_general(a, b, (((1,), (1,)), ((), ())), preferred_element_type=_F32)


def _dot_tn(a, b):
    return lax.dot_general(a, b, (((0,), (0,)), ((), ())), preferred_element_type=_F32)


def _silu(x):
    return x * (1.0 / (1.0 + jnp.exp(-x)))


def _gelu_tanh(x):
    c = np.float32(np.sqrt(2.0 / np.pi))
    return x * (0.5 * (1.0 + jnp.tanh(c * (x + 0.044715 * (x * x * x)))))


def _rms(x):
    return x * lax.rsqrt(jnp.mean(x * x, axis=-1, keepdims=True) + RMS_EPS)


def _head(a, h):
    return a[:, h * HEAD_DIM:(h + 1) * HEAD_DIM]


def _mod_kernel(c_ref, w_ref, b_ref, o_ref):
    o_ref[...] = _dot(_silu(c_ref[...]), w_ref[...]) + b_ref[...]


def _modulation(cc, w_mod, b_mod, block_n):
    d, n = w_mod.shape
    return pl.pallas_call(
        _mod_kernel,
        grid=(n // block_n,),
        in_specs=[pl.BlockSpec((MOD_ROWS, d), lambda j: (0, 0)),
                  pl.BlockSpec((d, block_n), lambda j: (0, j)),
                  pl.BlockSpec((1, block_n), lambda j: (0, j))],
        out_specs=pl.BlockSpec((MOD_ROWS, block_n), lambda j: (0, j)),
        out_shape=jax.ShapeDtypeStruct((MOD_ROWS, n), _F32),
        name="mod",
    )(cc, w_mod, b_mod)


def _tables_kernel(lf_ref, lb_ref, o_ref):
    def log_sigmoid(x):
        return -(jnp.maximum(-x, 0.0) + jnp.log1p(jnp.exp(-jnp.abs(x))))

    shape = (CHUNK, HEADS * HEAD_DIM)
    lg_f = jnp.broadcast_to(log_sigmoid(lf_ref[...]), shape)
    lg_b = jnp.broadcast_to(log_sigmoid(lb_ref[...]), shape)
    i = lax.broadcasted_iota(jnp.int32, shape, 0).astype(_F32)
    j = (lax.broadcasted_iota(jnp.int32, shape, 1) & (HEAD_DIM - 1)).astype(_F32)
    o_ref[T_DMASK_F] = jnp.where(i >= j, jnp.exp(lg_f * jnp.maximum(i - j, 0.0)), 0.0)
    o_ref[T_DMASK_B] = jnp.where(j >= i, jnp.exp(lg_b * jnp.maximum(j - i, 0.0)), 0.0)
    o_ref[T_XI_F] = jnp.exp(lg_f * (i + 1.0))
    o_ref[T_XI_B] = jnp.exp(lg_b * (CHUNK - i))
    o_ref[T_ZETA_F] = jnp.exp(lg_f * (CHUNK - 1.0 - i))
    o_ref[T_ZETA_B] = jnp.exp(lg_b * i)
    o_ref[T_DECAY_F] = jnp.exp(lg_f * CHUNK)
    o_ref[T_DECAY_B] = jnp.exp(lg_b * CHUNK)


def _retention_tables(logit_f, logit_b):
    rep = lambda l: jnp.repeat(l.astype(_F32), HEAD_DIM)[None, :]
    return pl.pallas_call(
        _tables_kernel,
        out_shape=jax.ShapeDtypeStruct((8, CHUNK, HEADS * HEAD_DIM), _F32),
        name="tables",
    )(rep(logit_f), rep(logit_b))


def _ctx_kernel(ctx_ref, sh_ref, sc_ref, n1_ref, wk_ref, wv_ref, tab_ref, sf_ref, sb_ref, *, n_chunks, k_scale):
    hc = ((_rms(ctx_ref[...]) * n1_ref[...]) * (1.0 + sc_ref[...]) + sh_ref[...]).astype(_BF16)
    k = (_dot(hc, wk_ref[...]) * k_scale).astype(_BF16)
    v = _dot(hc, wv_ref[...])
    for h in range(HEADS):
        s_f = jnp.zeros((HEAD_DIM, HEAD_DIM), _F32)
        s_b = jnp.zeros((HEAD_DIM, HEAD_DIM), _F32)
        for n in range(n_chunks):
            rows = slice(n * CHUNK, (n + 1) * CHUNK)
            upd = _dot_tn(_head(k[rows], h), (_head(v[rows], h) * _head(tab_ref[T_ZETA_F], h)).astype(_BF16))
            s_f = _head(tab_ref[T_DECAY_F], h) * s_f + upd
        for n in reversed(range(n_chunks)):
            rows = slice(n * CHUNK, (n + 1) * CHUNK)
            upd = _dot_tn(_head(k[rows], h), (_head(v[rows], h) * _head(tab_ref[T_ZETA_B], h)).astype(_BF16))
            s_b = _head(tab_ref[T_DECAY_B], h) * s_b + upd
        sf_ref[h] = s_f
        sb_ref[h] = s_b


def _context_states(ctx, mod3, norm1, w_qk, w_vg, tables, k_scale):
    b, lc, d = ctx.shape
    r_width = HEADS * HEAD_DIM
    state = jax.ShapeDtypeStruct((b, HEADS, HEAD_DIM, HEAD_DIM), _F32)
    ctx_row = b
    return pl.pallas_call(
        functools.partial(_ctx_kernel, n_chunks=lc // CHUNK, k_scale=k_scale),
        grid=(b,),
        in_specs=[pl.BlockSpec((None, lc, d), lambda i: (i, 0, 0)),
                  pl.BlockSpec((None, 1, d), lambda i: (ctx_row, 0, 0)),
                  pl.BlockSpec((None, 1, d), lambda i: (ctx_row, 0, 1)),
                  pl.BlockSpec((1, d), lambda i: (0, 0)),
                  pl.BlockSpec((d, r_width), lambda i: (0, 1)),
                  pl.BlockSpec((d, r_width), lambda i: (0, 0)),
                  pl.BlockSpec(tables.shape, lambda i: (0, 0, 0))],
        out_specs=[pl.BlockSpec((None, HEADS, HEAD_DIM, HEAD_DIM), lambda i: (i, 0, 0, 0))] * 2,
        out_shape=[state, state],
        compiler_params=pltpu.CompilerParams(dimension_semantics=("arbitrary",),
                                             vmem_limit_bytes=VMEM_LIMIT_BYTES),
        name="ctx",
    )(ctx, mod3, mod3, norm1, w_qk, w_vg, tables)


def _retention_chunk(q, k, v, gate, s_ref, tab_ref, t_dmask, t_xi, t_zeta, t_decay):
    outs = []
    for h in range(HEADS):
        qh, kh, vh = _head(q, h), _head(k, h), _head(v, h)
        scores = _dot_nt(qh, kh) * _head(tab_ref[t_dmask], h)
        s_in = s_ref[h]
        xq = (qh.astype(_F32) * _head(tab_ref[t_xi], h)).astype(_BF16)
        o = _dot(scores.astype(_BF16), vh.astype(_BF16)) + _dot(xq, s_in.astype(_BF16))
        upd = _dot_tn(kh, (vh * _head(tab_ref[t_zeta], h)).astype(_BF16))
        s_ref[h] = _head(tab_ref[t_decay], h) * s_in + upd
        outs.append(_silu(_head(gate, h)) * _rms(o))
    return jnp.concatenate(outs, axis=-1)


def _fwd_kernel(x_ref, sh_ref, sc_ref, n1_ref, wuv_ref, wqk_ref, wvg_ref, gain_ref, sgw_ref, sgb_ref, rc_ref, rs_ref,
                tab_ref, s0_ref, ya_ref, qkvg_ref, s_ref, *, n_chunks, k_scale):
    a_width = GROUPS * LANES
    r_width = HEADS * HEAD_DIM

    @pl.when(pl.program_id(1) == 0)
    def _():
        s_ref[...] = s0_ref[...]

    hx = ((_rms(x_ref[...]) * n1_ref[...]) * (1.0 + sc_ref[...]) + sh_ref[...]).astype(_BF16)

    def proj(w_ref, lo, width):
        return _dot(hx, w_ref[:, lo:lo + width])

    u = _gelu_tanh(proj(wuv_ref, 0, a_width))
    v = _gelu_tanh(proj(wuv_ref, a_width, a_width))
    for g in range(GROUPS):
        cols = slice(g * LANES, (g + 1) * LANES)
        vg = (_rms(v[:, cols]) * gain_ref[:, cols]).astype(_BF16)
        for n in range(n_chunks):
            rows = slice(n * CHUNK, (n + 1) * CHUNK)
            mixed = _dot(sgw_ref[g], vg[rows]) + sgb_ref[:, cols]
            ya_ref[rows, cols] = (u[rows, cols] * mixed).astype(_BF16)

    def rope(t):
        return jnp.concatenate(
            [_head(t, h) * rc_ref[...] + pltpu.roll(_head(t, h), HEAD_DIM // 2, 1) * rs_ref[...]
             for h in range(HEADS)], axis=-1)

    q = rope(proj(wqk_ref, 0, r_width)).astype(_BF16)
    k = (rope(proj(wqk_ref, r_width, r_width)) * k_scale).astype(_BF16)
    vr = proj(wvg_ref, 0, r_width)
    gf = proj(wvg_ref, r_width, r_width)
    gb = proj(wvg_ref, 2 * r_width, r_width)
    qkvg_ref[:, 0:r_width] = q
    qkvg_ref[:, r_width:2 * r_width] = k
    qkvg_ref[:, 2 * r_width:3 * r_width] = vr.astype(_BF16)
    qkvg_ref[:, 3 * r_width:4 * r_width] = gb.astype(_BF16)

    for n in range(n_chunks):
        rows = slice(n * CHUNK, (n + 1) * CHUNK)
        y = _retention_chunk(q[rows], k[rows], vr[rows], gf[rows], s_ref, tab_ref,
                             T_DMASK_F, T_XI_F, T_ZETA_F, T_DECAY_F)
        ya_ref[rows, a_width:a_width + r_width] = y.astype(_BF16)


def _forward_pass(x, mod3, norm1, w_uv, w_qk, w_vg, sg_gain, sg_w, sg_bias, rope_c, rope_s, tables, s_cf, tile,
                  k_scale):
    b, l, d = x.shape
    a_width, r_width = GROUPS * LANES, HEADS * HEAD_DIM
    const2 = lambda i, t: (0, 0)
    resident = dict(pipeline_mode=pl.Buffered(1))
    return pl.pallas_call(
        functools.partial(_fwd_kernel, n_chunks=tile // CHUNK, k_scale=k_scale),
        grid=(b, l // tile),
        in_specs=[pl.BlockSpec((None, tile, d), lambda i, t: (i, t, 0)),
                  pl.BlockSpec((None, 1, d), lambda i, t: (i, 0, 0)),
                  pl.BlockSpec((None, 1, d), lambda i, t: (i, 0, 1)),
                  pl.BlockSpec((1, d), const2),
                  pl.BlockSpec(w_uv.shape, const2, **resident),
                  pl.BlockSpec(w_qk.shape, const2, **resident),
                  pl.BlockSpec(w_vg.shape, const2, **resident),
                  pl.BlockSpec((1, a_width), const2),
                  pl.BlockSpec(sg_w.shape, lambda i, t: (0, 0, 0)),
                  pl.BlockSpec(sg_bias.shape, const2),
                  pl.BlockSpec((tile, HEAD_DIM), lambda i, t: (t, 0)),
                  pl.BlockSpec((tile, HEAD_DIM), lambda i, t: (t, 0)),
                  pl.BlockSpec(tables.shape, lambda i, t: (0, 0, 0)),
                  pl.BlockSpec((None, HEADS, HEAD_DIM, HEAD_DIM), lambda i, t: (i, 0, 0, 0))],
        out_specs=[pl.BlockSpec((None, tile, a_width + r_width), lambda i, t: (i, t, 0)),
                   pl.BlockSpec((None, tile, 4 * r_width), lambda i, t: (i, t, 0))],
        out_shape=[jax.ShapeDtypeStruct((b, l, a_width + r_width), _BF16),
                   jax.ShapeDtypeStruct((b, l, 4 * r_width), _BF16)],
        scratch_shapes=[pltpu.VMEM((HEADS, HEAD_DIM, HEAD_DIM), _F32)],
        compiler_params=pltpu.CompilerParams(dimension_semantics=("arbitrary", "arbitrary"),
                                             vmem_limit_bytes=VMEM_LIMIT_BYTES),
        name="fwd",
    )(x, mod3, mod3, norm1, w_uv, w_qk, w_vg, sg_gain, sg_w, sg_bias, rope_c, rope_s, tables, s_cf)


def _bwd_kernel(x_ref, ya_ref, qkvg_ref, g1_ref, sh_ref, sc_ref, g2_ref, n2_ref, nf_ref,
                wout_ref, wg_ref, wu_ref, wd_ref, tab_ref, s0_ref, o_ref, s_ref, *, n_chunks):
    a_width = GROUPS * LANES
    r_width = HEADS * HEAD_DIM

    @pl.when(pl.program_id(1) == 0)
    def _():
        s_ref[...] = s0_ref[...]

    yb = [None] * n_chunks
    for n in reversed(range(n_chunks)):
        rows = slice(n * CHUNK, (n + 1) * CHUNK)
        yb[n] = _retention_chunk(qkvg_ref[rows, 0:r_width], qkvg_ref[rows, r_width:2 * r_width],
                                 qkvg_ref[rows, 2 * r_width:3 * r_width].astype(_F32),
                                 qkvg_ref[rows, 3 * r_width:4 * r_width].astype(_F32),
                                 s_ref, tab_ref, T_DMASK_B, T_XI_B, T_ZETA_B, T_DECAY_B)
    y_r = ya_ref[:, a_width:].astype(_F32) + jnp.concatenate(yb, axis=0)
    y = _dot(ya_ref[:, :a_width], wout_ref[:a_width, :]) + _dot(y_r.astype(_BF16), wout_ref[a_width:, :])
    x1 = x_ref[...] + g1_ref[...] * y
    h2 = ((_rms(x1) * n2_ref[...]) * (1.0 + sc_ref[...]) + sh_ref[...]).astype(_BF16)
    act = (_silu(_dot(h2, wg_ref[...])) * _dot(h2, wu_ref[...])).astype(_BF16)
    x2 = x1 + g2_ref[...] * _dot(act, wd_ref[...])
    o_ref[...] = _rms(x2) * nf_ref[...]


def _backward_pass(x, ya, qkvg, mod3, norm2, norm_f, w_out, w_gate, w_up, w_down, tables, s_cb, tile):
    b, l, d = x.shape
    nt = l // tile
    const2 = lambda i, t: (0, 0)
    rev = lambda i, t: (i, nt - 1 - t, 0)
    resident = dict(pipeline_mode=pl.Buffered(1))
    mod_spec = lambda j: pl.BlockSpec((None, 1, d), lambda i, t: (i, 0, j))
    return pl.pallas_call(
        functools.partial(_bwd_kernel, n_chunks=tile // CHUNK),
        grid=(b, nt),
        in_specs=[pl.BlockSpec((None, tile, d), rev),
                  pl.BlockSpec((None, tile, ya.shape[-1]), rev),
                  pl.BlockSpec((None, tile, qkvg.shape[-1]), rev),
                  mod_spec(2), mod_spec(3), mod_spec(4), mod_spec(5),
                  pl.BlockSpec((1, d), const2),
                  pl.BlockSpec((1, d), const2),
                  pl.BlockSpec(w_out.shape, const2, **resident),
                  pl.BlockSpec(w_gate.shape, const2, **resident),
                  pl.BlockSpec(w_up.shape, const2, **resident),
                  pl.BlockSpec(w_down.shape, const2, **resident),
                  pl.BlockSpec(tables.shape, lambda i, t: (0, 0, 0)),
                  pl.BlockSpec((None, HEADS, HEAD_DIM, HEAD_DIM), lambda i, t: (i, 0, 0, 0))],
        out_specs=pl.BlockSpec((None, tile, d), rev),
        out_shape=jax.ShapeDtypeStruct((b, l, d), x.dtype),
        scratch_shapes=[pltpu.VMEM((HEADS, HEAD_DIM, HEAD_DIM), _F32)],
        compiler_params=pltpu.CompilerParams(dimension_semantics=("arbitrary", "arbitrary"),
                                             vmem_limit_bytes=VMEM_LIMIT_BYTES),
        name="bwd",
    )(x, ya, qkvg, mod3, mod3, mod3, mod3, norm2, norm_f, w_out, w_gate, w_up, w_down, tables, s_cb)


def _rope_tables(l):
    n_freq = HEAD_DIM // 4
    inv = np.float32(ROPE_BASE) ** (-np.arange(n_freq, dtype=np.float32) / np.float32(n_freq))
    rows = np.repeat(np.arange(l // GRID_W, dtype=np.float32), GRID_W)
    cols = np.tile(np.arange(GRID_W, dtype=np.float32), l // GRID_W)
    ang_r, ang_c = rows[:, None] * inv[None, :], cols[:, None] * inv[None, :]
    cos = np.concatenate([np.cos(ang_r), np.cos(ang_c)], axis=-1)
    sin = np.concatenate([np.sin(ang_r), np.sin(ang_c)], axis=-1)
    return (np.concatenate([cos, cos], axis=-1).astype(np.float32),
            np.concatenate([-sin, sin], axis=-1).astype(np.float32))


def _reorder_head_dims(w):
    d, n = w.shape
    q = HEAD_DIM // 4
    return w.reshape(d, n // HEAD_DIM, 2, 2, q).transpose(0, 1, 3, 2, 4).reshape(d, n)


def kernel(x, c, ctx, c_ctx, w_mod, b_mod, norm1, w_in, sg_gain, sg_w, sg_b, ret_logit_f, ret_logit_b,
           w_out, norm2, w_gate, w_up, w_down, norm_f):
    b, l, d = x.shape
    a_width, r_width = GROUPS * LANES, HEADS * HEAD_DIM
    assert w_mod.shape[0] == 1 and b + 1 <= MOD_ROWS and l % 256 == 0 and ctx.shape[1] % CHUNK == 0
    assert w_in.shape[-1] == 2 * a_width + 5 * r_width and d == a_width + r_width
    k_scale = HEAD_DIM ** -0.5
    tile = 256

    cc = jnp.zeros((MOD_ROWS, d), _F32).at[:b].set(c).at[b].set(c_ctx)
    mod = _modulation(cc, w_mod[0], b_mod[0][None, :], block_n=1536)
    mod3 = mod.reshape(MOD_ROWS, 1, 6 * d)

    tables = _retention_tables(ret_logit_f[0], ret_logit_b[0])

    w = w_in[0]
    w_uv = w[:, :2 * a_width].astype(_BF16)
    w_qk = _reorder_head_dims(w[:, 2 * a_width:2 * a_width + 2 * r_width]).astype(_BF16)
    w_vg = w[:, 2 * a_width + 2 * r_width:].astype(_BF16)

    s_cf, s_cb = _context_states(ctx, mod3, norm1, w_qk, w_vg, tables, k_scale)

    rope_c, rope_s = _rope_tables(l)
    sg_bias = jnp.broadcast_to(jnp.transpose(sg_b[0])[:, :, None], (CHUNK, GROUPS, LANES)).reshape(CHUNK, a_width)
    ya, qkvg = _forward_pass(x, mod3, norm1, w_uv, w_qk, w_vg, sg_gain, sg_w[0].astype(_BF16), sg_bias,
                             rope_c, rope_s, tables, s_cf, tile, k_scale)
    return _backward_pass(x, ya, qkvg, mod3, norm2, norm_f[None, :], w_out[0].astype(_BF16),
                          w_gate[0].astype(_BF16), w_up[0].astype(_BF16), w_down[0].astype(_BF16),
                          tables, s_cb, tile)
```

```python
import functools

import numpy as np
import jax
import jax.numpy as jnp
from jax import lax
from jax.experimental import pallas as pl
from jax.experimental.pallas import tpu as pltpu

CHUNK = 128
GRID_W = 64
GROUPS = 4
HEADS = 4
HEAD_DIM = 128
RMS_EPS = 1e-6
ROPE_BASE = 10000.0
LANES = 128
MOD_ROWS = 8
TILE = 512
FFN_BLOCK = 1024
VMEM_LIMIT_BYTES = 56 * 1024 * 1024

T_DMASK_F, T_DMASK_B, T_XI_F, T_XI_B, T_ZETA_F, T_ZETA_B, T_DECAY_F, T_DECAY_B = range(8)

_BF16 = jnp.bfloat16
_F32 = jnp.float32


def _dot(a, b):
    return jnp.dot(a, b, preferred_element_type=_F32)


def _dot_nt(a, b):
    return lax.dot_general(a, b, (((1,), (1,)), ((), ())), preferred_element_type=_F32)


def _dot_tn(a, b):
    return lax.dot_general(a, b, (((0,), (0,)), ((), ())), preferred_element_type=_F32)


def _silu(x):
    return x * (1.0 / (1.0 + jnp.exp(-x)))


def _gelu_tanh(x):
    c = np.float32(np.sqrt(2.0 / np.pi))
    return x * (0.5 * (1.0 + jnp.tanh(c * (x + 0.044715 * (x * x * x)))))


def _rms(x):
    return x * lax.rsqrt(jnp.mean(x * x, axis=-1, keepdims=True) + RMS_EPS)


def _head(a, h):
    return a[:, h * HEAD_DIM:(h + 1) * HEAD_DIM]


def _mod_kernel(c_ref, w_ref, b_ref, o_ref):
    o_ref[...] = _dot(_silu(c_ref[...]), w_ref[...]) + b_ref[...]


def _modulation(cc, w_mod, b_mod, block_n):
    d, n = w_mod.shape
    return pl.pallas_call(
        _mod_kernel,
        grid=(n // block_n,),
        in_specs=[pl.BlockSpec((MOD_ROWS, d), lambda j: (0, 0)),
                  pl.BlockSpec((d, block_n), lambda j: (0, j)),
                  pl.BlockSpec((1, block_n), lambda j: (0, j))],
        out_specs=pl.BlockSpec((MOD_ROWS, block_n), lambda j: (0, j)),
        out_shape=jax.ShapeDtypeStruct((MOD_ROWS, n), _F32),
        name="mod",
    )(cc, w_mod, b_mod)


def _tables_kernel(lf_ref, lb_ref, o_ref):
    def log_sigmoid(x):
        return -(jnp.maximum(-x, 0.0) + jnp.log1p(jnp.exp(-jnp.abs(x))))

    shape = (CHUNK, HEADS * HEAD_DIM)
    lg_f = jnp.broadcast_to(log_sigmoid(lf_ref[...]), shape)
    lg_b = jnp.broadcast_to(log_sigmoid(lb_ref[...]), shape)
    i = lax.broadcasted_iota(jnp.int32, shape, 0).astype(_F32)
    j = (lax.broadcasted_iota(jnp.int32, shape, 1) & (HEAD_DIM - 1)).astype(_F32)
    o_ref[T_DMASK_F] = jnp.where(i >= j, jnp.exp(lg_f * jnp.maximum(i - j, 0.0)), 0.0)
    o_ref[T_DMASK_B] = jnp.where(j >= i, jnp.exp(lg_b * jnp.maximum(j - i, 0.0)), 0.0)
    o_ref[T_XI_F] = jnp.exp(lg_f * (i + 1.0))
    o_ref[T_XI_B] = jnp.exp(lg_b * (CHUNK - i))
    o_ref[T_ZETA_F] = jnp.exp(lg_f * (CHUNK - 1.0 - i))
    o_ref[T_ZETA_B] = jnp.exp(lg_b * i)
    o_ref[T_DECAY_F] = jnp.exp(lg_f * CHUNK)
    o_ref[T_DECAY_B] = jnp.exp(lg_b * CHUNK)


def _retention_tables(logit_f, logit_b):
    rep = lambda l: jnp.repeat(l.astype(_F32), HEAD_DIM)[None, :]
    return pl.pallas_call(
        _tables_kernel,
        out_shape=jax.ShapeDtypeStruct((8, CHUNK, HEADS * HEAD_DIM), _F32),
        name="tables",
    )(rep(logit_f), rep(logit_b))


def _ctx_kernel(ctx_ref, sh_ref, sc_ref, n1_ref, wk_ref, wv_ref, tab_ref, sf_ref, sb_ref, *, n_chunks, k_scale):
    hc = ((_rms(ctx_ref[...]) * n1_ref[...]) * (1.0 + sc_ref[...]) + sh_ref[...]).astype(_BF16)
    k = (_dot(hc, wk_ref[...]) * k_scale).astype(_BF16)
    v = _dot(hc, wv_ref[...])
    for h in range(HEADS):
        s_f = jnp.zeros((HEAD_DIM, HEAD_DIM), _F32)
        s_b = jnp.zeros((HEAD_DIM, HEAD_DIM), _F32)
        for n in range(n_chunks):
            rows = slice(n * CHUNK, (n + 1) * CHUNK)
            upd = _dot_tn(_head(k[rows], h), (_head(v[rows], h) * _head(tab_ref[T_ZETA_F], h)).astype(_BF16))
            s_f = _head(tab_ref[T_DECAY_F], h) * s_f + upd
        for n in reversed(range(n_chunks)):
            rows = slice(n * CHUNK, (n + 1) * CHUNK)
            upd = _dot_tn(_head(k[rows], h), (_head(v[rows], h) * _head(tab_ref[T_ZETA_B], h)).astype(_BF16))
            s_b = _head(tab_ref[T_DECAY_B], h) * s_b + upd
        sf_ref[h] = s_f
        sb_ref[h] = s_b


def _context_states(ctx, mod3, norm1, w_qk, w_vg, tables, k_scale):
    b, lc, d = ctx.shape
    r_width = HEADS * HEAD_DIM
    state = jax.ShapeDtypeStruct((b, HEADS, HEAD_DIM, HEAD_DIM), _F32)
    ctx_row = b
    return pl.pallas_call(
        functools.partial(_ctx_kernel, n_chunks=lc // CHUNK, k_scale=k_scale),
        grid=(b,),
        in_specs=[pl.BlockSpec((None, lc, d), lambda i: (i, 0, 0)),
                  pl.BlockSpec((None, 1, d), lambda i: (ctx_row, 0, 0)),
                  pl.BlockSpec((None, 1, d), lambda i: (ctx_row, 0, 1)),
                  pl.BlockSpec((1, d), lambda i: (0, 0)),
                  pl.BlockSpec((d, r_width), lambda i: (0, 1)),
                  pl.BlockSpec((d, r_width), lambda i: (0, 0)),
                  pl.BlockSpec(tables.shape, lambda i: (0, 0, 0))],
        out_specs=[pl.BlockSpec((None, HEADS, HEAD_DIM, HEAD_DIM), lambda i: (i, 0, 0, 0))] * 2,
        out_shape=[state, state],
        compiler_params=pltpu.CompilerParams(dimension_semantics=("arbitrary",),
                                             vmem_limit_bytes=VMEM_LIMIT_BYTES),
        name="ctx",
    )(ctx, mod3, mod3, norm1, w_qk, w_vg, tables)


def _retention_tile(load, chunk_order, s_ref, tab_ref, t_dmask, t_xi, t_zeta, t_decay, emit, out):
    keys = [(n, h) for n in chunk_order for h in range(HEADS)]
    q, k, v, gate = {}, {}, {}, {}
    for n in chunk_order:
        q[n], k[n], v[n], gate[n] = load(n)
    scores = {(n, h): _dot_nt(_head(q[n], h), _head(k[n], h)) for n, h in keys}
    upd = {(n, h): _dot_tn(_head(k[n], h), (_head(v[n], h) * _head(tab_ref[t_zeta], h)).astype(_BF16))
           for n, h in keys}
    yield
    state = [s_ref[h] for h in range(HEADS)]
    lhs, rhs = {}, {}
    for n, h in keys:
        p = (scores[n, h] * _head(tab_ref[t_dmask], h)).astype(_BF16)
        xq = (_head(q[n], h).astype(_F32) * _head(tab_ref[t_xi], h)).astype(_BF16)
        lhs[n, h] = jnp.concatenate([p, xq], axis=1)
        rhs[n, h] = jnp.concatenate([_head(v[n], h).astype(_BF16), state[h].astype(_BF16)], axis=0)
        state[h] = _head(tab_ref[t_decay], h) * state[h] + upd[n, h]
    yield
    o = {key: _dot(lhs[key], rhs[key]) for key in keys}
    yield
    for n in chunk_order:
        emit(n, jnp.concatenate([_silu(_head(gate[n], h)) * _rms(o[n, h]) for h in range(HEADS)], axis=-1))
    for h in range(HEADS):
        out.add(s_ref, h, state[h])


def _interleave(*stages):
    stages = list(stages)
    while stages:
        for g in list(stages):
            try:
                next(g)
            except StopIteration:
                stages.remove(g)


class _DeferredStores:
    def __init__(self):
        self._items = []

    def add(self, ref, idx, value):
        self._items.append((ref, idx, value))

    def commit(self):
        for ref, idx, value in self._items:
            ref[idx] = value


def _skewed_steps(first, middle, last):
    s = pl.program_id(1)
    n_tiles = pl.num_programs(1) - 1

    @pl.when(s == 0)
    def _():
        first(0)

    for parity in range(2):
        @pl.when((s > 0) & (s < n_tiles) & ((s & 1) == parity))
        def _():
            middle(parity)

    for parity in range(2):
        @pl.when((s == n_tiles) & ((s & 1) == parity))
        def _():
            last(1 - parity)


def _fwd_kernel(x_ref, sh_ref, sc_ref, n1_ref, wuv_ref, wqk_ref, wvg_ref, gain_ref, sgw_ref, sgb_ref, rc_ref, rs_ref,
                tab_ref, s0_ref, ya_ref, qkvg_ref, s_ref, f32_ref, b16_ref, *, n_chunks, k_scale):
    a_width = GROUPS * LANES
    r_width = HEADS * HEAD_DIM
    U, VR, GF = 0, a_width, a_width + r_width
    VG, Q, K = 0, a_width, a_width + r_width
    rows_all = slice(None)

    def project(slot, out):
        hx = ((_rms(x_ref[...]) * n1_ref[...]) * (1.0 + sc_ref[...]) + sh_ref[...]).astype(_BF16)

        def proj(w_ref, lo, width):
            return _dot(hx, w_ref[:, lo:lo + width])

        yield
        out.add(f32_ref, (slot, rows_all, slice(U, U + a_width)), _gelu_tanh(proj(wuv_ref, 0, a_width)))
        v = _gelu_tanh(proj(wuv_ref, a_width, a_width))
        vg = jnp.concatenate([(_rms(_head(v, g)) * _head(gain_ref[...], g)).astype(_BF16) for g in range(GROUPS)],
                             axis=-1)
        out.add(b16_ref, (slot, rows_all, slice(VG, VG + a_width)), vg)
        yield

        def rope(t):
            return jnp.concatenate(
                [_head(t, h) * rc_ref[...] + pltpu.roll(_head(t, h), HEAD_DIM // 2, 1) * rs_ref[...]
                 for h in range(HEADS)], axis=-1)

        q = rope(proj(wqk_ref, 0, r_width)).astype(_BF16)
        k = (rope(proj(wqk_ref, r_width, r_width)) * k_scale).astype(_BF16)
        out.add(b16_ref, (slot, rows_all, slice(Q, Q + r_width)), q)
        out.add(b16_ref, (slot, rows_all, slice(K, K + r_width)), k)
        out.add(qkvg_ref, (rows_all, slice(0, r_width)), q)
        out.add(qkvg_ref, (rows_all, slice(r_width, 2 * r_width)), k)
        yield
        vr = proj(wvg_ref, 0, r_width)
        out.add(f32_ref, (slot, rows_all, slice(VR, VR + r_width)), vr)
        out.add(f32_ref, (slot, rows_all, slice(GF, GF + r_width)), proj(wvg_ref, r_width, r_width))
        out.add(qkvg_ref, (rows_all, slice(2 * r_width, 3 * r_width)), vr.astype(_BF16))
        out.add(qkvg_ref, (rows_all, slice(3 * r_width, 4 * r_width)),
                proj(wvg_ref, 2 * r_width, r_width).astype(_BF16))

    def gate_mix(slot, out):
        mixed = []
        for g in range(GROUPS):
            vg = jnp.concatenate([b16_ref[slot, n * CHUNK:(n + 1) * CHUNK, VG + g * LANES:VG + (g + 1) * LANES]
                                  for n in range(n_chunks)], axis=1)
            mixed.append(_dot(sgw_ref[g], vg))
        yield
        for g in range(GROUPS):
            cols = slice(g * LANES, (g + 1) * LANES)
            for n in range(n_chunks):
                rows = slice(n * CHUNK, (n + 1) * CHUNK)
                m = mixed[g][:, n * LANES:(n + 1) * LANES] + sgb_ref[:, cols]
                u = f32_ref[slot, rows, U + g * LANES:U + (g + 1) * LANES]
                out.add(ya_ref, (rows, cols), (u * m).astype(_BF16))

    def retain(slot, out):
        def load(n):
            rows = slice(n * CHUNK, (n + 1) * CHUNK)
            return (b16_ref[slot, rows, Q:Q + r_width], b16_ref[slot, rows, K:K + r_width],
                    f32_ref[slot, rows, VR:VR + r_width], f32_ref[slot, rows, GF:GF + r_width])

        def emit(n, y):
            out.add(ya_ref, (slice(n * CHUNK, (n + 1) * CHUNK), slice(a_width, a_width + r_width)), y.astype(_BF16))

        return _retention_tile(load, range(n_chunks), s_ref, tab_ref, T_DMASK_F, T_XI_F, T_ZETA_F, T_DECAY_F,
                               emit, out)

    def first(slot):
        s_ref[...] = s0_ref[...]
        out = _DeferredStores()
        _interleave(project(slot, out))
        out.commit()

    def middle(slot):
        out = _DeferredStores()
        _interleave(retain(1 - slot, out), gate_mix(1 - slot, out), project(slot, out))
        out.commit()

    def last(slot):
        out = _DeferredStores()
        _interleave(retain(slot, out), gate_mix(slot, out))
        out.commit()

    _skewed_steps(first, middle, last)


def _forward_pass(x, mod3, norm1, w_uv, w_qk, w_vg, sg_gain, sg_w, sg_bias, rope_c, rope_s, tables, s_cf, tile,
                  k_scale):
    b, l, d = x.shape
    nt = l // tile
    a_width, r_width = GROUPS * LANES, HEADS * HEAD_DIM
    const2 = lambda i, s: (0, 0)
    cur = lambda i, s: (i, jnp.minimum(s, nt - 1), 0)
    prev = lambda i, s: (i, jnp.maximum(s - 1, 0), 0)
    resident = dict(pipeline_mode=pl.Buffered(1))
    return pl.pallas_call(
        functools.partial(_fwd_kernel, n_chunks=tile // CHUNK, k_scale=k_scale),
        grid=(b, nt + 1),
        in_specs=[pl.BlockSpec((None, tile, d), cur),
                  pl.BlockSpec((None, 1, d), lambda i, s: (i, 0, 0)),
                  pl.BlockSpec((None, 1, d), lambda i, s: (i, 0, 1)),
                  pl.BlockSpec((1, d), const2),
                  pl.BlockSpec(w_uv.shape, const2, **resident),
                  pl.BlockSpec(w_qk.shape, const2, **resident),
                  pl.BlockSpec(w_vg.shape, const2, **resident),
                  pl.BlockSpec((1, a_width), const2),
                  pl.BlockSpec(sg_w.shape, lambda i, s: (0, 0, 0)),
                  pl.BlockSpec(sg_bias.shape, const2),
                  pl.BlockSpec((tile, HEAD_DIM), lambda i, s: (jnp.minimum(s, nt - 1), 0)),
                  pl.BlockSpec((tile, HEAD_DIM), lambda i, s: (jnp.minimum(s, nt - 1), 0)),
                  pl.BlockSpec(tables.shape, lambda i, s: (0, 0, 0)),
                  pl.BlockSpec((None, HEADS, HEAD_DIM, HEAD_DIM), lambda i, s: (i, 0, 0, 0))],
        out_specs=[pl.BlockSpec((None, tile, a_width + r_width), prev),
                   pl.BlockSpec((None, tile, 4 * r_width), cur)],
        out_shape=[jax.ShapeDtypeStruct((b, l, a_width + r_width), _BF16),
                   jax.ShapeDtypeStruct((b, l, 4 * r_width), _BF16)],
        scratch_shapes=[pltpu.VMEM((HEADS, HEAD_DIM, HEAD_DIM), _F32),
                        pltpu.VMEM((2, tile, a_width + 2 * r_width), _F32),
                        pltpu.VMEM((2, tile, a_width + 2 * r_width), _BF16)],
        compiler_params=pltpu.CompilerParams(dimension_semantics=("arbitrary", "arbitrary"),
                                             vmem_limit_bytes=VMEM_LIMIT_BYTES),
        name="fwd",
    )(x, mod3, mod3, norm1, w_uv, w_qk, w_vg, sg_gain, sg_w, sg_bias, rope_c, rope_s, tables, s_cf)


def _bwd_kernel(x_ref, ya_ref, qkvg_ref, g1_ref, sh_ref, sc_ref, g2_ref, n2_ref, nf_ref,
                wout_ref, wg_ref, wu_ref, wd_ref, tab_ref, s0_ref, o_ref, s_ref, yb_ref, *, n_chunks):
    a_width = GROUPS * LANES
    r_width = HEADS * HEAD_DIM

    def retain(slot, out):
        def load(n):
            rows = slice(n * CHUNK, (n + 1) * CHUNK)
            return (qkvg_ref[rows, 0:r_width], qkvg_ref[rows, r_width:2 * r_width],
                    qkvg_ref[rows, 2 * r_width:3 * r_width].astype(_F32),
                    qkvg_ref[rows, 3 * r_width:4 * r_width].astype(_F32))

        def emit(n, y):
            out.add(yb_ref, (slot, slice(n * CHUNK, (n + 1) * CHUNK), slice(None)), y)

        return _retention_tile(load, list(reversed(range(n_chunks))), s_ref, tab_ref,
                               T_DMASK_B, T_XI_B, T_ZETA_B, T_DECAY_B, emit, out)

    def finish(slot, out):
        y_r = ya_ref[:, a_width:].astype(_F32) + yb_ref[slot]
        y = _dot(ya_ref[:, :a_width], wout_ref[:a_width, :]) + _dot(y_r.astype(_BF16), wout_ref[a_width:, :])
        x1 = x_ref[...] + g1_ref[...] * y
        h2 = ((_rms(x1) * n2_ref[...]) * (1.0 + sc_ref[...]) + sh_ref[...]).astype(_BF16)
        acc = None
        d_ff = wg_ref.shape[1]
        for lo in range(0, d_ff, FFN_BLOCK):
            yield
            cols = slice(lo, min(lo + FFN_BLOCK, d_ff))
            act = (_silu(_dot(h2, wg_ref[:, cols])) * _dot(h2, wu_ref[:, cols])).astype(_BF16)
            part = _dot(act, wd_ref[cols, :])
            acc = part if acc is None else acc + part
        x2 = x1 + g2_ref[...] * acc
        out.add(o_ref, (slice(None), slice(None)), _rms(x2) * nf_ref[...])

    def first(slot):
        s_ref[...] = s0_ref[...]
        out = _DeferredStores()
        _interleave(retain(slot, out))
        out.commit()

    def middle(slot):
        out = _DeferredStores()
        _interleave(retain(slot, out), finish(1 - slot, out))
        out.commit()

    def last(slot):
        out = _DeferredStores()
        _interleave(finish(slot, out))
        out.commit()

    _skewed_steps(first, middle, last)


def _backward_pass(x, ya, qkvg, mod3, norm2, norm_f, w_out, w_gate, w_up, w_down, tables, s_cb, tile):
    b, l, d = x.shape
    nt = l // tile
    const2 = lambda i, s: (0, 0)
    cur = lambda i, s: (i, nt - 1 - jnp.minimum(s, nt - 1), 0)
    prev = lambda i, s: (i, nt - 1 - jnp.maximum(s - 1, 0), 0)
    resident = dict(pipeline_mode=pl.Buffered(1))
    mod_spec = lambda j: pl.BlockSpec((None, 1, d), lambda i, s: (i, 0, j))
    return pl.pallas_call(
        functools.partial(_bwd_kernel, n_chunks=tile // CHUNK),
        grid=(b, nt + 1),
        in_specs=[pl.BlockSpec((None, tile, d), prev),
                  pl.BlockSpec((None, tile, ya.shape[-1]), prev),
                  pl.BlockSpec((None, tile, qkvg.shape[-1]), cur),
                  mod_spec(2), mod_spec(3), mod_spec(4), mod_spec(5),
                  pl.BlockSpec((1, d), const2),
                  pl.BlockSpec((1, d), const2),
                  pl.BlockSpec(w_out.shape, const2, **resident),
                  pl.BlockSpec(w_gate.shape, const2, **resident),
                  pl.BlockSpec(w_up.shape, const2, **resident),
                  pl.BlockSpec(w_down.shape, const2, **resident),
                  pl.BlockSpec(tables.shape, lambda i, s: (0, 0, 0)),
                  pl.BlockSpec((None, HEADS, HEAD_DIM, HEAD_DIM), lambda i, s: (i, 0, 0, 0))],
        out_specs=pl.BlockSpec((None, tile, d), prev),
        out_shape=jax.ShapeDtypeStruct((b, l, d), x.dtype),
        scratch_shapes=[pltpu.VMEM((HEADS, HEAD_DIM, HEAD_DIM), _F32),
                        pltpu.VMEM((2, tile, HEADS * HEAD_DIM), _F32)],
        compiler_params=pltpu.CompilerParams(dimension_semantics=("arbitrary", "arbitrary"),
                                             vmem_limit_bytes=VMEM_LIMIT_BYTES),
        name="bwd",
    )(x, ya, qkvg, mod3, mod3, mod3, mod3, norm2, norm_f, w_out, w_gate, w_up, w_down, tables, s_cb)


def _rope_tables(l):
    n_freq = HEAD_DIM // 4
    inv = np.float32(ROPE_BASE) ** (-np.arange(n_freq, dtype=np.float32) / np.float32(n_freq))
    rows = np.repeat(np.arange(l // GRID_W, dtype=np.float32), GRID_W)
    cols = np.tile(np.arange(GRID_W, dtype=np.float32), l // GRID_W)
    ang_r, ang_c = rows[:, None] * inv[None, :], cols[:, None] * inv[None, :]
    cos = np.concatenate([np.cos(ang_r), np.cos(ang_c)], axis=-1)
    sin = np.concatenate([np.sin(ang_r), np.sin(ang_c)], axis=-1)
    return (np.concatenate([cos, cos], axis=-1).astype(np.float32),
            np.concatenate([-sin, sin], axis=-1).astype(np.float32))


def _reorder_head_dims(w):
    d, n = w.shape
    q = HEAD_DIM // 4
    return w.reshape(d, n // HEAD_DIM, 2, 2, q).transpose(0, 1, 3, 2, 4).reshape(d, n)


def kernel(x, c, ctx, c_ctx, w_mod, b_mod, norm1, w_in, sg_gain, sg_w, sg_b, ret_logit_f, ret_logit_b,
           w_out, norm2, w_gate, w_up, w_down, norm_f):
    b, l, d = x.shape
    a_width, r_width = GROUPS * LANES, HEADS * HEAD_DIM
    tile = min(TILE, l)
    assert w_mod.shape[0] == 1 and b + 1 <= MOD_ROWS and l % tile == 0 and tile % CHUNK == 0
    assert ctx.shape[1] % CHUNK == 0 and l // tile >= 2
    assert w_in.shape[-1] == 2 * a_width + 5 * r_width and d == a_width + r_width
    k_scale = HEAD_DIM ** -0.5

    cc = jnp.zeros((MOD_ROWS, d), _F32).at[:b].set(c).at[b].set(c_ctx)
    mod = _modulation(cc, w_mod[0], b_mod[0][None, :], block_n=1536)
    mod3 = mod.reshape(MOD_ROWS, 1, 6 * d)

    tables = _retention_tables(ret_logit_f[0], ret_logit_b[0])

    w = w_in[0]
    w_uv = w[:, :2 * a_width].astype(_BF16)
    w_qk = _reorder_head_dims(w[:, 2 * a_width:2 * a_width + 2 * r_width]).astype(_BF16)
    w_vg = w[:, 2 * a_width + 2 * r_width:].astype(_BF16)

    s_cf, s_cb = _context_states(ctx, mod3, norm1, w_qk, w_vg, tables, k_scale)

    rope_c, rope_s = _rope_tables(l)
    sg_bias = jnp.broadcast_to(jnp.transpose(sg_b[0])[:, :, None], (CHUNK, GROUPS, LANES)).reshape(CHUNK, a_width)
    ya, qkvg = _forward_pass(x, mod3, norm1, w_uv, w_qk, w_vg, sg_gain, sg_w[0].astype(_BF16), sg_bias,
                             rope_c, rope_s, tables, s_cf, tile, k_scale)
    return _backward_pass(x, ya, qkvg, mod3, norm2, norm_f[None, :], w_out[0].astype(_BF16),
                          w_gate[0].astype(_BF16), w_up[0].astype(_BF16), w_down[0].astype(_BF16),
                          tables, s_cb, tile)
```

```python
import functools

import numpy as np
import jax
import jax.numpy as jnp
from jax import lax
from jax.experimental import pallas as pl
from jax.experimental.pallas import tpu as pltpu

CHUNK = 128
GRID_W = 64
GROUPS = 4
HEADS = 4
HEAD_DIM = 128
RMS_EPS = 1e-6
ROPE_BASE = 10000.0
LANES = 128
MOD_ROWS = 8
TILE = 512
FFN_BLOCK = 1024
VMEM_LIMIT_BYTES = 56 * 1024 * 1024

T_DMASK_F, T_DMASK_B, T_XI_F, T_XI_B, T_ZETA_F, T_ZETA_B, T_DECAY_F, T_DECAY_B = range(8)

_BF16 = jnp.bfloat16
_F32 = jnp.float32


def _dot(a, b):
    return jnp.dot(a, b, preferred_element_type=_F32)


def _dot_nt(a, b):
    return lax.dot_general(a, b, (((1,), (1,)), ((), ())), preferred_element_type=_F32)


def _dot_tn(a, b):
    return lax.dot_general(a, b, (((0,), (0,)), ((), ())), preferred_element_type=_F32)


def _silu(x):
    return x * (1.0 / (1.0 + jnp.exp(-x)))


def _gelu_tanh(x):
    c = float(np.sqrt(2.0 / np.pi))
    half = 0.5 * x
    return half + half * jnp.tanh(x * (c + (0.044715 * c) * (x * x)))


def _rms(x):
    return x * lax.rsqrt(jnp.mean(x * x, axis=-1, keepdims=True) + RMS_EPS)


def _head(a, h):
    return a[:, h * HEAD_DIM:(h + 1) * HEAD_DIM]


def _mod_kernel(c_ref, w_ref, b_ref, o_ref):
    o_ref[...] = _dot(_silu(c_ref[...]), w_ref[...]) + b_ref[...]


def _modulation(cc, w_mod, b_mod, block_n):
    d, n = w_mod.shape
    return pl.pallas_call(
        _mod_kernel,
        grid=(n // block_n,),
        in_specs=[pl.BlockSpec((MOD_ROWS, d), lambda j: (0, 0)),
                  pl.BlockSpec((d, block_n), lambda j: (0, j)),
                  pl.BlockSpec((1, block_n), lambda j: (0, j))],
        out_specs=pl.BlockSpec((MOD_ROWS, block_n), lambda j: (0, j)),
        out_shape=jax.ShapeDtypeStruct((MOD_ROWS, n), _F32),
        name="mod",
    )(cc, w_mod, b_mod)


def _tables_kernel(lf_ref, lb_ref, o_ref):
    def log_sigmoid(x):
        return -(jnp.maximum(-x, 0.0) + jnp.log1p(jnp.exp(-jnp.abs(x))))

    shape = (CHUNK, HEADS * HEAD_DIM)
    lg_f = jnp.broadcast_to(log_sigmoid(lf_ref[...]), shape)
    lg_b = jnp.broadcast_to(log_sigmoid(lb_ref[...]), shape)
    i = lax.broadcasted_iota(jnp.int32, shape, 0).astype(_F32)
    j = (lax.broadcasted_iota(jnp.int32, shape, 1) & (HEAD_DIM - 1)).astype(_F32)
    o_ref[T_DMASK_F] = jnp.where(i >= j, jnp.exp(lg_f * jnp.maximum(i - j, 0.0)), 0.0)
    o_ref[T_DMASK_B] = jnp.where(j >= i, jnp.exp(lg_b * jnp.maximum(j - i, 0.0)), 0.0)
    o_ref[T_XI_F] = jnp.exp(lg_f * (i + 1.0))
    o_ref[T_XI_B] = jnp.exp(lg_b * (CHUNK - i))
    o_ref[T_ZETA_F] = jnp.exp(lg_f * (CHUNK - 1.0 - i))
    o_ref[T_ZETA_B] = jnp.exp(lg_b * i)
    o_ref[T_DECAY_F] = jnp.exp(lg_f * CHUNK)
    o_ref[T_DECAY_B] = jnp.exp(lg_b * CHUNK)


def _retention_tables(logit_f, logit_b):
    rep = lambda l: jnp.repeat(l.astype(_F32), HEAD_DIM)[None, :]
    return pl.pallas_call(
        _tables_kernel,
        out_shape=jax.ShapeDtypeStruct((8, CHUNK, HEADS * HEAD_DIM), _F32),
        name="tables",
    )(rep(logit_f), rep(logit_b))


def _ctx_kernel(ctx_ref, sh_ref, sc_ref, n1_ref, wk_ref, wv_ref, tab_ref, sf_ref, sb_ref, *, n_chunks, k_scale):
    hc = ((_rms(ctx_ref[...]) * n1_ref[...]) * (1.0 + sc_ref[...]) + sh_ref[...]).astype(_BF16)
    k = (_dot(hc, wk_ref[...]) * k_scale).astype(_BF16)
    v = _dot(hc, wv_ref[...])
    for h in range(HEADS):
        s_f = jnp.zeros((HEAD_DIM, HEAD_DIM), _F32)
        s_b = jnp.zeros((HEAD_DIM, HEAD_DIM), _F32)
        for n in range(n_chunks):
            rows = slice(n * CHUNK, (n + 1) * CHUNK)
            upd = _dot_tn(_head(k[rows], h), (_head(v[rows], h) * _head(tab_ref[T_ZETA_F], h)).astype(_BF16))
            s_f = _head(tab_ref[T_DECAY_F], h) * s_f + upd
        for n in reversed(range(n_chunks)):
            rows = slice(n * CHUNK, (n + 1) * CHUNK)
            upd = _dot_tn(_head(k[rows], h), (_head(v[rows], h) * _head(tab_ref[T_ZETA_B], h)).astype(_BF16))
            s_b = _head(tab_ref[T_DECAY_B], h) * s_b + upd
        sf_ref[h] = s_f
        sb_ref[h] = s_b


def _context_states(ctx, mod3, norm1, w_qk, w_vg, tables, k_scale):
    b, lc, d = ctx.shape
    r_width = HEADS * HEAD_DIM
    state = jax.ShapeDtypeStruct((b, HEADS, HEAD_DIM, HEAD_DIM), _F32)
    ctx_row = b
    return pl.pallas_call(
        functools.partial(_ctx_kernel, n_chunks=lc // CHUNK, k_scale=k_scale),
        grid=(b,),
        in_specs=[pl.BlockSpec((None, lc, d), lambda i: (i, 0, 0)),
                  pl.BlockSpec((None, 1, d), lambda i: (ctx_row, 0, 0)),
                  pl.BlockSpec((None, 1, d), lambda i: (ctx_row, 0, 1)),
                  pl.BlockSpec((1, d), lambda i: (0, 0)),
                  pl.BlockSpec((d, r_width), lambda i: (0, 1)),
                  pl.BlockSpec((d, r_width), lambda i: (0, 0)),
                  pl.BlockSpec(tables.shape, lambda i: (0, 0, 0))],
        out_specs=[pl.BlockSpec((None, HEADS, HEAD_DIM, HEAD_DIM), lambda i: (i, 0, 0, 0))] * 2,
        out_shape=[state, state],
        compiler_params=pltpu.CompilerParams(dimension_semantics=("arbitrary",),
                                             vmem_limit_bytes=VMEM_LIMIT_BYTES),
        name="ctx",
    )(ctx, mod3, mod3, norm1, w_qk, w_vg, tables)


def _retention_tile(load, chunk_order, s_ref, tab_ref, t_dmask, t_decay, emit, out):
    keys = [(n, h) for n in chunk_order for h in range(HEADS)]
    q, k, xq, v, vz, gate = {}, {}, {}, {}, {}, {}
    for n in chunk_order:
        q[n], k[n], xq[n], v[n], vz[n], gate[n] = load(n)
    scores = {(n, h): _dot_nt(_head(q[n], h), _head(k[n], h)) for n, h in keys}
    upd = {(n, h): _dot_tn(_head(k[n], h), _head(vz[n], h)) for n, h in keys}
    yield
    state = [s_ref[h] for h in range(HEADS)]
    lhs, rhs = {}, {}
    for n, h in keys:
        p = (scores[n, h] * _head(tab_ref[t_dmask], h)).astype(_BF16)
        lhs[n, h] = jnp.concatenate([p, _head(xq[n], h)], axis=1)
        rhs[n, h] = jnp.concatenate([_head(v[n], h), state[h].astype(_BF16)], axis=0)
        state[h] = _head(tab_ref[t_decay], h) * state[h] + upd[n, h]
    yield
    o = {key: _dot(lhs[key], rhs[key]) for key in keys}
    yield
    for n in chunk_order:
        emit(n, jnp.concatenate([_silu(_head(gate[n], h)) * _rms(o[n, h]) for h in range(HEADS)], axis=-1))
    for h in range(HEADS):
        out.add(s_ref, h, state[h])


def _interleave(*stages):
    stages = list(stages)
    while stages:
        for g in list(stages):
            try:
                next(g)
            except StopIteration:
                stages.remove(g)


class _DeferredStores:
    def __init__(self):
        self._items = []

    def add(self, ref, idx, value):
        self._items.append((ref, idx, value))

    def commit(self):
        for ref, idx, value in self._items:
            ref[idx] = value


def _skewed_steps(first, middle, last):
    s = pl.program_id(1)
    n_tiles = pl.num_programs(1) - 1

    @pl.when(s == 0)
    def _():
        first(0)

    for parity in range(2):
        @pl.when((s > 0) & (s < n_tiles) & ((s & 1) == parity))
        def _():
            middle(parity)

    for parity in range(2):
        @pl.when((s == n_tiles) & ((s & 1) == parity))
        def _():
            last(1 - parity)


def _fwd_kernel(x_ref, sh_ref, sc_ref, n1_ref, wuv_ref, wqk_ref, wvg_ref, gain_ref, sgw_ref, sgb_ref,
                rcq_ref, rsq_ref, rck_ref, rsk_ref, tab_ref, s0_ref, wo32_ref, wg32_ref, wu32_ref, wd32_ref,
                ya_ref, qkvg_ref, wo16_ref, wg16_ref, wu16_ref, wd16_ref, s_ref, f32_ref, b16_ref, *, n_chunks):
    a_width = GROUPS * LANES
    r_width = HEADS * HEAD_DIM
    U, GF = 0, a_width
    VG, Q, K, XQ, VB, VZ = (i * r_width for i in range(6))
    rows_all = slice(None)

    def per_chunk(t, table):
        return jnp.concatenate([t[n * CHUNK:(n + 1) * CHUNK] * table for n in range(n_chunks)], axis=0)

    def convert(out):
        for src, dst in ((wo32_ref, wo16_ref), (wg32_ref, wg16_ref), (wu32_ref, wu16_ref), (wd32_ref, wd16_ref)):
            out.add(dst, (rows_all, rows_all), src[...].astype(_BF16))

    def project(slot, out):
        x = x_ref[...]
        r = lax.rsqrt(jnp.mean(x * x, axis=-1, keepdims=True) + RMS_EPS)
        hx = ((x * r) * (n1_ref[...] * (1.0 + sc_ref[...])) + sh_ref[...]).astype(_BF16)

        def proj(w_ref, lo, width):
            return _dot(hx, w_ref[:, lo:lo + width])

        yield
        out.add(f32_ref, (slot, rows_all, slice(U, U + a_width)), _gelu_tanh(proj(wuv_ref, 0, a_width)))
        v = _gelu_tanh(proj(wuv_ref, a_width, a_width))
        vg = jnp.concatenate([(_rms(_head(v, g)) * _head(gain_ref[...], g)).astype(_BF16) for g in range(GROUPS)],
                             axis=-1)
        out.add(b16_ref, (slot, rows_all, slice(VG, VG + a_width)), vg)
        yield

        def rope(t, c_ref, s_ref_):
            return jnp.concatenate(
                [_head(t, h) * c_ref[...] + pltpu.roll(_head(t, h), HEAD_DIM // 2, 1) * s_ref_[...]
                 for h in range(HEADS)], axis=-1)

        q = rope(proj(wqk_ref, 0, r_width), rcq_ref, rsq_ref)
        q16 = q.astype(_BF16)
        k16 = rope(proj(wqk_ref, r_width, r_width), rck_ref, rsk_ref).astype(_BF16)
        out.add(b16_ref, (slot, rows_all, slice(Q, Q + r_width)), q16)
        out.add(b16_ref, (slot, rows_all, slice(K, K + r_width)), k16)
        out.add(b16_ref, (slot, rows_all, slice(XQ, XQ + r_width)), per_chunk(q, tab_ref[T_XI_F]).astype(_BF16))
        out.add(qkvg_ref, (rows_all, slice(0, r_width)), q16)
        out.add(qkvg_ref, (rows_all, slice(r_width, 2 * r_width)), k16)
        yield
        vr = proj(wvg_ref, 0, r_width)
        v16 = vr.astype(_BF16)
        out.add(b16_ref, (slot, rows_all, slice(VB, VB + r_width)), v16)
        out.add(b16_ref, (slot, rows_all, slice(VZ, VZ + r_width)), per_chunk(vr, tab_ref[T_ZETA_F]).astype(_BF16))
        out.add(f32_ref, (slot, rows_all, slice(GF, GF + r_width)), proj(wvg_ref, r_width, r_width))
        out.add(qkvg_ref, (rows_all, slice(2 * r_width, 3 * r_width)), v16)
        out.add(qkvg_ref, (rows_all, slice(3 * r_width, 4 * r_width)),
                proj(wvg_ref, 2 * r_width, r_width).astype(_BF16))

    def gate_mix(slot, out):
        mixed = []
        for g in range(GROUPS):
            vg = jnp.concatenate([b16_ref[slot, n * CHUNK:(n + 1) * CHUNK, VG + g * LANES:VG + (g + 1) * LANES]
                                  for n in range(n_chunks)], axis=1)
            mixed.append(_dot(sgw_ref[g], vg))
        yield
        for g in range(GROUPS):
            cols = slice(g * LANES, (g + 1) * LANES)
            for n in range(n_chunks):
                rows = slice(n * CHUNK, (n + 1) * CHUNK)
                m = mixed[g][:, n * LANES:(n + 1) * LANES] + sgb_ref[:, cols]
                u = f32_ref[slot, rows, U + g * LANES:U + (g + 1) * LANES]
                out.add(ya_ref, (rows, cols), (u * m).astype(_BF16))

    def retain(slot, out):
        def load(n):
            rows = slice(n * CHUNK, (n + 1) * CHUNK)
            return tuple(b16_ref[slot, rows, c:c + r_width] for c in (Q, K, XQ, VB, VZ)) + (
                f32_ref[slot, rows, GF:GF + r_width],)

        def emit(n, y):
            out.add(ya_ref, (slice(n * CHUNK, (n + 1) * CHUNK), slice(a_width, a_width + r_width)), y.astype(_BF16))

        return _retention_tile(load, range(n_chunks), s_ref, tab_ref, T_DMASK_F, T_DECAY_F, emit, out)

    def first(slot):
        s_ref[...] = s0_ref[...]
        out = _DeferredStores()
        convert(out)
        _interleave(project(slot, out))
        out.commit()

    def middle(slot):
        out = _DeferredStores()
        convert(out)
        _interleave(retain(1 - slot, out), gate_mix(1 - slot, out), project(slot, out))
        out.commit()

    def last(slot):
        out = _DeferredStores()
        convert(out)
        _interleave(retain(slot, out), gate_mix(slot, out))
        out.commit()

    _skewed_steps(first, middle, last)


def _forward_pass(x, mod3, norm1, w_uv, w_qk, w_vg, sg_gain, sg_w, sg_bias, rope, tables, s_cf, bwd_weights, tile):
    b, l, d = x.shape
    nt = l // tile
    a_width, r_width = GROUPS * LANES, HEADS * HEAD_DIM
    const2 = lambda i, s: (0, 0)
    cur = lambda i, s: (i, jnp.minimum(s, nt - 1), 0)
    prev = lambda i, s: (i, jnp.maximum(s - 1, 0), 0)
    rope_spec = pl.BlockSpec((tile, HEAD_DIM), lambda i, s: (jnp.minimum(s, nt - 1), 0))
    resident = dict(pipeline_mode=pl.Buffered(1))
    n_blk = max(n for n in (1, 2, 4, 8, 16) if n <= b * (nt + 1))
    blk = lambda i, s: (jnp.minimum(i * (nt + 1) + s, n_blk - 1), 0)
    w_specs = [pl.BlockSpec((w.shape[0] // n_blk, w.shape[1]), blk) for w in bwd_weights]
    return pl.pallas_call(
        functools.partial(_fwd_kernel, n_chunks=tile // CHUNK),
        grid=(b, nt + 1),
        in_specs=[pl.BlockSpec((None, tile, d), cur),
                  pl.BlockSpec((None, 1, d), lambda i, s: (i, 0, 0)),
                  pl.BlockSpec((None, 1, d), lambda i, s: (i, 0, 1)),
                  pl.BlockSpec((1, d), const2),
                  pl.BlockSpec(w_uv.shape, const2, **resident),
                  pl.BlockSpec(w_qk.shape, const2, **resident),
                  pl.BlockSpec(w_vg.shape, const2, **resident),
                  pl.BlockSpec((1, a_width), const2),
                  pl.BlockSpec(sg_w.shape, lambda i, s: (0, 0, 0)),
                  pl.BlockSpec(sg_bias.shape, const2),
                  rope_spec, rope_spec, rope_spec, rope_spec,
                  pl.BlockSpec(tables.shape, lambda i, s: (0, 0, 0)),
                  pl.BlockSpec((None, HEADS, HEAD_DIM, HEAD_DIM), lambda i, s: (i, 0, 0, 0))] + w_specs,
        out_specs=[pl.BlockSpec((None, tile, a_width + r_width), prev),
                   pl.BlockSpec((None, tile, 4 * r_width), cur)] + w_specs,
        out_shape=[jax.ShapeDtypeStruct((b, l, a_width + r_width), _BF16),
                   jax.ShapeDtypeStruct((b, l, 4 * r_width), _BF16)]
                  + [jax.ShapeDtypeStruct(w.shape, _BF16) for w in bwd_weights],
        scratch_shapes=[pltpu.VMEM((HEADS, HEAD_DIM, HEAD_DIM), _F32),
                        pltpu.VMEM((2, tile, a_width + r_width), _F32),
                        pltpu.VMEM((2, tile, 6 * r_width), _BF16)],
        compiler_params=pltpu.CompilerParams(dimension_semantics=("arbitrary", "arbitrary"),
                                             vmem_limit_bytes=VMEM_LIMIT_BYTES),
        name="fwd",
    )(x, mod3, mod3, norm1, w_uv, w_qk, w_vg, sg_gain, sg_w, sg_bias, *rope, tables, s_cf, *bwd_weights)


def _bwd_kernel(x_ref, ya_ref, qkvg_ref, g1_ref, sh_ref, sc_ref, g2_ref, n2_ref, nf_ref,
                wout_ref, wg_ref, wu_ref, wd_ref, tab_ref, s0_ref, o_ref, s_ref, yb_ref, *, n_chunks):
    a_width = GROUPS * LANES
    r_width = HEADS * HEAD_DIM

    def retain(slot, out):
        def load(n):
            rows = slice(n * CHUNK, (n + 1) * CHUNK)
            q, k, v, gate = (qkvg_ref[rows, i * r_width:(i + 1) * r_width] for i in range(4))
            return (q, k, (q.astype(_F32) * tab_ref[T_XI_B]).astype(_BF16),
                    v, (v.astype(_F32) * tab_ref[T_ZETA_B]).astype(_BF16), gate.astype(_F32))

        def emit(n, y):
            out.add(yb_ref, (slot, slice(n * CHUNK, (n + 1) * CHUNK), slice(None)), y)

        return _retention_tile(load, list(reversed(range(n_chunks))), s_ref, tab_ref, T_DMASK_B, T_DECAY_B,
                               emit, out)

    def finish(slot, out):
        y_r = ya_ref[:, a_width:].astype(_F32) + yb_ref[slot]
        y = _dot(ya_ref[:, :a_width], wout_ref[:a_width, :]) + _dot(y_r.astype(_BF16), wout_ref[a_width:, :])
        x1 = x_ref[...] + g1_ref[...] * y
        h2 = ((_rms(x1) * n2_ref[...]) * (1.0 + sc_ref[...]) + sh_ref[...]).astype(_BF16)
        acc = None
        d_ff = wg_ref.shape[1]
        for lo in range(0, d_ff, FFN_BLOCK):
            yield
            cols = slice(lo, min(lo + FFN_BLOCK, d_ff))
            act = (_silu(_dot(h2, wg_ref[:, cols])) * _dot(h2, wu_ref[:, cols])).astype(_BF16)
            part = _dot(act, wd_ref[cols, :])
            acc = part if acc is None else acc + part
        x2 = x1 + g2_ref[...] * acc
        out.add(o_ref, (slice(None), slice(None)), _rms(x2) * nf_ref[...])

    def first(slot):
        s_ref[...] = s0_ref[...]
        out = _DeferredStores()
        _interleave(retain(slot, out))
        out.commit()

    def middle(slot):
        out = _DeferredStores()
        _interleave(retain(slot, out), finish(1 - slot, out))
        out.commit()

    def last(slot):
        out = _DeferredStores()
        _interleave(finish(slot, out))
        out.commit()

    _skewed_steps(first, middle, last)


def _backward_pass(x, ya, qkvg, mod3, norm2, norm_f, w_out, w_gate, w_up, w_down, tables, s_cb, tile):
    b, l, d = x.shape
    nt = l // tile
    const2 = lambda i, s: (0, 0)
    cur = lambda i, s: (i, nt - 1 - jnp.minimum(s, nt - 1), 0)
    prev = lambda i, s: (i, nt - 1 - jnp.maximum(s - 1, 0), 0)
    resident = dict(pipeline_mode=pl.Buffered(1))
    mod_spec = lambda j: pl.BlockSpec((None, 1, d), lambda i, s: (i, 0, j))
    return pl.pallas_call(
        functools.partial(_bwd_kernel, n_chunks=tile // CHUNK),
        grid=(b, nt + 1),
        in_specs=[pl.BlockSpec((None, tile, d), prev),
                  pl.BlockSpec((None, tile, ya.shape[-1]), prev),
                  pl.BlockSpec((None, tile, qkvg.shape[-1]), cur),
                  mod_spec(2), mod_spec(3), mod_spec(4), mod_spec(5),
                  pl.BlockSpec((1, d), const2),
                  pl.BlockSpec((1, d), const2),
                  pl.BlockSpec(w_out.shape, const2, **resident),
                  pl.BlockSpec(w_gate.shape, const2, **resident),
                  pl.BlockSpec(w_up.shape, const2, **resident),
                  pl.BlockSpec(w_down.shape, const2, **resident),
                  pl.BlockSpec(tables.shape, lambda i, s: (0, 0, 0)),
                  pl.BlockSpec((None, HEADS, HEAD_DIM, HEAD_DIM), lambda i, s: (i, 0, 0, 0))],
        out_specs=pl.BlockSpec((None, tile, d), prev),
        out_shape=jax.ShapeDtypeStruct((b, l, d), x.dtype),
        scratch_shapes=[pltpu.VMEM((HEADS, HEAD_DIM, HEAD_DIM), _F32),
                        pltpu.VMEM((2, tile, HEADS * HEAD_DIM), _F32)],
        compiler_params=pltpu.CompilerParams(dimension_semantics=("arbitrary", "arbitrary"),
                                             vmem_limit_bytes=VMEM_LIMIT_BYTES),
        name="bwd",
    )(x, ya, qkvg, mod3, mod3, mod3, mod3, norm2, norm_f, w_out, w_gate, w_up, w_down, tables, s_cb)


def _rope_tables(l, k_scale):
    n_freq = HEAD_DIM // 4
    inv = np.float32(ROPE_BASE) ** (-np.arange(n_freq, dtype=np.float32) / np.float32(n_freq))
    rows = np.repeat(np.arange(l // GRID_W, dtype=np.float32), GRID_W)
    cols = np.tile(np.arange(GRID_W, dtype=np.float32), l // GRID_W)
    ang_r, ang_c = rows[:, None] * inv[None, :], cols[:, None] * inv[None, :]
    cos = np.concatenate([np.cos(ang_r), np.cos(ang_c)], axis=-1)
    sin = np.concatenate([np.sin(ang_r), np.sin(ang_c)], axis=-1)
    cos2, sin2 = np.concatenate([cos, cos], axis=-1), np.concatenate([-sin, sin], axis=-1)
    return tuple(t.astype(np.float32) for t in (cos2, sin2, cos2 * k_scale, sin2 * k_scale))


def _reorder_head_dims(w):
    d, n = w.shape
    q = HEAD_DIM // 4
    return w.reshape(d, n // HEAD_DIM, 2, 2, q).transpose(0, 1, 3, 2, 4).reshape(d, n)


def kernel(x, c, ctx, c_ctx, w_mod, b_mod, norm1, w_in, sg_gain, sg_w, sg_b, ret_logit_f, ret_logit_b,
           w_out, norm2, w_gate, w_up, w_down, norm_f):
    b, l, d = x.shape
    a_width, r_width = GROUPS * LANES, HEADS * HEAD_DIM
    tile = min(TILE, l)
    assert w_mod.shape[0] == 1 and b + 1 <= MOD_ROWS and l % tile == 0 and tile % CHUNK == 0
    assert ctx.shape[1] % CHUNK == 0 and l // tile >= 2
    assert w_in.shape[-1] == 2 * a_width + 5 * r_width and d == a_width + r_width
    k_scale = HEAD_DIM ** -0.5

    cc = jnp.zeros((MOD_ROWS, d), _F32).at[:b].set(c).at[b].set(c_ctx)
    mod = _modulation(cc, w_mod[0], b_mod[0][None, :], block_n=1536)
    mod3 = mod.reshape(MOD_ROWS, 1, 6 * d)

    tables = _retention_tables(ret_logit_f[0], ret_logit_b[0])

    w = w_in[0]
    w_uv = w[:, :2 * a_width].astype(_BF16)
    w_qk = _reorder_head_dims(w[:, 2 * a_width:2 * a_width + 2 * r_width]).astype(_BF16)
    w_vg = w[:, 2 * a_width + 2 * r_width:].astype(_BF16)

    s_cf, s_cb = _context_states(ctx, mod3, norm1, w_qk, w_vg, tables, k_scale)

    sg_bias = jnp.broadcast_to(jnp.transpose(sg_b[0])[:, :, None], (CHUNK, GROUPS, LANES)).reshape(CHUNK, a_width)
    ya, qkvg, w_out_b, w_gate_b, w_up_b, w_down_b = _forward_pass(
        x, mod3, norm1, w_uv, w_qk, w_vg, sg_gain, sg_w[0].astype(_BF16), sg_bias, _rope_tables(l, k_scale),
        tables, s_cf, (w_out[0], w_gate[0], w_up[0], w_down[0]), tile)
    return _backward_pass(x, ya, qkvg, mod3, norm2, norm_f[None, :], w_out_b, w_gate_b, w_up_b, w_down_b,
                          tables, s_cb, tile)
```

```python
import functools

import numpy as np
import jax
import jax.numpy as jnp
from jax import lax
from jax.experimental import pallas as pl
from jax.experimental.pallas import tpu as pltpu

CHUNK = 128
GRID_W = 64
GROUPS = 4
HEADS = 4
HEAD_DIM = 128
RMS_EPS = 1e-6
ROPE_BASE = 10000.0
LANES = 128
MOD_ROWS = 8
TILE = 512
FFN_BLOCK = 1024
VMEM_LIMIT_BYTES = 56 * 1024 * 1024

IN_BLOCKS = 7
U_BLOCK, V_BLOCK, Q_BLOCK, K_BLOCK, VR_BLOCK, GF_BLOCK, GB_BLOCK = range(IN_BLOCKS)

T_DMASK_F, T_DMASK_B, T_XI_F, T_XI_B, T_ZETA_F, T_ZETA_B, T_DECAY_F, T_DECAY_B = range(8)

_BF16 = jnp.bfloat16
_F32 = jnp.float32


def _dot(a, b):
    return jnp.dot(a, b, preferred_element_type=_F32)


def _dot_nt(a, b):
    return lax.dot_general(a, b, (((1,), (1,)), ((), ())), preferred_element_type=_F32)


def _dot_tn(a, b):
    return lax.dot_general(a, b, (((0,), (0,)), ((), ())), preferred_element_type=_F32)


def _silu(x):
    return x * (1.0 / (1.0 + jnp.exp(-x)))


def _gelu_tanh(x):
    c = float(np.sqrt(2.0 / np.pi))
    half = 0.5 * x
    return half + half * jnp.tanh(x * (c + (0.044715 * c) * (x * x)))


def _rms(x):
    return x * lax.rsqrt(jnp.mean(x * x, axis=-1, keepdims=True) + RMS_EPS)


def _head(a, h):
    return a[:, h * HEAD_DIM:(h + 1) * HEAD_DIM]


def _prep_kernel(c_ref, wmod_ref, bmod_ref, win_ref, lf_ref, lb_ref, mod_ref, w16_ref, tab_ref, *, n_mod_blocks):
    j = pl.program_id(0)

    mod = _dot(_silu(c_ref[...]), wmod_ref[...]) + bmod_ref[...]
    for r in range(MOD_ROWS):
        mod_ref[r] = mod[r:r + 1, :]

    def reorder_head_dims(t):
        quarter = HEAD_DIM // 4
        lane = lax.broadcasted_iota(jnp.int32, (t.shape[0], HEAD_DIM), 1)
        heads = []
        for h in range(t.shape[1] // HEAD_DIM):
            th = _head(t, h)
            from_right = pltpu.roll(th, HEAD_DIM - quarter, 1)
            from_left = pltpu.roll(th, quarter, 1)
            heads.append(jnp.where((lane >= quarter) & (lane < 2 * quarter), from_right,
                                   jnp.where((lane >= 2 * quarter) & (lane < 3 * quarter), from_left, th)))
        return jnp.concatenate(heads, axis=-1)

    is_qk = (j == Q_BLOCK) | (j == K_BLOCK)

    @pl.when(is_qk)
    def _():
        w16_ref[...] = reorder_head_dims(win_ref[...]).astype(_BF16)

    @pl.when(jnp.logical_not(is_qk))
    def _():
        w16_ref[...] = win_ref[...].astype(_BF16)

    @pl.when(j == 0)
    def _():
        def log_sigmoid(x):
            return -(jnp.maximum(-x, 0.0) + jnp.log1p(jnp.exp(-jnp.abs(x))))

        shape = (CHUNK, HEADS * HEAD_DIM)
        lg_f = jnp.broadcast_to(log_sigmoid(lf_ref[...]), shape)
        lg_b = jnp.broadcast_to(log_sigmoid(lb_ref[...]), shape)
        i = lax.broadcasted_iota(jnp.int32, shape, 0).astype(_F32)
        jj = (lax.broadcasted_iota(jnp.int32, shape, 1) & (HEAD_DIM - 1)).astype(_F32)
        tab_ref[T_DMASK_F] = jnp.where(i >= jj, jnp.exp(lg_f * jnp.maximum(i - jj, 0.0)), 0.0)
        tab_ref[T_DMASK_B] = jnp.where(jj >= i, jnp.exp(lg_b * jnp.maximum(jj - i, 0.0)), 0.0)
        tab_ref[T_XI_F] = jnp.exp(lg_f * (i + 1.0))
        tab_ref[T_XI_B] = jnp.exp(lg_b * (CHUNK - i))
        tab_ref[T_ZETA_F] = jnp.exp(lg_f * (CHUNK - 1.0 - i))
        tab_ref[T_ZETA_B] = jnp.exp(lg_b * i)
        tab_ref[T_DECAY_F] = jnp.exp(lg_f * CHUNK)
        tab_ref[T_DECAY_B] = jnp.exp(lg_b * CHUNK)


def _prepare(cc, w_mod, b_mod, w_in, logit_f, logit_b):
    d, n_mod = w_mod.shape
    r_width = HEADS * HEAD_DIM
    n_mod_blocks = n_mod // d
    assert w_in.shape == (d, IN_BLOCKS * r_width) and n_mod_blocks <= IN_BLOCKS
    mod_blk = lambda j: jnp.minimum(j, n_mod_blocks - 1)
    rep = lambda l: jnp.repeat(l.astype(_F32), HEAD_DIM)[None, :]
    return pl.pallas_call(
        functools.partial(_prep_kernel, n_mod_blocks=n_mod_blocks),
        grid=(IN_BLOCKS,),
        in_specs=[pl.BlockSpec((MOD_ROWS, d), lambda j: (0, 0)),
                  pl.BlockSpec((d, d), lambda j: (0, mod_blk(j))),
                  pl.BlockSpec((1, d), lambda j: (0, mod_blk(j))),
                  pl.BlockSpec((d, r_width), lambda j: (0, j)),
                  pl.BlockSpec((1, r_width), lambda j: (0, 0)),
                  pl.BlockSpec((1, r_width), lambda j: (0, 0))],
        out_specs=[pl.BlockSpec((MOD_ROWS, 1, d), lambda j: (0, 0, mod_blk(j))),
                   pl.BlockSpec((d, r_width), lambda j: (0, j)),
                   pl.BlockSpec((8, CHUNK, r_width), lambda j: (0, 0, 0))],
        out_shape=[jax.ShapeDtypeStruct((MOD_ROWS, 1, n_mod), _F32),
                   jax.ShapeDtypeStruct(w_in.shape, _BF16),
                   jax.ShapeDtypeStruct((8, CHUNK, r_width), _F32)],
        compiler_params=pltpu.CompilerParams(dimension_semantics=("arbitrary",),
                                             vmem_limit_bytes=VMEM_LIMIT_BYTES),
        name="prep",
    )(cc, w_mod, b_mod, w_in, rep(logit_f), rep(logit_b))


def _ctx_kernel(ctx_ref, sh_ref, sc_ref, n1_ref, wk_ref, wv_ref, tab_ref, sf_ref, sb_ref, *, n_chunks, k_scale):
    hc = ((_rms(ctx_ref[...]) * n1_ref[...]) * (1.0 + sc_ref[...]) + sh_ref[...]).astype(_BF16)
    k = (_dot(hc, wk_ref[...]) * k_scale).astype(_BF16)
    v = _dot(hc, wv_ref[...])
    for h in range(HEADS):
        s_f = jnp.zeros((HEAD_DIM, HEAD_DIM), _F32)
        s_b = jnp.zeros((HEAD_DIM, HEAD_DIM), _F32)
        for n in range(n_chunks):
            rows = slice(n * CHUNK, (n + 1) * CHUNK)
            upd = _dot_tn(_head(k[rows], h), (_head(v[rows], h) * _head(tab_ref[T_ZETA_F], h)).astype(_BF16))
            s_f = _head(tab_ref[T_DECAY_F], h) * s_f + upd
        for n in reversed(range(n_chunks)):
            rows = slice(n * CHUNK, (n + 1) * CHUNK)
            upd = _dot_tn(_head(k[rows], h), (_head(v[rows], h) * _head(tab_ref[T_ZETA_B], h)).astype(_BF16))
            s_b = _head(tab_ref[T_DECAY_B], h) * s_b + upd
        sf_ref[h] = s_f
        sb_ref[h] = s_b


def _context_states(ctx, mod3, norm1, w16, tables, k_scale):
    b, lc, d = ctx.shape
    r_width = HEADS * HEAD_DIM
    state = jax.ShapeDtypeStruct((b, HEADS, HEAD_DIM, HEAD_DIM), _F32)
    ctx_row = b
    return pl.pallas_call(
        functools.partial(_ctx_kernel, n_chunks=lc // CHUNK, k_scale=k_scale),
        grid=(b,),
        in_specs=[pl.BlockSpec((None, lc, d), lambda i: (i, 0, 0)),
                  pl.BlockSpec((None, 1, d), lambda i: (ctx_row, 0, 0)),
                  pl.BlockSpec((None, 1, d), lambda i: (ctx_row, 0, 1)),
                  pl.BlockSpec((1, d), lambda i: (0, 0)),
                  pl.BlockSpec((d, r_width), lambda i: (0, K_BLOCK)),
                  pl.BlockSpec((d, r_width), lambda i: (0, VR_BLOCK)),
                  pl.BlockSpec(tables.shape, lambda i: (0, 0, 0))],
        out_specs=[pl.BlockSpec((None, HEADS, HEAD_DIM, HEAD_DIM), lambda i: (i, 0, 0, 0))] * 2,
        out_shape=[state, state],
        compiler_params=pltpu.CompilerParams(dimension_semantics=("arbitrary",),
                                             vmem_limit_bytes=VMEM_LIMIT_BYTES),
        name="ctx",
    )(ctx, mod3, mod3, norm1, w16, w16, tables)


def _retention_tile(load, chunk_order, s_ref, tab_ref, t_dmask, t_decay, emit, out):
    keys = [(n, h) for n in chunk_order for h in range(HEADS)]
    q, k, xq, v, vz, gate = {}, {}, {}, {}, {}, {}
    for n in chunk_order:
        q[n], k[n], xq[n], v[n], vz[n], gate[n] = load(n)
    scores = {(n, h): _dot_nt(_head(q[n], h), _head(k[n], h)) for n, h in keys}
    upd = {(n, h): _dot_tn(_head(k[n], h), _head(vz[n], h)) for n, h in keys}
    yield
    state = [s_ref[h] for h in range(HEADS)]
    lhs, rhs = {}, {}
    for n, h in keys:
        p = (scores[n, h] * _head(tab_ref[t_dmask], h)).astype(_BF16)
        lhs[n, h] = jnp.concatenate([p, _head(xq[n], h)], axis=1)
        rhs[n, h] = jnp.concatenate([_head(v[n], h), state[h].astype(_BF16)], axis=0)
        state[h] = _head(tab_ref[t_decay], h) * state[h] + upd[n, h]
    yield
    o = {key: _dot(lhs[key], rhs[key]) for key in keys}
    yield
    for n in chunk_order:
        emit(n, jnp.concatenate([_silu(_head(gate[n], h)) * _rms(o[n, h]) for h in range(HEADS)], axis=-1))
    for h in range(HEADS):
        out.add(s_ref, h, state[h])


def _interleave(*stages):
    stages = list(stages)
    while stages:
        for g in list(stages):
            try:
                next(g)
            except StopIteration:
                stages.remove(g)


class _DeferredStores:
    def __init__(self):
        self._items = []

    def add(self, ref, idx, value):
        self._items.append((ref, idx, value))

    def commit(self):
        for ref, idx, value in self._items:
            ref[idx] = value


def _skewed_steps(first, middle, last):
    s = pl.program_id(1)
    n_tiles = pl.num_programs(1) - 1

    @pl.when(s == 0)
    def _():
        first(0)

    for parity in range(2):
        @pl.when((s > 0) & (s < n_tiles) & ((s & 1) == parity))
        def _():
            middle(parity)

    for parity in range(2):
        @pl.when((s == n_tiles) & ((s & 1) == parity))
        def _():
            last(1 - parity)


def _fwd_kernel(x_ref, sh_ref, sc_ref, n1_ref, wuv_ref, wqk_ref, wvr_ref, wgf_ref, wgb_ref, gain_ref, sgw_ref, sgb_ref,
                rcq_ref, rsq_ref, rck_ref, rsk_ref, tab_ref, s0_ref, wo32_ref, wg32_ref, wu32_ref, wd32_ref,
                ya_ref, qkvg_ref, wo16_ref, wg16_ref, wu16_ref, wd16_ref, s_ref, f32_ref, b16_ref, *, n_chunks):
    a_width = GROUPS * LANES
    r_width = HEADS * HEAD_DIM
    U, GF = 0, a_width
    VG, Q, K, XQ, VB, VZ = (i * r_width for i in range(6))
    rows_all = slice(None)

    def per_chunk(t, table):
        return jnp.concatenate([t[n * CHUNK:(n + 1) * CHUNK] * table for n in range(n_chunks)], axis=0)

    def convert(out):
        for src, dst in ((wo32_ref, wo16_ref), (wg32_ref, wg16_ref), (wu32_ref, wu16_ref), (wd32_ref, wd16_ref)):
            out.add(dst, (rows_all, rows_all), src[...].astype(_BF16))

    def project(slot, out):
        x = x_ref[...]
        r = lax.rsqrt(jnp.mean(x * x, axis=-1, keepdims=True) + RMS_EPS)
        hx = ((x * r) * (n1_ref[...] * (1.0 + sc_ref[...])) + sh_ref[...]).astype(_BF16)

        def proj(w_ref, lo, width):
            return _dot(hx, w_ref[:, lo:lo + width])

        yield
        out.add(f32_ref, (slot, rows_all, slice(U, U + a_width)), _gelu_tanh(proj(wuv_ref, 0, a_width)))
        v = _gelu_tanh(proj(wuv_ref, a_width, a_width))
        vg = jnp.concatenate([(_rms(_head(v, g)) * _head(gain_ref[...], g)).astype(_BF16) for g in range(GROUPS)],
                             axis=-1)
        out.add(b16_ref, (slot, rows_all, slice(VG, VG + a_width)), vg)
        yield

        def rope(t, c_ref, s_ref_):
            return jnp.concatenate(
                [_head(t, h) * c_ref[...] + pltpu.roll(_head(t, h), HEAD_DIM // 2, 1) * s_ref_[...]
                 for h in range(HEADS)], axis=-1)

        q = rope(proj(wqk_ref, 0, r_width), rcq_ref, rsq_ref)
        q16 = q.astype(_BF16)
        k16 = rope(proj(wqk_ref, r_width, r_width), rck_ref, rsk_ref).astype(_BF16)
        out.add(b16_ref, (slot, rows_all, slice(Q, Q + r_width)), q16)
        out.add(b16_ref, (slot, rows_all, slice(K, K + r_width)), k16)
        out.add(b16_ref, (slot, rows_all, slice(XQ, XQ + r_width)), per_chunk(q, tab_ref[T_XI_F]).astype(_BF16))
        out.add(qkvg_ref, (rows_all, slice(0, r_width)), q16)
        out.add(qkvg_ref, (rows_all, slice(r_width, 2 * r_width)), k16)
        yield
        vr = proj(wvr_ref, 0, r_width)
        v16 = vr.astype(_BF16)
        out.add(b16_ref, (slot, rows_all, slice(VB, VB + r_width)), v16)
        out.add(b16_ref, (slot, rows_all, slice(VZ, VZ + r_width)), per_chunk(vr, tab_ref[T_ZETA_F]).astype(_BF16))
        out.add(f32_ref, (slot, rows_all, slice(GF, GF + r_width)), proj(wgf_ref, 0, r_width))
        out.add(qkvg_ref, (rows_all, slice(2 * r_width, 3 * r_width)), v16)
        out.add(qkvg_ref, (rows_all, slice(3 * r_width, 4 * r_width)),
                proj(wgb_ref, 0, r_width).astype(_BF16))

    def gate_mix(slot, out):
        mixed = []
        for g in range(GROUPS):
            vg = jnp.concatenate([b16_ref[slot, n * CHUNK:(n + 1) * CHUNK, VG + g * LANES:VG + (g + 1) * LANES]
                                  for n in range(n_chunks)], axis=1)
            mixed.append(_dot(sgw_ref[g], vg))
        yield
        for g in range(GROUPS):
            cols = slice(g * LANES, (g + 1) * LANES)
            for n in range(n_chunks):
                rows = slice(n * CHUNK, (n + 1) * CHUNK)
                m = mixed[g][:, n * LANES:(n + 1) * LANES] + sgb_ref[:, cols]
                u = f32_ref[slot, rows, U + g * LANES:U + (g + 1) * LANES]
                out.add(ya_ref, (rows, cols), (u * m).astype(_BF16))

    def retain(slot, out):
        def load(n):
            rows = slice(n * CHUNK, (n + 1) * CHUNK)
            return tuple(b16_ref[slot, rows, c:c + r_width] for c in (Q, K, XQ, VB, VZ)) + (
                f32_ref[slot, rows, GF:GF + r_width],)

        def emit(n, y):
            out.add(ya_ref, (slice(n * CHUNK, (n + 1) * CHUNK), slice(a_width, a_width + r_width)), y.astype(_BF16))

        return _retention_tile(load, range(n_chunks), s_ref, tab_ref, T_DMASK_F, T_DECAY_F, emit, out)

    def first(slot):
        s_ref[...] = s0_ref[...]
        out = _DeferredStores()
        convert(out)
        _interleave(project(slot, out))
        out.commit()

    def middle(slot):
        out = _DeferredStores()
        convert(out)
        _interleave(retain(1 - slot, out), gate_mix(1 - slot, out), project(slot, out))
        out.commit()

    def last(slot):
        out = _DeferredStores()
        convert(out)
        _interleave(retain(slot, out), gate_mix(slot, out))
        out.commit()

    _skewed_steps(first, middle, last)


def _forward_pass(x, mod3, norm1, w16, sg_gain, sg_w, sg_bias, rope, tables, s_cf, bwd_weights, tile):
    b, l, d = x.shape
    nt = l // tile
    a_width, r_width = GROUPS * LANES, HEADS * HEAD_DIM
    const2 = lambda i, s: (0, 0)
    cur = lambda i, s: (i, jnp.minimum(s, nt - 1), 0)
    prev = lambda i, s: (i, jnp.maximum(s - 1, 0), 0)
    rope_spec = pl.BlockSpec((tile, HEAD_DIM), lambda i, s: (jnp.minimum(s, nt - 1), 0))
    resident = dict(pipeline_mode=pl.Buffered(1))
    n_blk = max(n for n in (1, 2, 4, 8, 16) if n <= b * (nt + 1))
    blk = lambda i, s: (jnp.minimum(i * (nt + 1) + s, n_blk - 1), 0)
    w_specs = [pl.BlockSpec((w.shape[0] // n_blk, w.shape[1]), blk) for w in bwd_weights]
    return pl.pallas_call(
        functools.partial(_fwd_kernel, n_chunks=tile // CHUNK),
        grid=(b, nt + 1),
        in_specs=[pl.BlockSpec((None, tile, d), cur),
                  pl.BlockSpec((None, 1, d), lambda i, s: (i, 0, 0)),
                  pl.BlockSpec((None, 1, d), lambda i, s: (i, 0, 1)),
                  pl.BlockSpec((1, d), const2),
                  pl.BlockSpec((d, 2 * r_width), lambda i, s: (0, U_BLOCK // 2), **resident),
                  pl.BlockSpec((d, 2 * r_width), lambda i, s: (0, Q_BLOCK // 2), **resident),
                  pl.BlockSpec((d, r_width), lambda i, s: (0, VR_BLOCK), **resident),
                  pl.BlockSpec((d, r_width), lambda i, s: (0, GF_BLOCK), **resident),
                  pl.BlockSpec((d, r_width), lambda i, s: (0, GB_BLOCK), **resident),
                  pl.BlockSpec((1, a_width), const2),
                  pl.BlockSpec(sg_w.shape, lambda i, s: (0, 0, 0)),
                  pl.BlockSpec(sg_bias.shape, const2),
                  rope_spec, rope_spec, rope_spec, rope_spec,
                  pl.BlockSpec(tables.shape, lambda i, s: (0, 0, 0)),
                  pl.BlockSpec((None, HEADS, HEAD_DIM, HEAD_DIM), lambda i, s: (i, 0, 0, 0))] + w_specs,
        out_specs=[pl.BlockSpec((None, tile, a_width + r_width), prev),
                   pl.BlockSpec((None, tile, 4 * r_width), cur)] + w_specs,
        out_shape=[jax.ShapeDtypeStruct((b, l, a_width + r_width), _BF16),
                   jax.ShapeDtypeStruct((b, l, 4 * r_width), _BF16)]
                  + [jax.ShapeDtypeStruct(w.shape, _BF16) for w in bwd_weights],
        scratch_shapes=[pltpu.VMEM((HEADS, HEAD_DIM, HEAD_DIM), _F32),
                        pltpu.VMEM((2, tile, a_width + r_width), _F32),
                        pltpu.VMEM((2, tile, 6 * r_width), _BF16)],
        compiler_params=pltpu.CompilerParams(dimension_semantics=("arbitrary", "arbitrary"),
                                             vmem_limit_bytes=VMEM_LIMIT_BYTES),
        name="fwd",
    )(x, mod3, mod3, norm1, w16, w16, w16, w16, w16, sg_gain, sg_w, sg_bias, *rope, tables, s_cf, *bwd_weights)


def _bwd_kernel(x_ref, ya_ref, qkvg_ref, g1_ref, sh_ref, sc_ref, g2_ref, n2_ref, nf_ref,
                wout_ref, wg_ref, wu_ref, wd_ref, tab_ref, s0_ref, o_ref, s_ref, yb_ref, *, n_chunks):
    a_width = GROUPS * LANES
    r_width = HEADS * HEAD_DIM

    def retain(slot, out):
        def load(n):
            rows = slice(n * CHUNK, (n + 1) * CHUNK)
            q, k, v, gate = (qkvg_ref[rows, i * r_width:(i + 1) * r_width] for i in range(4))
            return (q, k, (q.astype(_F32) * tab_ref[T_XI_B]).astype(_BF16),
                    v, (v.astype(_F32) * tab_ref[T_ZETA_B]).astype(_BF16), gate.astype(_F32))

        def emit(n, y):
            out.add(yb_ref, (slot, slice(n * CHUNK, (n + 1) * CHUNK), slice(None)), y)

        return _retention_tile(load, list(reversed(range(n_chunks))), s_ref, tab_ref, T_DMASK_B, T_DECAY_B,
                               emit, out)

    def finish(slot, out):
        y_r = ya_ref[:, a_width:].astype(_F32) + yb_ref[slot]
        y = _dot(ya_ref[:, :a_width], wout_ref[:a_width, :]) + _dot(y_r.astype(_BF16), wout_ref[a_width:, :])
        x1 = x_ref[...] + g1_ref[...] * y
        h2 = ((_rms(x1) * n2_ref[...]) * (1.0 + sc_ref[...]) + sh_ref[...]).astype(_BF16)
        acc = None
        d_ff = wg_ref.shape[1]
        for lo in range(0, d_ff, FFN_BLOCK):
            yield
            cols = slice(lo, min(lo + FFN_BLOCK, d_ff))
            act = (_silu(_dot(h2, wg_ref[:, cols])) * _dot(h2, wu_ref[:, cols])).astype(_BF16)
            part = _dot(act, wd_ref[cols, :])
            acc = part if acc is None else acc + part
        x2 = x1 + g2_ref[...] * acc
        out.add(o_ref, (slice(None), slice(None)), _rms(x2) * nf_ref[...])

    def first(slot):
        s_ref[...] = s0_ref[...]
        out = _DeferredStores()
        _interleave(retain(slot, out))
        out.commit()

    def middle(slot):
        out = _DeferredStores()
        _interleave(retain(slot, out), finish(1 - slot, out))
        out.commit()

    def last(slot):
        out = _DeferredStores()
        _interleave(finish(slot, out))
        out.commit()

    _skewed_steps(first, middle, last)


def _backward_pass(x, ya, qkvg, mod3, norm2, norm_f, w_out, w_gate, w_up, w_down, tables, s_cb, tile):
    b, l, d = x.shape
    nt = l // tile
    const2 = lambda i, s: (0, 0)
    cur = lambda i, s: (i, nt - 1 - jnp.minimum(s, nt - 1), 0)
    prev = lambda i, s: (i, nt - 1 - jnp.maximum(s - 1, 0), 0)
    resident = dict(pipeline_mode=pl.Buffered(1))
    mod_spec = lambda j: pl.BlockSpec((None, 1, d), lambda i, s: (i, 0, j))
    return pl.pallas_call(
        functools.partial(_bwd_kernel, n_chunks=tile // CHUNK),
        grid=(b, nt + 1),
        in_specs=[pl.BlockSpec((None, tile, d), prev),
                  pl.BlockSpec((None, tile, ya.shape[-1]), prev),
                  pl.BlockSpec((None, tile, qkvg.shape[-1]), cur),
                  mod_spec(2), mod_spec(3), mod_spec(4), mod_spec(5),
                  pl.BlockSpec((1, d), const2),
                  pl.BlockSpec((1, d), const2),
                  pl.BlockSpec(w_out.shape, const2, **resident),
                  pl.BlockSpec(w_gate.shape, const2, **resident),
                  pl.BlockSpec(w_up.shape, const2, **resident),
                  pl.BlockSpec(w_down.shape, const2, **resident),
                  pl.BlockSpec(tables.shape, lambda i, s: (0, 0, 0)),
                  pl.BlockSpec((None, HEADS, HEAD_DIM, HEAD_DIM), lambda i, s: (i, 0, 0, 0))],
        out_specs=pl.BlockSpec((None, tile, d), prev),
        out_shape=jax.ShapeDtypeStruct((b, l, d), x.dtype),
        scratch_shapes=[pltpu.VMEM((HEADS, HEAD_DIM, HEAD_DIM), _F32),
                        pltpu.VMEM((2, tile, HEADS * HEAD_DIM), _F32)],
        compiler_params=pltpu.CompilerParams(dimension_semantics=("arbitrary", "arbitrary"),
                                             vmem_limit_bytes=VMEM_LIMIT_BYTES),
        name="bwd",
    )(x, ya, qkvg, mod3, mod3, mod3, mod3, norm2, norm_f, w_out, w_gate, w_up, w_down, tables, s_cb)


def _rope_tables(l, k_scale):
    n_freq = HEAD_DIM // 4
    inv = np.float32(ROPE_BASE) ** (-np.arange(n_freq, dtype=np.float32) / np.float32(n_freq))
    rows = np.repeat(np.arange(l // GRID_W, dtype=np.float32), GRID_W)
    cols = np.tile(np.arange(GRID_W, dtype=np.float32), l // GRID_W)
    ang_r, ang_c = rows[:, None] * inv[None, :], cols[:, None] * inv[None, :]
    cos = np.concatenate([np.cos(ang_r), np.cos(ang_c)], axis=-1)
    sin = np.concatenate([np.sin(ang_r), np.sin(ang_c)], axis=-1)
    cos2, sin2 = np.concatenate([cos, cos], axis=-1), np.concatenate([-sin, sin], axis=-1)
    return tuple(t.astype(np.float32) for t in (cos2, sin2, cos2 * k_scale, sin2 * k_scale))


def kernel(x, c, ctx, c_ctx, w_mod, b_mod, norm1, w_in, sg_gain, sg_w, sg_b, ret_logit_f, ret_logit_b,
           w_out, norm2, w_gate, w_up, w_down, norm_f):
    b, l, d = x.shape
    a_width, r_width = GROUPS * LANES, HEADS * HEAD_DIM
    tile = min(TILE, l)
    assert w_mod.shape[0] == 1 and b + 1 <= MOD_ROWS and l % tile == 0 and tile % CHUNK == 0
    assert ctx.shape[1] % CHUNK == 0 and l // tile >= 2
    assert a_width == r_width and d == a_width + r_width
    k_scale = HEAD_DIM ** -0.5

    cc = jnp.concatenate([c, c_ctx[None, :], jnp.zeros((MOD_ROWS - b - 1, d), _F32)], axis=0)
    mod3, w16, tables = _prepare(cc, w_mod[0], b_mod[0][None, :], w_in[0], ret_logit_f[0], ret_logit_b[0])
    s_cf, s_cb = _context_states(ctx, mod3, norm1, w16, tables, k_scale)
    sg_bias = jnp.broadcast_to(jnp.transpose(sg_b[0])[:, :, None], (CHUNK, GROUPS, LANES)).reshape(CHUNK, a_width)
    ya, qkvg, w_out_b, w_gate_b, w_up_b, w_down_b = _forward_pass(
        x, mod3, norm1, w16, sg_gain, sg_w[0].astype(_BF16), sg_bias, _rope_tables(l, k_scale),
        tables, s_cf, (w_out[0], w_gate[0], w_up[0], w_down[0]), tile)
    return _backward_pass(x, ya, qkvg, mod3, norm2, norm_f[None, :], w_out_b, w_gate_b, w_up_b, w_down_b,
                          tables, s_cb, tile)
```

```python
import functools

import numpy as np
import jax
import jax.numpy as jnp
from jax import lax
from jax.experimental import pallas as pl
from jax.experimental.pallas import tpu as pltpu

CHUNK = 128
GRID_W = 64
GROUPS = 4
HEADS = 4
HEAD_DIM = 128
RMS_EPS = 1e-6
ROPE_BASE = 10000.0
LANES = 128
MOD_ROWS = 8
TILE = 512
PROJ_CHUNKS = 2
FFN_BLOCK = 1024
VMEM_LIMIT_BYTES = 56 * 1024 * 1024

IN_BLOCKS = 7
U_BLOCK, V_BLOCK, Q_BLOCK, K_BLOCK, VR_BLOCK, GF_BLOCK, GB_BLOCK = range(IN_BLOCKS)

T_DMASK_F, T_DMASK_B, T_XI_F, T_XI_B, T_ZETA_F, T_ZETA_B, T_DECAY_F, T_DECAY_B = range(8)

_BF16 = jnp.bfloat16
_F32 = jnp.float32


def _dot(a, b):
    return jnp.dot(a, b, preferred_element_type=_F32)


def _dot_nt(a, b):
    return lax.dot_general(a, b, (((1,), (1,)), ((), ())), preferred_element_type=_F32)


def _dot_tn(a, b):
    return lax.dot_general(a, b, (((0,), (0,)), ((), ())), preferred_element_type=_F32)


def _silu(x):
    return x * (1.0 / (1.0 + jnp.exp(-x)))


def _gelu_tanh(x):
    c = float(np.sqrt(2.0 / np.pi))
    half = 0.5 * x
    return half + half * jnp.tanh(x * (c + (0.044715 * c) * (x * x)))


def _rms(x):
    return x * lax.rsqrt(jnp.mean(x * x, axis=-1, keepdims=True) + RMS_EPS)


def _head(a, h):
    return a[:, h * HEAD_DIM:(h + 1) * HEAD_DIM]


def _prep_kernel(c_ref, cctx_ref, wmod_ref, bmod_ref, win_ref, lf_ref, lb_ref, mod_ref, w16_ref, tab_ref):
    j = pl.program_id(0)

    n_batch, d = c_ref.shape
    cc = jnp.concatenate([c_ref[...], cctx_ref[...], jnp.zeros((MOD_ROWS - n_batch - 1, d), _F32)], axis=0)
    mod = _dot(_silu(cc), wmod_ref[...]) + bmod_ref[...]
    for r in range(MOD_ROWS):
        mod_ref[r] = mod[r:r + 1, :]

    def reorder_head_dims(t):
        quarter = HEAD_DIM // 4
        lane = lax.broadcasted_iota(jnp.int32, (t.shape[0], HEAD_DIM), 1)
        heads = []
        for h in range(t.shape[1] // HEAD_DIM):
            th = _head(t, h)
            from_right = pltpu.roll(th, HEAD_DIM - quarter, 1)
            from_left = pltpu.roll(th, quarter, 1)
            heads.append(jnp.where((lane >= quarter) & (lane < 2 * quarter), from_right,
                                   jnp.where((lane >= 2 * quarter) & (lane < 3 * quarter), from_left, th)))
        return jnp.concatenate(heads, axis=-1)

    is_qk = (j == Q_BLOCK) | (j == K_BLOCK)

    @pl.when(is_qk)
    def _():
        w16_ref[...] = reorder_head_dims(win_ref[...]).astype(_BF16)

    @pl.when(jnp.logical_not(is_qk))
    def _():
        w16_ref[...] = win_ref[...].astype(_BF16)

    @pl.when(j == 0)
    def _():
        def log_sigmoid(x):
            return -(jnp.maximum(-x, 0.0) + jnp.log1p(jnp.exp(-jnp.abs(x))))

        shape = (CHUNK, HEADS * HEAD_DIM)
        head = lax.broadcasted_iota(jnp.int32, shape, 1) // HEAD_DIM

        def per_head(logit_ref):
            t = jnp.zeros(shape, _F32)
            for h in range(HEADS):
                t = jnp.where(head == h, logit_ref[h], t)
            return t

        lg_f = log_sigmoid(per_head(lf_ref))
        lg_b = log_sigmoid(per_head(lb_ref))
        i = lax.broadcasted_iota(jnp.int32, shape, 0).astype(_F32)
        jj = (lax.broadcasted_iota(jnp.int32, shape, 1) & (HEAD_DIM - 1)).astype(_F32)
        tab_ref[T_DMASK_F] = jnp.where(i >= jj, jnp.exp(lg_f * jnp.maximum(i - jj, 0.0)), 0.0)
        tab_ref[T_DMASK_B] = jnp.where(jj >= i, jnp.exp(lg_b * jnp.maximum(jj - i, 0.0)), 0.0)
        tab_ref[T_XI_F] = jnp.exp(lg_f * (i + 1.0))
        tab_ref[T_XI_B] = jnp.exp(lg_b * (CHUNK - i))
        tab_ref[T_ZETA_F] = jnp.exp(lg_f * (CHUNK - 1.0 - i))
        tab_ref[T_ZETA_B] = jnp.exp(lg_b * i)
        tab_ref[T_DECAY_F] = jnp.exp(lg_f * CHUNK)
        tab_ref[T_DECAY_B] = jnp.exp(lg_b * CHUNK)


def _prepare(c, c_ctx, w_mod, b_mod, w_in, logit_f, logit_b):
    d, n_mod = w_mod.shape
    r_width = HEADS * HEAD_DIM
    n_mod_blocks = n_mod // d
    assert w_in.shape == (d, IN_BLOCKS * r_width) and n_mod_blocks <= IN_BLOCKS
    mod_blk = lambda j: jnp.minimum(j, n_mod_blocks - 1)
    return pl.pallas_call(
        _prep_kernel,
        grid=(IN_BLOCKS,),
        in_specs=[pl.BlockSpec(c.shape, lambda j: (0, 0)),
                  pl.BlockSpec((1, d), lambda j: (0, 0)),
                  pl.BlockSpec((d, d), lambda j: (0, mod_blk(j))),
                  pl.BlockSpec((1, d), lambda j: (0, mod_blk(j))),
                  pl.BlockSpec((d, r_width), lambda j: (0, j)),
                  pl.BlockSpec(memory_space=pltpu.SMEM),
                  pl.BlockSpec(memory_space=pltpu.SMEM)],
        out_specs=[pl.BlockSpec((MOD_ROWS, 1, d), lambda j: (0, 0, mod_blk(j))),
                   pl.BlockSpec((d, r_width), lambda j: (0, j)),
                   pl.BlockSpec((8, CHUNK, r_width), lambda j: (0, 0, 0))],
        out_shape=[jax.ShapeDtypeStruct((MOD_ROWS, 1, n_mod), _F32),
                   jax.ShapeDtypeStruct(w_in.shape, _BF16),
                   jax.ShapeDtypeStruct((8, CHUNK, r_width), _F32)],
        compiler_params=pltpu.CompilerParams(dimension_semantics=("arbitrary",),
                                             vmem_limit_bytes=VMEM_LIMIT_BYTES),
        name="prep",
    )(c, c_ctx[None, :], w_mod, b_mod, w_in, logit_f.astype(_F32), logit_b.astype(_F32))


def _ctx_kernel(ctx_ref, sh_ref, sc_ref, n1_ref, wk_ref, wv_ref, tab_ref, sf_ref, sb_ref, *, n_chunks, k_scale):
    hc = ((_rms(ctx_ref[...]) * n1_ref[...]) * (1.0 + sc_ref[...]) + sh_ref[...]).astype(_BF16)
    k = (_dot(hc, wk_ref[...]) * k_scale).astype(_BF16)
    v = _dot(hc, wv_ref[...])
    for h in range(HEADS):
        s_f = jnp.zeros((HEAD_DIM, HEAD_DIM), _F32)
        s_b = jnp.zeros((HEAD_DIM, HEAD_DIM), _F32)
        for n in range(n_chunks):
            rows = slice(n * CHUNK, (n + 1) * CHUNK)
            upd = _dot_tn(_head(k[rows], h), (_head(v[rows], h) * _head(tab_ref[T_ZETA_F], h)).astype(_BF16))
            s_f = _head(tab_ref[T_DECAY_F], h) * s_f + upd
        for n in reversed(range(n_chunks)):
            rows = slice(n * CHUNK, (n + 1) * CHUNK)
            upd = _dot_tn(_head(k[rows], h), (_head(v[rows], h) * _head(tab_ref[T_ZETA_B], h)).astype(_BF16))
            s_b = _head(tab_ref[T_DECAY_B], h) * s_b + upd
        sf_ref[h] = s_f
        sb_ref[h] = s_b


def _context_states(ctx, mod3, norm1, w16, tables, k_scale):
    b, lc, d = ctx.shape
    r_width = HEADS * HEAD_DIM
    state = jax.ShapeDtypeStruct((b, HEADS, HEAD_DIM, HEAD_DIM), _F32)
    ctx_row = b
    return pl.pallas_call(
        functools.partial(_ctx_kernel, n_chunks=lc // CHUNK, k_scale=k_scale),
        grid=(b,),
        in_specs=[pl.BlockSpec((None, lc, d), lambda i: (i, 0, 0)),
                  pl.BlockSpec((None, 1, d), lambda i: (ctx_row, 0, 0)),
                  pl.BlockSpec((None, 1, d), lambda i: (ctx_row, 0, 1)),
                  pl.BlockSpec((1, d), lambda i: (0, 0)),
                  pl.BlockSpec((d, r_width), lambda i: (0, K_BLOCK)),
                  pl.BlockSpec((d, r_width), lambda i: (0, VR_BLOCK)),
                  pl.BlockSpec(tables.shape, lambda i: (0, 0, 0))],
        out_specs=[pl.BlockSpec((None, HEADS, HEAD_DIM, HEAD_DIM), lambda i: (i, 0, 0, 0))] * 2,
        out_shape=[state, state],
        compiler_params=pltpu.CompilerParams(dimension_semantics=("arbitrary",),
                                             vmem_limit_bytes=VMEM_LIMIT_BYTES),
        name="ctx",
    )(ctx, mod3, mod3, norm1, w16, w16, tables)


def _retention_tile(load, chunk_order, s_ref, tab_ref, t_dmask, t_decay, emit, out):
    keys = [(n, h) for n in chunk_order for h in range(HEADS)]
    q, k, xq, v, vz, gate = {}, {}, {}, {}, {}, {}
    for n in chunk_order:
        q[n], k[n], xq[n], v[n], vz[n], gate[n] = load(n)
    scores = {(n, h): _dot_nt(_head(q[n], h), _head(k[n], h)) for n, h in keys}
    upd = {(n, h): _dot_tn(_head(k[n], h), _head(vz[n], h)) for n, h in keys}
    yield
    state = [s_ref[h] for h in range(HEADS)]
    lhs, rhs = {}, {}
    for n, h in keys:
        p = (scores[n, h] * _head(tab_ref[t_dmask], h)).astype(_BF16)
        lhs[n, h] = jnp.concatenate([p, _head(xq[n], h)], axis=1)
        rhs[n, h] = jnp.concatenate([_head(v[n], h), state[h].astype(_BF16)], axis=0)
        state[h] = _head(tab_ref[t_decay], h) * state[h] + upd[n, h]
    yield
    o = {key: _dot(lhs[key], rhs[key]) for key in keys}
    yield
    for n in chunk_order:
        emit(n, jnp.concatenate([_silu(_head(gate[n], h)) * _rms(o[n, h]) for h in range(HEADS)], axis=-1))
    for h in range(HEADS):
        out.add(s_ref, h, state[h])


def _interleave(*stages):
    stages = list(stages)
    while stages:
        for g in list(stages):
            try:
                next(g)
            except StopIteration:
                stages.remove(g)


class _DeferredStores:
    def __init__(self):
        self._items = []

    def add(self, ref, idx, value):
        self._items.append((ref, idx, value))

    def commit(self):
        for ref, idx, value in self._items:
            ref[idx] = value


def _skewed_steps(first, middle, last):
    s = pl.program_id(1)
    n_tiles = pl.num_programs(1) - 1

    @pl.when(s == 0)
    def _():
        first(0)

    for parity in range(2):
        @pl.when((s > 0) & (s < n_tiles) & ((s & 1) == parity))
        def _():
            middle(parity)

    for parity in range(2):
        @pl.when((s == n_tiles) & ((s & 1) == parity))
        def _():
            last(1 - parity)


def _fwd_kernel(x_ref, sh_ref, sc_ref, n1_ref, wuv_ref, wqk_ref, wvr_ref, wgf_ref, wgb_ref, gain_ref, sgw_ref, sgb_ref,
                rcq_ref, rsq_ref, rck_ref, rsk_ref, tab_ref, s0_ref, wo32_ref, wg32_ref, wu32_ref, wd32_ref,
                ya_ref, qkvg_ref, wo16_ref, wg16_ref, wu16_ref, wd16_ref, s_ref, f32_ref, b16_ref, *, n_chunks):
    a_width = GROUPS * LANES
    r_width = HEADS * HEAD_DIM
    U, GF = 0, a_width
    VG, Q, K, XQ, VB, VZ = (i * r_width for i in range(6))
    rows_all = slice(None)

    def convert(out):
        for src, dst in ((wo32_ref, wo16_ref), (wg32_ref, wg16_ref), (wu32_ref, wu16_ref), (wd32_ref, wd16_ref)):
            out.add(dst, (rows_all, rows_all), src[...].astype(_BF16))

    def project(slot, out):
        n_sub = n_chunks // PROJ_CHUNKS
        for blk in range(n_sub):
            rows = slice(blk * PROJ_CHUNKS * CHUNK, (blk + 1) * PROJ_CHUNKS * CHUNK)

            def per_chunk(t, table):
                return jnp.concatenate([t[n * CHUNK:(n + 1) * CHUNK] * table for n in range(PROJ_CHUNKS)], axis=0)

            x = x_ref[rows, :]
            r = lax.rsqrt(jnp.mean(x * x, axis=-1, keepdims=True) + RMS_EPS)
            hx = ((x * r) * (n1_ref[...] * (1.0 + sc_ref[...])) + sh_ref[...]).astype(_BF16)

            def proj(w_ref, lo, width):
                return _dot(hx, w_ref[:, lo:lo + width])

            yield
            out.add(f32_ref, (slot, rows, slice(U, U + a_width)), _gelu_tanh(proj(wuv_ref, 0, a_width)))
            v = _gelu_tanh(proj(wuv_ref, a_width, a_width))
            vg = jnp.concatenate([(_rms(_head(v, g)) * _head(gain_ref[...], g)).astype(_BF16)
                                  for g in range(GROUPS)], axis=-1)
            out.add(b16_ref, (slot, rows, slice(VG, VG + a_width)), vg)
            yield

            def rope(t, c_ref, s_ref_):
                return jnp.concatenate(
                    [_head(t, h) * c_ref[rows, :] + pltpu.roll(_head(t, h), HEAD_DIM // 2, 1) * s_ref_[rows, :]
                     for h in range(HEADS)], axis=-1)

            q = rope(proj(wqk_ref, 0, r_width), rcq_ref, rsq_ref)
            q16 = q.astype(_BF16)
            k16 = rope(proj(wqk_ref, r_width, r_width), rck_ref, rsk_ref).astype(_BF16)
            out.add(b16_ref, (slot, rows, slice(Q, Q + r_width)), q16)
            out.add(b16_ref, (slot, rows, slice(K, K + r_width)), k16)
            out.add(b16_ref, (slot, rows, slice(XQ, XQ + r_width)), per_chunk(q, tab_ref[T_XI_F]).astype(_BF16))
            out.add(qkvg_ref, (rows, slice(0, r_width)), q16)
            out.add(qkvg_ref, (rows, slice(r_width, 2 * r_width)), k16)
            yield
            vr = proj(wvr_ref, 0, r_width)
            v16 = vr.astype(_BF16)
            out.add(b16_ref, (slot, rows, slice(VB, VB + r_width)), v16)
            out.add(b16_ref, (slot, rows, slice(VZ, VZ + r_width)), per_chunk(vr, tab_ref[T_ZETA_F]).astype(_BF16))
            out.add(f32_ref, (slot, rows, slice(GF, GF + r_width)), proj(wgf_ref, 0, r_width))
            out.add(qkvg_ref, (rows, slice(2 * r_width, 3 * r_width)), v16)
            out.add(qkvg_ref, (rows, slice(3 * r_width, 4 * r_width)),
                    proj(wgb_ref, 0, r_width).astype(_BF16))

    def gate_mix(slot, out):
        mixed = []
        for g in range(GROUPS):
            vg = jnp.concatenate([b16_ref[slot, n * CHUNK:(n + 1) * CHUNK, VG + g * LANES:VG + (g + 1) * LANES]
                                  for n in range(n_chunks)], axis=1)
            mixed.append(_dot(sgw_ref[g].astype(_BF16), vg))
        yield
        for g in range(GROUPS):
            cols = slice(g * LANES, (g + 1) * LANES)
            for n in range(n_chunks):
                rows = slice(n * CHUNK, (n + 1) * CHUNK)
                m = mixed[g][:, n * LANES:(n + 1) * LANES] + sgb_ref[:, cols]
                u = f32_ref[slot, rows, U + g * LANES:U + (g + 1) * LANES]
                out.add(ya_ref, (rows, cols), (u * m).astype(_BF16))

    def retain(slot, out):
        def load(n):
            rows = slice(n * CHUNK, (n + 1) * CHUNK)
            return tuple(b16_ref[slot, rows, c:c + r_width] for c in (Q, K, XQ, VB, VZ)) + (
                f32_ref[slot, rows, GF:GF + r_width],)

        def emit(n, y):
            out.add(ya_ref, (slice(n * CHUNK, (n + 1) * CHUNK), slice(a_width, a_width + r_width)), y.astype(_BF16))

        return _retention_tile(load, range(n_chunks), s_ref, tab_ref, T_DMASK_F, T_DECAY_F, emit, out)

    def first(slot):
        s_ref[...] = s0_ref[...]
        out = _DeferredStores()
        convert(out)
        _interleave(project(slot, out))
        out.commit()

    def middle(slot):
        out = _DeferredStores()
        convert(out)
        _interleave(retain(1 - slot, out), gate_mix(1 - slot, out), project(slot, out))
        out.commit()

    def last(slot):
        out = _DeferredStores()
        convert(out)
        _interleave(retain(slot, out), gate_mix(slot, out))
        out.commit()

    _skewed_steps(first, middle, last)


def _forward_pass(x, mod3, norm1, w16, sg_gain, sg_w, sg_bias, rope, tables, s_cf, bwd_weights, tile):
    b, l, d = x.shape
    nt = l // tile
    a_width, r_width = GROUPS * LANES, HEADS * HEAD_DIM
    const2 = lambda i, s: (0, 0)
    cur = lambda i, s: (i, jnp.minimum(s, nt - 1), 0)
    prev = lambda i, s: (i, jnp.maximum(s - 1, 0), 0)
    rope_spec = pl.BlockSpec((tile, HEAD_DIM), lambda i, s: (jnp.minimum(s, nt - 1), 0))
    resident = dict(pipeline_mode=pl.Buffered(1))
    n_blk = max(n for n in (1, 2, 4, 8, 16) if n <= b * (nt + 1))
    blk = lambda i, s: (jnp.minimum(i * (nt + 1) + s, n_blk - 1), 0)
    w_specs = [pl.BlockSpec((w.shape[0] // n_blk, w.shape[1]), blk) for w in bwd_weights]
    return pl.pallas_call(
        functools.partial(_fwd_kernel, n_chunks=tile // CHUNK),
        grid=(b, nt + 1),
        in_specs=[pl.BlockSpec((None, tile, d), cur),
                  pl.BlockSpec((None, 1, d), lambda i, s: (i, 0, 0)),
                  pl.BlockSpec((None, 1, d), lambda i, s: (i, 0, 1)),
                  pl.BlockSpec((1, d), const2),
                  pl.BlockSpec((d, 2 * r_width), lambda i, s: (0, U_BLOCK // 2), **resident),
                  pl.BlockSpec((d, 2 * r_width), lambda i, s: (0, Q_BLOCK // 2), **resident),
                  pl.BlockSpec((d, r_width), lambda i, s: (0, VR_BLOCK), **resident),
                  pl.BlockSpec((d, r_width), lambda i, s: (0, GF_BLOCK), **resident),
                  pl.BlockSpec((d, r_width), lambda i, s: (0, GB_BLOCK), **resident),
                  pl.BlockSpec((1, a_width), const2),
                  pl.BlockSpec(sg_w.shape, lambda i, s: (0, 0, 0)),
                  pl.BlockSpec(sg_bias.shape, const2),
                  rope_spec, rope_spec, rope_spec, rope_spec,
                  pl.BlockSpec(tables.shape, lambda i, s: (0, 0, 0)),
                  pl.BlockSpec((None, HEADS, HEAD_DIM, HEAD_DIM), lambda i, s: (i, 0, 0, 0))] + w_specs,
        out_specs=[pl.BlockSpec((None, tile, a_width + r_width), prev),
                   pl.BlockSpec((None, tile, 4 * r_width), cur)] + w_specs,
        out_shape=[jax.ShapeDtypeStruct((b, l, a_width + r_width), _BF16),
                   jax.ShapeDtypeStruct((b, l, 4 * r_width), _BF16)]
                  + [jax.ShapeDtypeStruct(w.shape, _BF16) for w in bwd_weights],
        scratch_shapes=[pltpu.VMEM((HEADS, HEAD_DIM, HEAD_DIM), _F32),
                        pltpu.VMEM((2, tile, a_width + r_width), _F32),
                        pltpu.VMEM((2, tile, 6 * r_width), _BF16)],
        compiler_params=pltpu.CompilerParams(dimension_semantics=("arbitrary", "arbitrary"),
                                             vmem_limit_bytes=VMEM_LIMIT_BYTES),
        name="fwd",
    )(x, mod3, mod3, norm1, w16, w16, w16, w16, w16, sg_gain, sg_w, sg_bias, *rope, tables, s_cf, *bwd_weights)


def _bwd_kernel(x_ref, ya_ref, qkvg_ref, g1_ref, sh_ref, sc_ref, g2_ref, n2_ref, nf_ref,
                wout_ref, wg_ref, wu_ref, wd_ref, tab_ref, s0_ref, o_ref, s_ref, yb_ref, *, n_chunks):
    a_width = GROUPS * LANES
    r_width = HEADS * HEAD_DIM

    def retain(slot, out):
        def load(n):
            rows = slice(n * CHUNK, (n + 1) * CHUNK)
            q, k, v, gate = (qkvg_ref[rows, i * r_width:(i + 1) * r_width] for i in range(4))
            return (q, k, (q.astype(_F32) * tab_ref[T_XI_B]).astype(_BF16),
                    v, (v.astype(_F32) * tab_ref[T_ZETA_B]).astype(_BF16), gate.astype(_F32))

        def emit(n, y):
            out.add(yb_ref, (slot, slice(n * CHUNK, (n + 1) * CHUNK), slice(None)), y)

        return _retention_tile(load, list(reversed(range(n_chunks))), s_ref, tab_ref, T_DMASK_B, T_DECAY_B,
                               emit, out)

    def finish(slot, out):
        y_r = ya_ref[:, a_width:].astype(_F32) + yb_ref[slot]
        y = _dot(ya_ref[:, :a_width], wout_ref[:a_width, :]) + _dot(y_r.astype(_BF16), wout_ref[a_width:, :])
        x1 = x_ref[...] + g1_ref[...] * y
        h2 = ((_rms(x1) * n2_ref[...]) * (1.0 + sc_ref[...]) + sh_ref[...]).astype(_BF16)
        acc = None
        d_ff = wg_ref.shape[1]
        for lo in range(0, d_ff, FFN_BLOCK):
            yield
            cols = slice(lo, min(lo + FFN_BLOCK, d_ff))
            act = (_silu(_dot(h2, wg_ref[:, cols])) * _dot(h2, wu_ref[:, cols])).astype(_BF16)
            part = _dot(act, wd_ref[cols, :])
            acc = part if acc is None else acc + part
        x2 = x1 + g2_ref[...] * acc
        out.add(o_ref, (slice(None), slice(None)), _rms(x2) * nf_ref[...])

    def first(slot):
        s_ref[...] = s0_ref[...]
        out = _DeferredStores()
        _interleave(retain(slot, out))
        out.commit()

    def middle(slot):
        out = _DeferredStores()
        _interleave(retain(slot, out), finish(1 - slot, out))
        out.commit()

    def last(slot):
        out = _DeferredStores()
        _interleave(finish(slot, out))
        out.commit()

    _skewed_steps(first, middle, last)


def _backward_pass(x, ya, qkvg, mod3, norm2, norm_f, w_out, w_gate, w_up, w_down, tables, s_cb, tile):
    b, l, d = x.shape
    nt = l // tile
    const2 = lambda i, s: (0, 0)
    cur = lambda i, s: (i, nt - 1 - jnp.minimum(s, nt - 1), 0)
    prev = lambda i, s: (i, nt - 1 - jnp.maximum(s - 1, 0), 0)
    resident = dict(pipeline_mode=pl.Buffered(1))
    mod_spec = lambda j: pl.BlockSpec((None, 1, d), lambda i, s: (i, 0, j))
    return pl.pallas_call(
        functools.partial(_bwd_kernel, n_chunks=tile // CHUNK),
        grid=(b, nt + 1),
        in_specs=[pl.BlockSpec((None, tile, d), prev),
                  pl.BlockSpec((None, tile, ya.shape[-1]), prev),
                  pl.BlockSpec((None, tile, qkvg.shape[-1]), cur),
                  mod_spec(2), mod_spec(3), mod_spec(4), mod_spec(5),
                  pl.BlockSpec((1, d), const2),
                  pl.BlockSpec((1, d), const2),
                  pl.BlockSpec(w_out.shape, const2, **resident),
                  pl.BlockSpec(w_gate.shape, const2, **resident),
                  pl.BlockSpec(w_up.shape, const2, **resident),
                  pl.BlockSpec(w_down.shape, const2, **resident),
                  pl.BlockSpec(tables.shape, lambda i, s: (0, 0, 0)),
                  pl.BlockSpec((None, HEADS, HEAD_DIM, HEAD_DIM), lambda i, s: (i, 0, 0, 0))],
        out_specs=pl.BlockSpec((None, tile, d), prev),
        out_shape=jax.ShapeDtypeStruct((b, l, d), x.dtype),
        scratch_shapes=[pltpu.VMEM((HEADS, HEAD_DIM, HEAD_DIM), _F32),
                        pltpu.VMEM((2, tile, HEADS * HEAD_DIM), _F32)],
        compiler_params=pltpu.CompilerParams(dimension_semantics=("arbitrary", "arbitrary"),
                                             vmem_limit_bytes=VMEM_LIMIT_BYTES),
        name="bwd",
    )(x, ya, qkvg, mod3, mod3, mod3, mod3, norm2, norm_f, w_out, w_gate, w_up, w_down, tables, s_cb)


def _rope_tables(l, k_scale):
    n_freq = HEAD_DIM // 4
    inv = np.float32(ROPE_BASE) ** (-np.arange(n_freq, dtype=np.float32) / np.float32(n_freq))
    rows = np.repeat(np.arange(l // GRID_W, dtype=np.float32), GRID_W)
    cols = np.tile(np.arange(GRID_W, dtype=np.float32), l // GRID_W)
    ang_r, ang_c = rows[:, None] * inv[None, :], cols[:, None] * inv[None, :]
    cos = np.concatenate([np.cos(ang_r), np.cos(ang_c)], axis=-1)
    sin = np.concatenate([np.sin(ang_r), np.sin(ang_c)], axis=-1)
    cos2, sin2 = np.concatenate([cos, cos], axis=-1), np.concatenate([-sin, sin], axis=-1)
    return tuple(t.astype(np.float32) for t in (cos2, sin2, cos2 * k_scale, sin2 * k_scale))


def kernel(x, c, ctx, c_ctx, w_mod, b_mod, norm1, w_in, sg_gain, sg_w, sg_b, ret_logit_f, ret_logit_b,
           w_out, norm2, w_gate, w_up, w_down, norm_f):
    b, l, d = x.shape
    a_width, r_width = GROUPS * LANES, HEADS * HEAD_DIM
    tile = min(TILE, l)
    assert w_mod.shape[0] == 1 and b + 1 <= MOD_ROWS and l % tile == 0 and tile % CHUNK == 0
    assert ctx.shape[1] % CHUNK == 0 and l // tile >= 2
    assert a_width == r_width and d == a_width + r_width
    k_scale = HEAD_DIM ** -0.5

    mod3, w16, tables = _prepare(c, c_ctx, w_mod[0], b_mod[0][None, :], w_in[0], ret_logit_f[0], ret_logit_b[0])
    s_cf, s_cb = _context_states(ctx, mod3, norm1, w16, tables, k_scale)
    sg_bias = jnp.broadcast_to(jnp.transpose(sg_b[0])[:, :, None], (CHUNK, GROUPS, LANES)).reshape(CHUNK, a_width)
    ya, qkvg, w_out_b, w_gate_b, w_up_b, w_down_b = _forward_pass(
        x, mod3, norm1, w16, sg_gain, sg_w[0], sg_bias, _rope_tables(l, k_scale),
        tables, s_cf, (w_out[0], w_gate[0], w_up[0], w_down[0]), tile)
    return _backward_pass(x, ya, qkvg, mod3, norm2, norm_f[None, :], w_out_b, w_gate_b, w_up_b, w_down_b,
                          tables, s_cb, tile)
```

```python
import functools

import numpy as np
import jax
import jax.numpy as jnp
from jax import lax
from jax.experimental import pallas as pl
from jax.experimental.pallas import tpu as pltpu

CHUNK = 128
GRID_W = 64
GROUPS = 4
HEADS = 4
HEAD_DIM = 128
RMS_EPS = 1e-6
ROPE_BASE = 10000.0
LANES = 128
MOD_ROWS = 8
TILE = 512
PROJ_CHUNKS = 2
FFN_BLOCK = 512
VMEM_LIMIT_BYTES = 56 * 1024 * 1024

IN_BLOCKS = 7
U_BLOCK, V_BLOCK, Q_BLOCK, K_BLOCK, VR_BLOCK, GF_BLOCK, GB_BLOCK = range(IN_BLOCKS)

T_DMASK_F, T_DMASK_B, T_XI_F, T_XI_B, T_ZETA_F, T_ZETA_B, T_DECAY_F, T_DECAY_B = range(8)

_BF16 = jnp.bfloat16
_F32 = jnp.float32


def _dot(a, b):
    return jnp.dot(a, b, preferred_element_type=_F32)


def _dot_nt(a, b):
    return lax.dot_general(a, b, (((1,), (1,)), ((), ())), preferred_element_type=_F32)


def _dot_tn(a, b):
    return lax.dot_general(a, b, (((0,), (0,)), ((), ())), preferred_element_type=_F32)


def _silu(x):
    return x * (1.0 / (1.0 + jnp.exp(-x)))


def _gelu_tanh(x):
    c = float(np.sqrt(2.0 / np.pi))
    half = 0.5 * x
    return half + half * jnp.tanh(x * (c + (0.044715 * c) * (x * x)))


def _rms(x):
    return x * lax.rsqrt(jnp.mean(x * x, axis=-1, keepdims=True) + RMS_EPS)


def _head(a, h):
    return a[:, h * HEAD_DIM:(h + 1) * HEAD_DIM]


def _prep_kernel(c_ref, cctx_ref, wmod_ref, bmod_ref, win_ref, lf_ref, lb_ref, mod_ref, w16_ref, tab_ref):
    j = pl.program_id(0)

    n_batch, d = c_ref.shape
    cc = jnp.concatenate([c_ref[...], cctx_ref[...], jnp.zeros((MOD_ROWS - n_batch - 1, d), _F32)], axis=0)
    mod = _dot(_silu(cc), wmod_ref[...]) + bmod_ref[...]
    for r in range(MOD_ROWS):
        mod_ref[r] = mod[r:r + 1, :]

    def reorder_head_dims(t):
        quarter = HEAD_DIM // 4
        lane = lax.broadcasted_iota(jnp.int32, (t.shape[0], HEAD_DIM), 1)
        heads = []
        for h in range(t.shape[1] // HEAD_DIM):
            th = _head(t, h)
            from_right = pltpu.roll(th, HEAD_DIM - quarter, 1)
            from_left = pltpu.roll(th, quarter, 1)
            heads.append(jnp.where((lane >= quarter) & (lane < 2 * quarter), from_right,
                                   jnp.where((lane >= 2 * quarter) & (lane < 3 * quarter), from_left, th)))
        return jnp.concatenate(heads, axis=-1)

    is_qk = (j == Q_BLOCK) | (j == K_BLOCK)

    @pl.when(is_qk)
    def _():
        w16_ref[...] = reorder_head_dims(win_ref[...]).astype(_BF16)

    @pl.when(jnp.logical_not(is_qk))
    def _():
        w16_ref[...] = win_ref[...].astype(_BF16)

    @pl.when(j == 0)
    def _():
        def log_sigmoid(x):
            return -(jnp.maximum(-x, 0.0) + jnp.log1p(jnp.exp(-jnp.abs(x))))

        shape = (CHUNK, HEADS * HEAD_DIM)
        head = lax.broadcasted_iota(jnp.int32, shape, 1) // HEAD_DIM

        def per_head(logit_ref):
            t = jnp.zeros(shape, _F32)
            for h in range(HEADS):
                t = jnp.where(head == h, logit_ref[h], t)
            return t

        lg_f = log_sigmoid(per_head(lf_ref))
        lg_b = log_sigmoid(per_head(lb_ref))
        i = lax.broadcasted_iota(jnp.int32, shape, 0).astype(_F32)
        jj = (lax.broadcasted_iota(jnp.int32, shape, 1) & (HEAD_DIM - 1)).astype(_F32)
        tab_ref[T_DMASK_F] = jnp.where(i >= jj, jnp.exp(lg_f * jnp.maximum(i - jj, 0.0)), 0.0)
        tab_ref[T_DMASK_B] = jnp.where(jj >= i, jnp.exp(lg_b * jnp.maximum(jj - i, 0.0)), 0.0)
        tab_ref[T_XI_F] = jnp.exp(lg_f * (i + 1.0))
        tab_ref[T_XI_B] = jnp.exp(lg_b * (CHUNK - i))
        tab_ref[T_ZETA_F] = jnp.exp(lg_f * (CHUNK - 1.0 - i))
        tab_ref[T_ZETA_B] = jnp.exp(lg_b * i)
        tab_ref[T_DECAY_F] = jnp.exp(lg_f * CHUNK)
        tab_ref[T_DECAY_B] = jnp.exp(lg_b * CHUNK)


def _prepare(c, c_ctx, w_mod, b_mod, w_in, logit_f, logit_b):
    d, n_mod = w_mod.shape
    r_width = HEADS * HEAD_DIM
    n_mod_blocks = n_mod // d
    assert w_in.shape == (d, IN_BLOCKS * r_width) and n_mod_blocks <= IN_BLOCKS
    mod_blk = lambda j: jnp.minimum(j, n_mod_blocks - 1)
    return pl.pallas_call(
        _prep_kernel,
        grid=(IN_BLOCKS,),
        in_specs=[pl.BlockSpec(c.shape, lambda j: (0, 0)),
                  pl.BlockSpec((1, d), lambda j: (0, 0)),
                  pl.BlockSpec((d, d), lambda j: (0, mod_blk(j))),
                  pl.BlockSpec((1, d), lambda j: (0, mod_blk(j))),
                  pl.BlockSpec((d, r_width), lambda j: (0, j)),
                  pl.BlockSpec(memory_space=pltpu.SMEM),
                  pl.BlockSpec(memory_space=pltpu.SMEM)],
        out_specs=[pl.BlockSpec((MOD_ROWS, 1, d), lambda j: (0, 0, mod_blk(j))),
                   pl.BlockSpec((d, r_width), lambda j: (0, j)),
                   pl.BlockSpec((8, CHUNK, r_width), lambda j: (0, 0, 0))],
        out_shape=[jax.ShapeDtypeStruct((MOD_ROWS, 1, n_mod), _F32),
                   jax.ShapeDtypeStruct(w_in.shape, _BF16),
                   jax.ShapeDtypeStruct((8, CHUNK, r_width), _F32)],
        compiler_params=pltpu.CompilerParams(dimension_semantics=("arbitrary",),
                                             vmem_limit_bytes=VMEM_LIMIT_BYTES),
        name="prep",
    )(c, c_ctx[None, :], w_mod, b_mod, w_in, logit_f.astype(_F32), logit_b.astype(_F32))


def _context_states(ctx, shift, scale, gain, w_k, w_v, tab_ref, k_scale):
    hc = ((_rms(ctx) * gain) * (1.0 + scale) + shift).astype(_BF16)
    k = (_dot(hc, w_k) * k_scale).astype(_BF16)
    v = _dot(hc, w_v)
    n_chunks = ctx.shape[0] // CHUNK
    s_fwd, s_bwd = [], []
    for h in range(HEADS):
        s_f = jnp.zeros((HEAD_DIM, HEAD_DIM), _F32)
        s_b = jnp.zeros((HEAD_DIM, HEAD_DIM), _F32)
        for n in range(n_chunks):
            rows = slice(n * CHUNK, (n + 1) * CHUNK)
            upd = _dot_tn(_head(k[rows], h), (_head(v[rows], h) * _head(tab_ref[T_ZETA_F], h)).astype(_BF16))
            s_f = _head(tab_ref[T_DECAY_F], h) * s_f + upd
        for n in reversed(range(n_chunks)):
            rows = slice(n * CHUNK, (n + 1) * CHUNK)
            upd = _dot_tn(_head(k[rows], h), (_head(v[rows], h) * _head(tab_ref[T_ZETA_B], h)).astype(_BF16))
            s_b = _head(tab_ref[T_DECAY_B], h) * s_b + upd
        s_fwd.append(s_f)
        s_bwd.append(s_b)
    return s_fwd, s_bwd


def _retention_tile(load, chunk_order, s_ref, tab_ref, t_dmask, t_decay, emit, out):
    keys = [(n, h) for n in chunk_order for h in range(HEADS)]
    q, k, xq, v, vz, gate = {}, {}, {}, {}, {}, {}
    for n in chunk_order:
        q[n], k[n], xq[n], v[n], vz[n], gate[n] = load(n)
    scores = {(n, h): _dot_nt(_head(q[n], h), _head(k[n], h)) for n, h in keys}
    upd = {(n, h): _dot_tn(_head(k[n], h), _head(vz[n], h)) for n, h in keys}
    yield
    state = [s_ref[h] for h in range(HEADS)]
    lhs, rhs = {}, {}
    for n, h in keys:
        p = (scores[n, h] * _head(tab_ref[t_dmask], h)).astype(_BF16)
        lhs[n, h] = jnp.concatenate([p, _head(xq[n], h)], axis=1)
        rhs[n, h] = jnp.concatenate([_head(v[n], h), state[h].astype(_BF16)], axis=0)
        state[h] = _head(tab_ref[t_decay], h) * state[h] + upd[n, h]
    yield
    o = {key: _dot(lhs[key], rhs[key]) for key in keys}
    yield
    for n in chunk_order:
        emit(n, jnp.concatenate([_silu(_head(gate[n], h)) * _rms(o[n, h]) for h in range(HEADS)], axis=-1))
    for h in range(HEADS):
        out.add(s_ref, h, state[h])


def _interleave(*stages):
    stages = list(stages)
    while stages:
        for g in list(stages):
            try:
                next(g)
            except StopIteration:
                stages.remove(g)


class _DeferredStores:
    def __init__(self):
        self._items = []

    def add(self, ref, idx, value):
        self._items.append((ref, idx, value))

    def commit(self):
        for ref, idx, value in self._items:
            ref[idx] = value


def _skewed_steps(first, middle, last):
    s = pl.program_id(1)
    n_tiles = pl.num_programs(1) - 1

    @pl.when(s == 0)
    def _():
        first(0)

    for parity in range(2):
        @pl.when((s > 0) & (s < n_tiles) & ((s & 1) == parity))
        def _():
            middle(parity)

    for parity in range(2):
        @pl.when((s == n_tiles) & ((s & 1) == parity))
        def _():
            last(1 - parity)


def _fwd_kernel(x_ref, sh_ref, sc_ref, n1_ref, wuv_ref, wqk_ref, wvr_ref, wgf_ref, wgb_ref, gain_ref, sgw_ref, sgb_ref,
                rope_ref, tab_ref, ctx_ref, csh_ref, csc_ref, wo32_ref, wg32_ref, wu32_ref, wd32_ref,
                ya_ref, qkvg_ref, scb_ref, wo16_ref, wg16_ref, wu16_ref, wd16_ref, s_ref, f32_ref, b16_ref, *,
                n_chunks, k_scale):
    a_width = GROUPS * LANES
    r_width = HEADS * HEAD_DIM
    U, GF = 0, a_width
    VG, Q, K, XQ, VB, VZ = (i * r_width for i in range(6))
    rows_all = slice(None)

    def convert(out):
        for src, dst in ((wo32_ref, wo16_ref), (wg32_ref, wg16_ref), (wu32_ref, wu16_ref), (wd32_ref, wd16_ref)):
            out.add(dst, (rows_all, rows_all), src[...].astype(_BF16))

    def project(slot, out):
        n_sub = n_chunks // PROJ_CHUNKS
        for blk in range(n_sub):
            rows = slice(blk * PROJ_CHUNKS * CHUNK, (blk + 1) * PROJ_CHUNKS * CHUNK)

            def per_chunk(t, table):
                return jnp.concatenate([t[n * CHUNK:(n + 1) * CHUNK] * table for n in range(PROJ_CHUNKS)], axis=0)

            x = x_ref[rows, :]
            r = lax.rsqrt(jnp.mean(x * x, axis=-1, keepdims=True) + RMS_EPS)
            hx = ((x * r) * (n1_ref[...] * (1.0 + sc_ref[...])) + sh_ref[...]).astype(_BF16)

            def proj(w_ref, lo, width):
                return _dot(hx, w_ref[:, lo:lo + width])

            yield
            out.add(f32_ref, (slot, rows, slice(U, U + a_width)), _gelu_tanh(proj(wuv_ref, 0, a_width)))
            v = _gelu_tanh(proj(wuv_ref, a_width, a_width))
            vg = jnp.concatenate([(_rms(_head(v, g)) * _head(gain_ref[...], g)).astype(_BF16)
                                  for g in range(GROUPS)], axis=-1)
            out.add(b16_ref, (slot, rows, slice(VG, VG + a_width)), vg)
            yield

            def rope(t, table):
                cos = rope_ref[rows, 2 * table * HEAD_DIM:(2 * table + 1) * HEAD_DIM]
                sin = rope_ref[rows, (2 * table + 1) * HEAD_DIM:(2 * table + 2) * HEAD_DIM]
                return jnp.concatenate(
                    [_head(t, h) * cos + pltpu.roll(_head(t, h), HEAD_DIM // 2, 1) * sin for h in range(HEADS)],
                    axis=-1)

            q = rope(proj(wqk_ref, 0, r_width), 0)
            q16 = q.astype(_BF16)
            k16 = rope(proj(wqk_ref, r_width, r_width), 1).astype(_BF16)
            out.add(b16_ref, (slot, rows, slice(Q, Q + r_width)), q16)
            out.add(b16_ref, (slot, rows, slice(K, K + r_width)), k16)
            out.add(b16_ref, (slot, rows, slice(XQ, XQ + r_width)), per_chunk(q, tab_ref[T_XI_F]).astype(_BF16))
            out.add(qkvg_ref, (rows, slice(0, r_width)), q16)
            out.add(qkvg_ref, (rows, slice(r_width, 2 * r_width)), k16)
            yield
            vr = proj(wvr_ref, 0, r_width)
            v16 = vr.astype(_BF16)
            out.add(b16_ref, (slot, rows, slice(VB, VB + r_width)), v16)
            out.add(b16_ref, (slot, rows, slice(VZ, VZ + r_width)), per_chunk(vr, tab_ref[T_ZETA_F]).astype(_BF16))
            out.add(f32_ref, (slot, rows, slice(GF, GF + r_width)), proj(wgf_ref, 0, r_width))
            out.add(qkvg_ref, (rows, slice(2 * r_width, 3 * r_width)), v16)
            out.add(qkvg_ref, (rows, slice(3 * r_width, 4 * r_width)),
                    proj(wgb_ref, 0, r_width).astype(_BF16))

    def gate_mix(slot, out):
        mixed = []
        for g in range(GROUPS):
            vg = jnp.concatenate([b16_ref[slot, n * CHUNK:(n + 1) * CHUNK, VG + g * LANES:VG + (g + 1) * LANES]
                                  for n in range(n_chunks)], axis=1)
            mixed.append(_dot(sgw_ref[g].astype(_BF16), vg))
        yield
        for g in range(GROUPS):
            cols = slice(g * LANES, (g + 1) * LANES)
            for n in range(n_chunks):
                rows = slice(n * CHUNK, (n + 1) * CHUNK)
                m = mixed[g][:, n * LANES:(n + 1) * LANES] + sgb_ref[:, cols]
                u = f32_ref[slot, rows, U + g * LANES:U + (g + 1) * LANES]
                out.add(ya_ref, (rows, cols), (u * m).astype(_BF16))

    def retain(slot, out):
        def load(n):
            rows = slice(n * CHUNK, (n + 1) * CHUNK)
            return tuple(b16_ref[slot, rows, c:c + r_width] for c in (Q, K, XQ, VB, VZ)) + (
                f32_ref[slot, rows, GF:GF + r_width],)

        def emit(n, y):
            out.add(ya_ref, (slice(n * CHUNK, (n + 1) * CHUNK), slice(a_width, a_width + r_width)), y.astype(_BF16))

        return _retention_tile(load, range(n_chunks), s_ref, tab_ref, T_DMASK_F, T_DECAY_F, emit, out)

    def first(slot):
        s_fwd, s_bwd = _context_states(ctx_ref[...], csh_ref[...], csc_ref[...], n1_ref[...],
                                       wqk_ref[:, r_width:], wvr_ref[...], tab_ref, k_scale)
        for h in range(HEADS):
            s_ref[h] = s_fwd[h]
            scb_ref[h] = s_bwd[h]
        out = _DeferredStores()
        convert(out)
        _interleave(project(slot, out))
        out.commit()

    def middle(slot):
        out = _DeferredStores()
        convert(out)
        _interleave(retain(1 - slot, out), gate_mix(1 - slot, out), project(slot, out))
        out.commit()

    def last(slot):
        out = _DeferredStores()
        convert(out)
        _interleave(retain(slot, out), gate_mix(slot, out))
        out.commit()

    _skewed_steps(first, middle, last)


def _forward_pass(x, ctx, mod3, norm1, w16, sg_gain, sg_w, sg_bias, rope, tables, bwd_weights, tile, k_scale):
    b, l, d = x.shape
    nt = l // tile
    a_width, r_width = GROUPS * LANES, HEADS * HEAD_DIM
    ctx_row = b
    state_spec = pl.BlockSpec((None, HEADS, HEAD_DIM, HEAD_DIM), lambda i, s: (i, 0, 0, 0))
    const2 = lambda i, s: (0, 0)
    cur = lambda i, s: (i, jnp.minimum(s, nt - 1), 0)
    prev = lambda i, s: (i, jnp.maximum(s - 1, 0), 0)
    resident = dict(pipeline_mode=pl.Buffered(1))
    n_blk = max(n for n in (1, 2, 4, 8, 16) if n <= b * (nt + 1))
    blk = lambda i, s: (jnp.minimum(i * (nt + 1) + s, n_blk - 1), 0)
    w_specs = [pl.BlockSpec((w.shape[0] // n_blk, w.shape[1]), blk) for w in bwd_weights]
    return pl.pallas_call(
        functools.partial(_fwd_kernel, n_chunks=tile // CHUNK, k_scale=k_scale),
        grid=(b, nt + 1),
        in_specs=[pl.BlockSpec((None, tile, d), cur),
                  pl.BlockSpec((None, 1, d), lambda i, s: (i, 0, 0)),
                  pl.BlockSpec((None, 1, d), lambda i, s: (i, 0, 1)),
                  pl.BlockSpec((1, d), const2),
                  pl.BlockSpec((d, 2 * r_width), lambda i, s: (0, U_BLOCK // 2), **resident),
                  pl.BlockSpec((d, 2 * r_width), lambda i, s: (0, Q_BLOCK // 2), **resident),
                  pl.BlockSpec((d, r_width), lambda i, s: (0, VR_BLOCK), **resident),
                  pl.BlockSpec((d, r_width), lambda i, s: (0, GF_BLOCK), **resident),
                  pl.BlockSpec((d, r_width), lambda i, s: (0, GB_BLOCK), **resident),
                  pl.BlockSpec((1, a_width), const2),
                  pl.BlockSpec(sg_w.shape, lambda i, s: (0, 0, 0)),
                  pl.BlockSpec(sg_bias.shape, const2),
                  pl.BlockSpec((tile, rope.shape[1]), lambda i, s: (jnp.minimum(s, nt - 1), 0)),
                  pl.BlockSpec(tables.shape, lambda i, s: (0, 0, 0)),
                  pl.BlockSpec((None,) + ctx.shape[1:], lambda i, s: (i, 0, 0)),
                  pl.BlockSpec((None, 1, d), lambda i, s: (ctx_row, 0, 0)),
                  pl.BlockSpec((None, 1, d), lambda i, s: (ctx_row, 0, 1))]
                 + w_specs,
        out_specs=[pl.BlockSpec((None, tile, a_width + r_width), prev),
                   pl.BlockSpec((None, tile, 4 * r_width), cur), state_spec] + w_specs,
        out_shape=[jax.ShapeDtypeStruct((b, l, a_width + r_width), _BF16),
                   jax.ShapeDtypeStruct((b, l, 4 * r_width), _BF16),
                   jax.ShapeDtypeStruct((b, HEADS, HEAD_DIM, HEAD_DIM), _F32)]
                  + [jax.ShapeDtypeStruct(w.shape, _BF16) for w in bwd_weights],
        scratch_shapes=[pltpu.VMEM((HEADS, HEAD_DIM, HEAD_DIM), _F32),
                        pltpu.VMEM((2, tile, a_width + r_width), _F32),
                        pltpu.VMEM((2, tile, 6 * r_width), _BF16)],
        compiler_params=pltpu.CompilerParams(dimension_semantics=("arbitrary", "arbitrary"),
                                             vmem_limit_bytes=VMEM_LIMIT_BYTES),
        name="fwd",
    )(x, mod3, mod3, norm1, w16, w16, w16, w16, w16, sg_gain, sg_w, sg_bias, rope, tables, ctx, mod3, mod3,
      *bwd_weights)


def _bwd_kernel(x_ref, ya_ref, qkvg_ref, g1_ref, sh_ref, sc_ref, g2_ref, n2_ref, nf_ref,
                wout_ref, wg_ref, wu_ref, wd_ref, tab_ref, s0_ref, o_ref, s_ref, yb_ref, *, n_chunks):
    a_width = GROUPS * LANES
    r_width = HEADS * HEAD_DIM

    def retain(slot, out):
        def load(n):
            rows = slice(n * CHUNK, (n + 1) * CHUNK)
            q, k, v, gate = (qkvg_ref[rows, i * r_width:(i + 1) * r_width] for i in range(4))
            return (q, k, (q.astype(_F32) * tab_ref[T_XI_B]).astype(_BF16),
                    v, (v.astype(_F32) * tab_ref[T_ZETA_B]).astype(_BF16), gate.astype(_F32))

        def emit(n, y):
            out.add(yb_ref, (slot, slice(n * CHUNK, (n + 1) * CHUNK), slice(None)), y)

        return _retention_tile(load, list(reversed(range(n_chunks))), s_ref, tab_ref, T_DMASK_B, T_DECAY_B,
                               emit, out)

    def finish(slot, out):
        y_r = ya_ref[:, a_width:].astype(_F32) + yb_ref[slot]
        y = _dot(ya_ref[:, :a_width], wout_ref[:a_width, :]) + _dot(y_r.astype(_BF16), wout_ref[a_width:, :])
        x1 = x_ref[...] + g1_ref[...] * y
        h2 = ((_rms(x1) * n2_ref[...]) * (1.0 + sc_ref[...]) + sh_ref[...]).astype(_BF16)
        acc = None
        d_ff = wg_ref.shape[1]
        for lo in range(0, d_ff, FFN_BLOCK):
            yield
            cols = slice(lo, min(lo + FFN_BLOCK, d_ff))
            act = (_silu(_dot(h2, wg_ref[:, cols])) * _dot(h2, wu_ref[:, cols])).astype(_BF16)
            part = _dot(act, wd_ref[cols, :])
            acc = part if acc is None else acc + part
        x2 = x1 + g2_ref[...] * acc
        out.add(o_ref, (slice(None), slice(None)), _rms(x2) * nf_ref[...])

    def first(slot):
        s_ref[...] = s0_ref[...]
        out = _DeferredStores()
        _interleave(retain(slot, out))
        out.commit()

    def middle(slot):
        out = _DeferredStores()
        _interleave(retain(slot, out), finish(1 - slot, out))
        out.commit()

    def last(slot):
        out = _DeferredStores()
        _interleave(finish(slot, out))
        out.commit()

    _skewed_steps(first, middle, last)


def _backward_pass(x, ya, qkvg, mod3, norm2, norm_f, w_out, w_gate, w_up, w_down, tables, s_cb, tile):
    b, l, d = x.shape
    nt = l // tile
    const2 = lambda i, s: (0, 0)
    cur = lambda i, s: (i, nt - 1 - jnp.minimum(s, nt - 1), 0)
    prev = lambda i, s: (i, nt - 1 - jnp.maximum(s - 1, 0), 0)
    resident = dict(pipeline_mode=pl.Buffered(1))
    mod_spec = lambda j: pl.BlockSpec((None, 1, d), lambda i, s: (i, 0, j))
    return pl.pallas_call(
        functools.partial(_bwd_kernel, n_chunks=tile // CHUNK),
        grid=(b, nt + 1),
        in_specs=[pl.BlockSpec((None, tile, d), prev),
                  pl.BlockSpec((None, tile, ya.shape[-1]), prev),
                  pl.BlockSpec((None, tile, qkvg.shape[-1]), cur),
                  mod_spec(2), mod_spec(3), mod_spec(4), mod_spec(5),
                  pl.BlockSpec((1, d), const2),
                  pl.BlockSpec((1, d), const2),
                  pl.BlockSpec(w_out.shape, const2, **resident),
                  pl.BlockSpec(w_gate.shape, const2, **resident),
                  pl.BlockSpec(w_up.shape, const2, **resident),
                  pl.BlockSpec(w_down.shape, const2, **resident),
                  pl.BlockSpec(tables.shape, lambda i, s: (0, 0, 0)),
                  pl.BlockSpec((None, HEADS, HEAD_DIM, HEAD_DIM), lambda i, s: (i, 0, 0, 0))],
        out_specs=pl.BlockSpec((None, tile, d), prev),
        out_shape=jax.ShapeDtypeStruct((b, l, d), x.dtype),
        scratch_shapes=[pltpu.VMEM((HEADS, HEAD_DIM, HEAD_DIM), _F32),
                        pltpu.VMEM((2, tile, HEADS * HEAD_DIM), _F32)],
        compiler_params=pltpu.CompilerParams(dimension_semantics=("arbitrary", "arbitrary"),
                                             vmem_limit_bytes=VMEM_LIMIT_BYTES),
        name="bwd",
    )(x, ya, qkvg, mod3, mod3, mod3, mod3, norm2, norm_f, w_out, w_gate, w_up, w_down, tables, s_cb)


def _rope_tables(l, k_scale):
    n_freq = HEAD_DIM // 4
    inv = np.float32(ROPE_BASE) ** (-np.arange(n_freq, dtype=np.float32) / np.float32(n_freq))
    rows = np.repeat(np.arange(l // GRID_W, dtype=np.float32), GRID_W)
    cols = np.tile(np.arange(GRID_W, dtype=np.float32), l // GRID_W)
    ang_r, ang_c = rows[:, None] * inv[None, :], cols[:, None] * inv[None, :]
    cos = np.concatenate([np.cos(ang_r), np.cos(ang_c)], axis=-1)
    sin = np.concatenate([np.sin(ang_r), np.sin(ang_c)], axis=-1)
    cos2, sin2 = np.concatenate([cos, cos], axis=-1), np.concatenate([-sin, sin], axis=-1)
    return np.concatenate([cos2, sin2, cos2 * k_scale, sin2 * k_scale], axis=-1).astype(np.float32)


def kernel(x, c, ctx, c_ctx, w_mod, b_mod, norm1, w_in, sg_gain, sg_w, sg_b, ret_logit_f, ret_logit_b,
           w_out, norm2, w_gate, w_up, w_down, norm_f):
    b, l, d = x.shape
    a_width, r_width = GROUPS * LANES, HEADS * HEAD_DIM
    tile = min(TILE, l)
    assert w_mod.shape[0] == 1 and b + 1 <= MOD_ROWS and l % tile == 0 and tile % CHUNK == 0
    assert ctx.shape[1] % CHUNK == 0 and l // tile >= 2
    assert a_width == r_width and d == a_width + r_width
    k_scale = HEAD_DIM ** -0.5

    mod3, w16, tables = _prepare(c, c_ctx, w_mod[0], b_mod[0][None, :], w_in[0], ret_logit_f[0], ret_logit_b[0])
    sg_bias = jnp.broadcast_to(jnp.transpose(sg_b[0])[:, :, None], (CHUNK, GROUPS, LANES)).reshape(CHUNK, a_width)
    ya, qkvg, s_cb, w_out_b, w_gate_b, w_up_b, w_down_b = _forward_pass(
        x, ctx, mod3, norm1, w16, sg_gain, sg_w[0], sg_bias, _rope_tables(l, k_scale),
        tables, (w_out[0], w_gate[0], w_up[0], w_down[0]), tile, k_scale)
    return _backward_pass(x, ya, qkvg, mod3, norm2, norm_f[None, :], w_out_b, w_gate_b, w_up_b, w_down_b,
                          tables, s_cb, tile)
```

```python
import functools

import numpy as np
import jax
import jax.numpy as jnp
from jax import lax
from jax.experimental import pallas as pl
from jax.experimental.pallas import tpu as pltpu

CHUNK = 128
GRID_W = 64
GROUPS = 4
HEADS = 4
HEAD_DIM = 128
RMS_EPS = 1e-6
ROPE_BASE = 10000.0
LANES = 128
MOD_ROWS = 8
TILE = 512
PROJ_CHUNKS = 2
FFN_BLOCK = 1024
VMEM_LIMIT_BYTES = 56 * 1024 * 1024

IN_BLOCKS = 7
U_BLOCK, V_BLOCK, Q_BLOCK, K_BLOCK, VR_BLOCK, GF_BLOCK, GB_BLOCK = range(IN_BLOCKS)

T_DMASK_F, T_DMASK_B, T_XI_F, T_XI_B, T_ZETA_F, T_ZETA_B, T_DECAY_F, T_DECAY_B = range(8)

_BF16 = jnp.bfloat16
_F32 = jnp.float32


def _dot(a, b):
    return jnp.dot(a, b, preferred_element_type=_F32)


def _dot_nt(a, b):
    return lax.dot_general(a, b, (((1,), (1,)), ((), ())), preferred_element_type=_F32)


def _dot_tn(a, b):
    return lax.dot_general(a, b, (((0,), (0,)), ((), ())), preferred_element_type=_F32)


def _silu(x):
    return x * (1.0 / (1.0 + jnp.exp(-x)))


def _gelu_tanh(x):
    c = float(np.sqrt(2.0 / np.pi))
    half = 0.5 * x
    return half + half * jnp.tanh(x * (c + (0.044715 * c) * (x * x)))


def _rms(x):
    return x * lax.rsqrt(jnp.mean(x * x, axis=-1, keepdims=True) + RMS_EPS)


def _head(a, h):
    return a[:, h * HEAD_DIM:(h + 1) * HEAD_DIM]


def _prep_kernel(c_ref, cctx_ref, wmod_ref, bmod_ref, win_ref, lf_ref, lb_ref, mod_ref, w16_ref, tab_ref):
    j = pl.program_id(0)

    n_batch, d = c_ref.shape
    cc = jnp.concatenate([c_ref[...], cctx_ref[...], jnp.zeros((MOD_ROWS - n_batch - 1, d), _F32)], axis=0)
    mod = _dot(_silu(cc), wmod_ref[...]) + bmod_ref[...]
    for r in range(MOD_ROWS):
        mod_ref[r] = mod[r:r + 1, :]

    def reorder_head_dims(t):
        quarter = HEAD_DIM // 4
        lane = lax.broadcasted_iota(jnp.int32, (t.shape[0], HEAD_DIM), 1)
        heads = []
        for h in range(t.shape[1] // HEAD_DIM):
            th = _head(t, h)
            from_right = pltpu.roll(th, HEAD_DIM - quarter, 1)
            from_left = pltpu.roll(th, quarter, 1)
            heads.append(jnp.where((lane >= quarter) & (lane < 2 * quarter), from_right,
                                   jnp.where((lane >= 2 * quarter) & (lane < 3 * quarter), from_left, th)))
        return jnp.concatenate(heads, axis=-1)

    is_qk = (j == Q_BLOCK) | (j == K_BLOCK)

    @pl.when(is_qk)
    def _():
        w16_ref[...] = reorder_head_dims(win_ref[...]).astype(_BF16)

    @pl.when(jnp.logical_not(is_qk))
    def _():
        w16_ref[...] = win_ref[...].astype(_BF16)

    @pl.when(j == 0)
    def _():
        def log_sigmoid(x):
            return -(jnp.maximum(-x, 0.0) + jnp.log1p(jnp.exp(-jnp.abs(x))))

        shape = (CHUNK, HEADS * HEAD_DIM)
        head = lax.broadcasted_iota(jnp.int32, shape, 1) // HEAD_DIM

        def per_head(logit_ref):
            t = jnp.zeros(shape, _F32)
            for h in range(HEADS):
                t = jnp.where(head == h, logit_ref[h], t)
            return t

        lg_f = log_sigmoid(per_head(lf_ref))
        lg_b = log_sigmoid(per_head(lb_ref))
        i = lax.broadcasted_iota(jnp.int32, shape, 0).astype(_F32)
        jj = (lax.broadcasted_iota(jnp.int32, shape, 1) & (HEAD_DIM - 1)).astype(_F32)
        tab_ref[T_DMASK_F] = jnp.where(i >= jj, jnp.exp(lg_f * jnp.maximum(i - jj, 0.0)), 0.0)
        tab_ref[T_DMASK_B] = jnp.where(jj >= i, jnp.exp(lg_b * jnp.maximum(jj - i, 0.0)), 0.0)
        tab_ref[T_XI_F] = jnp.exp(lg_f * (i + 1.0))
        tab_ref[T_XI_B] = jnp.exp(lg_b * (CHUNK - i))
        tab_ref[T_ZETA_F] = jnp.exp(lg_f * (CHUNK - 1.0 - i))
        tab_ref[T_ZETA_B] = jnp.exp(lg_b * i)
        tab_ref[T_DECAY_F] = jnp.exp(lg_f * CHUNK)
        tab_ref[T_DECAY_B] = jnp.exp(lg_b * CHUNK)


def _prepare(c, c_ctx, w_mod, b_mod, w_in, logit_f, logit_b):
    d, n_mod = w_mod.shape
    r_width = HEADS * HEAD_DIM
    n_mod_blocks = n_mod // d
    assert w_in.shape == (d, IN_BLOCKS * r_width) and n_mod_blocks <= IN_BLOCKS
    mod_blk = lambda j: jnp.minimum(j, n_mod_blocks - 1)
    return pl.pallas_call(
        _prep_kernel,
        grid=(IN_BLOCKS,),
        in_specs=[pl.BlockSpec(c.shape, lambda j: (0, 0)),
                  pl.BlockSpec((1, d), lambda j: (0, 0)),
                  pl.BlockSpec((d, d), lambda j: (0, mod_blk(j))),
                  pl.BlockSpec((1, d), lambda j: (0, mod_blk(j))),
                  pl.BlockSpec((d, r_width), lambda j: (0, j)),
                  pl.BlockSpec(memory_space=pltpu.SMEM),
                  pl.BlockSpec(memory_space=pltpu.SMEM)],
        out_specs=[pl.BlockSpec((MOD_ROWS, 1, d), lambda j: (0, 0, mod_blk(j))),
                   pl.BlockSpec((d, r_width), lambda j: (0, j)),
                   pl.BlockSpec((8, CHUNK, r_width), lambda j: (0, 0, 0))],
        out_shape=[jax.ShapeDtypeStruct((MOD_ROWS, 1, n_mod), _F32),
                   jax.ShapeDtypeStruct(w_in.shape, _BF16),
                   jax.ShapeDtypeStruct((8, CHUNK, r_width), _F32)],
        compiler_params=pltpu.CompilerParams(dimension_semantics=("arbitrary",),
                                             vmem_limit_bytes=VMEM_LIMIT_BYTES),
        name="prep",
    )(c, c_ctx[None, :], w_mod, b_mod, w_in, logit_f.astype(_F32), logit_b.astype(_F32))


def _context_states(ctx, shift, scale, gain, w_k, w_v, tab_ref, k_scale):
    hc = ((_rms(ctx) * gain) * (1.0 + scale) + shift).astype(_BF16)
    k = (_dot(hc, w_k) * k_scale).astype(_BF16)
    v = _dot(hc, w_v)
    n_chunks = ctx.shape[0] // CHUNK
    s_fwd, s_bwd = [], []
    for h in range(HEADS):
        s_f = jnp.zeros((HEAD_DIM, HEAD_DIM), _F32)
        s_b = jnp.zeros((HEAD_DIM, HEAD_DIM), _F32)
        for n in range(n_chunks):
            rows = slice(n * CHUNK, (n + 1) * CHUNK)
            upd = _dot_tn(_head(k[rows], h), (_head(v[rows], h) * _head(tab_ref[T_ZETA_F], h)).astype(_BF16))
            s_f = _head(tab_ref[T_DECAY_F], h) * s_f + upd
        for n in reversed(range(n_chunks)):
            rows = slice(n * CHUNK, (n + 1) * CHUNK)
            upd = _dot_tn(_head(k[rows], h), (_head(v[rows], h) * _head(tab_ref[T_ZETA_B], h)).astype(_BF16))
            s_b = _head(tab_ref[T_DECAY_B], h) * s_b + upd
        s_fwd.append(s_f)
        s_bwd.append(s_b)
    return s_fwd, s_bwd


def _retention_tile(load, chunk_order, s_ref, tab_ref, t_dmask, t_decay, emit, out):
    keys = [(n, h) for n in chunk_order for h in range(HEADS)]
    q, k, xq, v, vz, gate = {}, {}, {}, {}, {}, {}
    for n in chunk_order:
        q[n], k[n], xq[n], v[n], vz[n], gate[n] = load(n)
    scores = {(n, h): _dot_nt(_head(q[n], h), _head(k[n], h)) for n, h in keys}
    upd = {(n, h): _dot_tn(_head(k[n], h), _head(vz[n], h)) for n, h in keys}
    yield
    state = [s_ref[h] for h in range(HEADS)]
    lhs, rhs = {}, {}
    for n, h in keys:
        p = (scores[n, h] * _head(tab_ref[t_dmask], h)).astype(_BF16)
        lhs[n, h] = jnp.concatenate([p, _head(xq[n], h)], axis=1)
        rhs[n, h] = jnp.concatenate([_head(v[n], h), state[h].astype(_BF16)], axis=0)
        state[h] = _head(tab_ref[t_decay], h) * state[h] + upd[n, h]
    yield
    o = {key: _dot(lhs[key], rhs[key]) for key in keys}
    yield
    for n in chunk_order:
        emit(n, jnp.concatenate([_silu(_head(gate[n], h)) * _rms(o[n, h]) for h in range(HEADS)], axis=-1))
    for h in range(HEADS):
        out.add(s_ref, h, state[h])


def _interleave(*stages):
    stages = list(stages)
    while stages:
        for g in list(stages):
            try:
                next(g)
            except StopIteration:
                stages.remove(g)


class _DeferredStores:
    def __init__(self):
        self._items = []

    def add(self, ref, idx, value):
        self._items.append((ref, idx, value))

    def commit(self):
        for ref, idx, value in self._items:
            ref[idx] = value


def _skewed_steps(first, middle, last):
    s = pl.program_id(1)
    n_tiles = pl.num_programs(1) - 1

    @pl.when(s == 0)
    def _():
        first(0)

    for parity in range(2):
        @pl.when((s > 0) & (s < n_tiles) & ((s & 1) == parity))
        def _():
            middle(parity)

    for parity in range(2):
        @pl.when((s == n_tiles) & ((s & 1) == parity))
        def _():
            last(1 - parity)


def _fwd_kernel(x_ref, sh_ref, sc_ref, n1_ref, wuv_ref, wqk_ref, wvr_ref, wgf_ref, wgb_ref, gain_ref, sgw_ref, sgb_ref,
                rope_ref, tab_ref, ctx_ref, csh_ref, csc_ref, wo32_ref, wg32_ref, wu32_ref, wd32_ref,
                ya_ref, qkvg_ref, scb_ref, wo16_ref, wg16_ref, wu16_ref, wd16_ref, s_ref, f32_ref, b16_ref, *,
                n_chunks, k_scale):
    a_width = GROUPS * LANES
    r_width = HEADS * HEAD_DIM
    U, GF = 0, a_width
    VG, Q, K, XQ, VB, VZ = (i * r_width for i in range(6))
    rows_all = slice(None)

    def convert(out):
        for src, dst in ((wo32_ref, wo16_ref), (wg32_ref, wg16_ref), (wu32_ref, wu16_ref), (wd32_ref, wd16_ref)):
            out.add(dst, (rows_all, rows_all), src[...].astype(_BF16))

    def project(slot, out):
        n_sub = n_chunks // PROJ_CHUNKS
        for blk in range(n_sub):
            rows = slice(blk * PROJ_CHUNKS * CHUNK, (blk + 1) * PROJ_CHUNKS * CHUNK)

            def per_chunk(t, table):
                return jnp.concatenate([t[n * CHUNK:(n + 1) * CHUNK] * table for n in range(PROJ_CHUNKS)], axis=0)

            x = x_ref[rows, :]
            r = lax.rsqrt(jnp.mean(x * x, axis=-1, keepdims=True) + RMS_EPS)
            hx = ((x * r) * (n1_ref[...] * (1.0 + sc_ref[...])) + sh_ref[...]).astype(_BF16)

            def proj(w_ref, lo, width):
                return _dot(hx, w_ref[:, lo:lo + width])

            yield
            out.add(f32_ref, (slot, rows, slice(U, U + a_width)), _gelu_tanh(proj(wuv_ref, 0, a_width)))
            v = _gelu_tanh(proj(wuv_ref, a_width, a_width))
            vg = jnp.concatenate([(_rms(_head(v, g)) * _head(gain_ref[...], g)).astype(_BF16)
                                  for g in range(GROUPS)], axis=-1)
            out.add(b16_ref, (slot, rows, slice(VG, VG + a_width)), vg)
            yield

            def rope(t, table):
                cos = rope_ref[rows, 2 * table * HEAD_DIM:(2 * table + 1) * HEAD_DIM]
                sin = rope_ref[rows, (2 * table + 1) * HEAD_DIM:(2 * table + 2) * HEAD_DIM]
                return jnp.concatenate(
                    [_head(t, h) * cos + pltpu.roll(_head(t, h), HEAD_DIM // 2, 1) * sin for h in range(HEADS)],
                    axis=-1)

            q = rope(proj(wqk_ref, 0, r_width), 0)
            q16 = q.astype(_BF16)
            k16 = rope(proj(wqk_ref, r_width, r_width), 1).astype(_BF16)
            out.add(b16_ref, (slot, rows, slice(Q, Q + r_width)), q16)
            out.add(b16_ref, (slot, rows, slice(K, K + r_width)), k16)
            out.add(b16_ref, (slot, rows, slice(XQ, XQ + r_width)), per_chunk(q, tab_ref[T_XI_F]).astype(_BF16))
            out.add(qkvg_ref, (rows, slice(0, r_width)), q16)
            out.add(qkvg_ref, (rows, slice(r_width, 2 * r_width)), k16)
            yield
            vr = proj(wvr_ref, 0, r_width)
            v16 = vr.astype(_BF16)
            out.add(b16_ref, (slot, rows, slice(VB, VB + r_width)), v16)
            out.add(b16_ref, (slot, rows, slice(VZ, VZ + r_width)), per_chunk(vr, tab_ref[T_ZETA_F]).astype(_BF16))
            out.add(f32_ref, (slot, rows, slice(GF, GF + r_width)), proj(wgf_ref, 0, r_width))
            out.add(qkvg_ref, (rows, slice(2 * r_width, 3 * r_width)), v16)
            out.add(qkvg_ref, (rows, slice(3 * r_width, 4 * r_width)),
                    proj(wgb_ref, 0, r_width).astype(_BF16))

    def gate_mix(slot, out):
        mixed = []
        for g in range(GROUPS):
            vg = jnp.concatenate([b16_ref[slot, n * CHUNK:(n + 1) * CHUNK, VG + g * LANES:VG + (g + 1) * LANES]
                                  for n in range(n_chunks)], axis=1)
            mixed.append(_dot(sgw_ref[g].astype(_BF16), vg))
        yield
        for g in range(GROUPS):
            cols = slice(g * LANES, (g + 1) * LANES)
            for n in range(n_chunks):
                rows = slice(n * CHUNK, (n + 1) * CHUNK)
                m = mixed[g][:, n * LANES:(n + 1) * LANES] + sgb_ref[:, cols]
                u = f32_ref[slot, rows, U + g * LANES:U + (g + 1) * LANES]
                out.add(ya_ref, (rows, cols), (u * m).astype(_BF16))

    def retain(slot, out):
        def load(n):
            rows = slice(n * CHUNK, (n + 1) * CHUNK)
            return tuple(b16_ref[slot, rows, c:c + r_width] for c in (Q, K, XQ, VB, VZ)) + (
                f32_ref[slot, rows, GF:GF + r_width],)

        def emit(n, y):
            out.add(ya_ref, (slice(n * CHUNK, (n + 1) * CHUNK), slice(a_width, a_width + r_width)), y.astype(_BF16))

        return _retention_tile(load, range(n_chunks), s_ref, tab_ref, T_DMASK_F, T_DECAY_F, emit, out)

    def first(slot):
        s_fwd, s_bwd = _context_states(ctx_ref[...], csh_ref[...], csc_ref[...], n1_ref[...],
                                       wqk_ref[:, r_width:], wvr_ref[...], tab_ref, k_scale)
        for h in range(HEADS):
            s_ref[h] = s_fwd[h]
            scb_ref[h] = s_bwd[h]
        out = _DeferredStores()
        convert(out)
        _interleave(project(slot, out))
        out.commit()

    def middle(slot):
        out = _DeferredStores()
        convert(out)
        _interleave(retain(1 - slot, out), gate_mix(1 - slot, out), project(slot, out))
        out.commit()

    def last(slot):
        out = _DeferredStores()
        convert(out)
        _interleave(retain(slot, out), gate_mix(slot, out))
        out.commit()

    _skewed_steps(first, middle, last)


def _forward_pass(x, ctx, mod3, norm1, w16, sg_gain, sg_w, sg_bias, rope, tables, bwd_weights, tile, k_scale):
    b, l, d = x.shape
    nt = l // tile
    a_width, r_width = GROUPS * LANES, HEADS * HEAD_DIM
    ctx_row = b
    state_spec = pl.BlockSpec((None, HEADS, HEAD_DIM, HEAD_DIM), lambda i, s: (i, 0, 0, 0))
    const2 = lambda i, s: (0, 0)
    cur = lambda i, s: (i, jnp.minimum(s, nt - 1), 0)
    prev = lambda i, s: (i, jnp.maximum(s - 1, 0), 0)
    resident = dict(pipeline_mode=pl.Buffered(1))
    n_blk = max(n for n in (1, 2, 4, 8, 16) if n <= b * (nt + 1))
    blk = lambda i, s: (jnp.minimum(i * (nt + 1) + s, n_blk - 1), 0)
    w_specs = [pl.BlockSpec((w.shape[0] // n_blk, w.shape[1]), blk) for w in bwd_weights]
    return pl.pallas_call(
        functools.partial(_fwd_kernel, n_chunks=tile // CHUNK, k_scale=k_scale),
        grid=(b, nt + 1),
        in_specs=[pl.BlockSpec((None, tile, d), cur),
                  pl.BlockSpec((None, 1, d), lambda i, s: (i, 0, 0)),
                  pl.BlockSpec((None, 1, d), lambda i, s: (i, 0, 1)),
                  pl.BlockSpec((1, d), const2),
                  pl.BlockSpec((d, 2 * r_width), lambda i, s: (0, U_BLOCK // 2), **resident),
                  pl.BlockSpec((d, 2 * r_width), lambda i, s: (0, Q_BLOCK // 2), **resident),
                  pl.BlockSpec((d, r_width), lambda i, s: (0, VR_BLOCK), **resident),
                  pl.BlockSpec((d, r_width), lambda i, s: (0, GF_BLOCK), **resident),
                  pl.BlockSpec((d, r_width), lambda i, s: (0, GB_BLOCK), **resident),
                  pl.BlockSpec((1, a_width), const2),
                  pl.BlockSpec(sg_w.shape, lambda i, s: (0, 0, 0)),
                  pl.BlockSpec(sg_bias.shape, const2),
                  pl.BlockSpec((tile, rope.shape[1]), lambda i, s: (jnp.minimum(s, nt - 1), 0)),
                  pl.BlockSpec(tables.shape, lambda i, s: (0, 0, 0)),
                  pl.BlockSpec((None,) + ctx.shape[1:], lambda i, s: (i, 0, 0)),
                  pl.BlockSpec((None, 1, d), lambda i, s: (ctx_row, 0, 0)),
                  pl.BlockSpec((None, 1, d), lambda i, s: (ctx_row, 0, 1))]
                 + w_specs,
        out_specs=[pl.BlockSpec((None, tile, a_width + r_width), prev),
                   pl.BlockSpec((None, tile, 4 * r_width), cur), state_spec] + w_specs,
        out_shape=[jax.ShapeDtypeStruct((b, l, a_width + r_width), _BF16),
                   jax.ShapeDtypeStruct((b, l, 4 * r_width), _BF16),
                   jax.ShapeDtypeStruct((b, HEADS, HEAD_DIM, HEAD_DIM), _F32)]
                  + [jax.ShapeDtypeStruct(w.shape, _BF16) for w in bwd_weights],
        scratch_shapes=[pltpu.VMEM((HEADS, HEAD_DIM, HEAD_DIM), _F32),
                        pltpu.VMEM((2, tile, a_width + r_width), _F32),
                        pltpu.VMEM((2, tile, 6 * r_width), _BF16)],
        compiler_params=pltpu.CompilerParams(dimension_semantics=("arbitrary", "arbitrary"),
                                             vmem_limit_bytes=VMEM_LIMIT_BYTES),
        name="fwd",
    )(x, mod3, mod3, norm1, w16, w16, w16, w16, w16, sg_gain, sg_w, sg_bias, rope, tables, ctx, mod3, mod3,
      *bwd_weights)


def _bwd_kernel(x_ref, ya_ref, qkvg_ref, g1_ref, sh_ref, sc_ref, g2_ref, n2_ref, nf_ref,
                wout_ref, wg_ref, wu_ref, wd_ref, tab_ref, s0_ref, o_ref, s_ref, yb_ref, *, n_chunks):
    a_width = GROUPS * LANES
    r_width = HEADS * HEAD_DIM

    def retain(slot, out):
        def load(n):
            rows = slice(n * CHUNK, (n + 1) * CHUNK)
            q, k, v, gate = (qkvg_ref[rows, i * r_width:(i + 1) * r_width] for i in range(4))
            return (q, k, (q.astype(_F32) * tab_ref[T_XI_B]).astype(_BF16),
                    v, (v.astype(_F32) * tab_ref[T_ZETA_B]).astype(_BF16), gate.astype(_F32))

        def emit(n, y):
            out.add(yb_ref, (slot, slice(n * CHUNK, (n + 1) * CHUNK), slice(None)), y)

        return _retention_tile(load, list(reversed(range(n_chunks))), s_ref, tab_ref, T_DMASK_B, T_DECAY_B,
                               emit, out)

    def finish(slot, out):
        y_r = ya_ref[:, a_width:].astype(_F32) + yb_ref[slot]
        y = _dot(ya_ref[:, :a_width], wout_ref[:a_width, :]) + _dot(y_r.astype(_BF16), wout_ref[a_width:, :])
        x1 = x_ref[...] + g1_ref[...] * y
        h2 = ((_rms(x1) * n2_ref[...]) * (1.0 + sc_ref[...]) + sh_ref[...]).astype(_BF16)
        acc = None
        d_ff = wg_ref.shape[1]
        for lo in range(0, d_ff, FFN_BLOCK):
            yield
            cols = slice(lo, min(lo + FFN_BLOCK, d_ff))
            act = (_silu(_dot(h2, wg_ref[:, cols])) * _dot(h2, wu_ref[:, cols])).astype(_BF16)
            part = _dot(act, wd_ref[cols, :])
            acc = part if acc is None else acc + part
        x2 = x1 + g2_ref[...] * acc
        out.add(o_ref, (slice(None), slice(None)), _rms(x2) * nf_ref[...])

    def first(slot):
        s_ref[...] = s0_ref[...]
        out = _DeferredStores()
        _interleave(retain(slot, out))
        out.commit()

    def middle(slot):
        out = _DeferredStores()
        _interleave(retain(slot, out), finish(1 - slot, out))
        out.commit()

    def last(slot):
        out = _DeferredStores()
        _interleave(finish(slot, out))
        out.commit()

    _skewed_steps(first, middle, last)


def _backward_pass(x, ya, qkvg, mod3, norm2, norm_f, w_out, w_gate, w_up, w_down, tables, s_cb, tile):
    b, l, d = x.shape
    nt = l // tile
    const2 = lambda i, s: (0, 0)
    cur = lambda i, s: (i, nt - 1 - jnp.minimum(s, nt - 1), 0)
    prev = lambda i, s: (i, nt - 1 - jnp.maximum(s - 1, 0), 0)
    resident = dict(pipeline_mode=pl.Buffered(1))
    mod_spec = lambda j: pl.BlockSpec((None, 1, d), lambda i, s: (i, 0, j))
    return pl.pallas_call(
        functools.partial(_bwd_kernel, n_chunks=tile // CHUNK),
        grid=(b, nt + 1),
        in_specs=[pl.BlockSpec((None, tile, d), prev),
                  pl.BlockSpec((None, tile, ya.shape[-1]), prev),
                  pl.BlockSpec((None, tile, qkvg.shape[-1]), cur),
                  mod_spec(2), mod_spec(3), mod_spec(4), mod_spec(5),
                  pl.BlockSpec((1, d), const2),
                  pl.BlockSpec((1, d), const2),
                  pl.BlockSpec(w_out.shape, const2, **resident),
                  pl.BlockSpec(w_gate.shape, const2, **resident),
                  pl.BlockSpec(w_up.shape, const2, **resident),
                  pl.BlockSpec(w_down.shape, const2, **resident),
                  pl.BlockSpec(tables.shape, lambda i, s: (0, 0, 0)),
                  pl.BlockSpec((None, HEADS, HEAD_DIM, HEAD_DIM), lambda i, s: (i, 0, 0, 0))],
        out_specs=pl.BlockSpec((None, tile, d), prev),
        out_shape=jax.ShapeDtypeStruct((b, l, d), x.dtype),
        scratch_shapes=[pltpu.VMEM((HEADS, HEAD_DIM, HEAD_DIM), _F32),
                        pltpu.VMEM((2, tile, HEADS * HEAD_DIM), _F32)],
        compiler_params=pltpu.CompilerParams(dimension_semantics=("arbitrary", "arbitrary"),
                                             vmem_limit_bytes=VMEM_LIMIT_BYTES),
        name="bwd",
    )(x, ya, qkvg, mod3, mod3, mod3, mod3, norm2, norm_f, w_out, w_gate, w_up, w_down, tables, s_cb)


def _rope_tables(l, k_scale):
    n_freq = HEAD_DIM // 4
    inv = np.float32(ROPE_BASE) ** (-np.arange(n_freq, dtype=np.float32) / np.float32(n_freq))
    rows = np.repeat(np.arange(l // GRID_W, dtype=np.float32), GRID_W)
    cols = np.tile(np.arange(GRID_W, dtype=np.float32), l // GRID_W)
    ang_r, ang_c = rows[:, None] * inv[None, :], cols[:, None] * inv[None, :]
    cos = np.concatenate([np.cos(ang_r), np.cos(ang_c)], axis=-1)
    sin = np.concatenate([np.sin(ang_r), np.sin(ang_c)], axis=-1)
    cos2, sin2 = np.concatenate([cos, cos], axis=-1), np.concatenate([-sin, sin], axis=-1)
    return np.concatenate([cos2, sin2, cos2 * k_scale, sin2 * k_scale], axis=-1).astype(np.float32)


def kernel(x, c, ctx, c_ctx, w_mod, b_mod, norm1, w_in, sg_gain, sg_w, sg_b, ret_logit_f, ret_logit_b,
           w_out, norm2, w_gate, w_up, w_down, norm_f):
    b, l, d = x.shape
    a_width, r_width = GROUPS * LANES, HEADS * HEAD_DIM
    tile = min(TILE, l)
    assert w_mod.shape[0] == 1 and b + 1 <= MOD_ROWS and l % tile == 0 and tile % CHUNK == 0
    assert ctx.shape[1] % CHUNK == 0 and l // tile >= 2
    assert a_width == r_width and d == a_width + r_width
    k_scale = HEAD_DIM ** -0.5

    mod3, w16, tables = _prepare(c, c_ctx, w_mod[0], b_mod[0][None, :], w_in[0], ret_logit_f[0], ret_logit_b[0])
    sg_bias = jnp.broadcast_to(jnp.transpose(sg_b[0])[:, :, None], (CHUNK, GROUPS, LANES)).reshape(CHUNK, a_width)
    ya, qkvg, s_cb, w_out_b, w_gate_b, w_up_b, w_down_b = _forward_pass(
        x, ctx, mod3, norm1, w16, sg_gain, sg_w[0], sg_bias, _rope_tables(l, k_scale),
        tables, (w_out[0], w_gate[0], w_up[0], w_down[0]), tile, k_scale)
    return _backward_pass(x, ya, qkvg, mod3, norm2, norm_f[None, :], w_out_b, w_gate_b, w_up_b, w_down_b,
                          tables, s_cb, tile)
```

```python
import functools

import numpy as np
import jax
import jax.numpy as jnp
from jax import lax
from jax.experimental import pallas as pl
from jax.experimental.pallas import tpu as pltpu

CHUNK = 128
GRID_W = 64
GROUPS = 4
HEADS = 4
HEAD_DIM = 128
RMS_EPS = 1e-6
ROPE_BASE = 10000.0
LANES = 128
MOD_ROWS = 8
TILE = 512
PROJ_CHUNKS = 2
FFN_BLOCK = 2816
VMEM_LIMIT_BYTES = 56 * 1024 * 1024

IN_BLOCKS = 7
U_BLOCK, V_BLOCK, Q_BLOCK, K_BLOCK, VR_BLOCK, GF_BLOCK, GB_BLOCK = range(IN_BLOCKS)

T_DMASK_F, T_DMASK_B, T_XI_F, T_XI_B, T_ZETA_F, T_ZETA_B, T_DECAY_F, T_DECAY_B = range(8)

_BF16 = jnp.bfloat16
_F32 = jnp.float32


def _dot(a, b):
    return jnp.dot(a, b, preferred_element_type=_F32)


def _dot_nt(a, b):
    return lax.dot_general(a, b, (((1,), (1,)), ((), ())), preferred_element_type=_F32)


def _dot_tn(a, b):
    return lax.dot_general(a, b, (((0,), (0,)), ((), ())), preferred_element_type=_F32)


def _silu(x):
    return x * (1.0 / (1.0 + jnp.exp(-x)))


def _gelu_tanh(x):
    c = float(np.sqrt(2.0 / np.pi))
    half = 0.5 * x
    return half + half * jnp.tanh(x * (c + (0.044715 * c) * (x * x)))


def _rms(x):
    return x * lax.rsqrt(jnp.mean(x * x, axis=-1, keepdims=True) + RMS_EPS)


def _head(a, h):
    return a[:, h * HEAD_DIM:(h + 1) * HEAD_DIM]


def _prep_kernel(c_ref, cctx_ref, wmod_ref, bmod_ref, win_ref, lf_ref, lb_ref, mod_ref, w16_ref, tab_ref):
    j = pl.program_id(0)

    n_batch, d = c_ref.shape
    cc = jnp.concatenate([c_ref[...], cctx_ref[...], jnp.zeros((MOD_ROWS - n_batch - 1, d), _F32)], axis=0)
    mod = _dot(_silu(cc), wmod_ref[...]) + bmod_ref[...]
    for r in range(MOD_ROWS):
        mod_ref[r] = mod[r:r + 1, :]

    def reorder_head_dims(t):
        quarter = HEAD_DIM // 4
        lane = lax.broadcasted_iota(jnp.int32, (t.shape[0], HEAD_DIM), 1)
        heads = []
        for h in range(t.shape[1] // HEAD_DIM):
            th = _head(t, h)
            from_right = pltpu.roll(th, HEAD_DIM - quarter, 1)
            from_left = pltpu.roll(th, quarter, 1)
            heads.append(jnp.where((lane >= quarter) & (lane < 2 * quarter), from_right,
                                   jnp.where((lane >= 2 * quarter) & (lane < 3 * quarter), from_left, th)))
        return jnp.concatenate(heads, axis=-1)

    is_qk = (j == Q_BLOCK) | (j == K_BLOCK)

    @pl.when(is_qk)
    def _():
        w16_ref[...] = reorder_head_dims(win_ref[...]).astype(_BF16)

    @pl.when(jnp.logical_not(is_qk))
    def _():
        w16_ref[...] = win_ref[...].astype(_BF16)

    @pl.when(j == 0)
    def _():
        def log_sigmoid(x):
            return -(jnp.maximum(-x, 0.0) + jnp.log1p(jnp.exp(-jnp.abs(x))))

        shape = (CHUNK, HEADS * HEAD_DIM)
        head = lax.broadcasted_iota(jnp.int32, shape, 1) // HEAD_DIM

        def per_head(logit_ref):
            t = jnp.zeros(shape, _F32)
            for h in range(HEADS):
                t = jnp.where(head == h, logit_ref[h], t)
            return t

        lg_f = log_sigmoid(per_head(lf_ref))
        lg_b = log_sigmoid(per_head(lb_ref))
        i = lax.broadcasted_iota(jnp.int32, shape, 0).astype(_F32)
        jj = (lax.broadcasted_iota(jnp.int32, shape, 1) & (HEAD_DIM - 1)).astype(_F32)
        tab_ref[T_DMASK_F] = jnp.where(i >= jj, jnp.exp(lg_f * jnp.maximum(i - jj, 0.0)), 0.0)
        tab_ref[T_DMASK_B] = jnp.where(jj >= i, jnp.exp(lg_b * jnp.maximum(jj - i, 0.0)), 0.0)
        tab_ref[T_XI_F] = jnp.exp(lg_f * (i + 1.0))
        tab_ref[T_XI_B] = jnp.exp(lg_b * (CHUNK - i))
        tab_ref[T_ZETA_F] = jnp.exp(lg_f * (CHUNK - 1.0 - i))
        tab_ref[T_ZETA_B] = jnp.exp(lg_b * i)
        tab_ref[T_DECAY_F] = jnp.exp(lg_f * CHUNK)
        tab_ref[T_DECAY_B] = jnp.exp(lg_b * CHUNK)


def _prepare(c, c_ctx, w_mod, b_mod, w_in, logit_f, logit_b):
    d, n_mod = w_mod.shape
    r_width = HEADS * HEAD_DIM
    n_mod_blocks = n_mod // d
    assert w_in.shape == (d, IN_BLOCKS * r_width) and n_mod_blocks <= IN_BLOCKS
    mod_blk = lambda j: jnp.minimum(j, n_mod_blocks - 1)
    return pl.pallas_call(
        _prep_kernel,
        grid=(IN_BLOCKS,),
        in_specs=[pl.BlockSpec(c.shape, lambda j: (0, 0)),
                  pl.BlockSpec((1, d), lambda j: (0, 0)),
                  pl.BlockSpec((d, d), lambda j: (0, mod_blk(j))),
                  pl.BlockSpec((1, d), lambda j: (0, mod_blk(j))),
                  pl.BlockSpec((d, r_width), lambda j: (0, j)),
                  pl.BlockSpec(memory_space=pltpu.SMEM),
                  pl.BlockSpec(memory_space=pltpu.SMEM)],
        out_specs=[pl.BlockSpec((MOD_ROWS, 1, d), lambda j: (0, 0, mod_blk(j))),
                   pl.BlockSpec((d, r_width), lambda j: (0, j)),
                   pl.BlockSpec((8, CHUNK, r_width), lambda j: (0, 0, 0))],
        out_shape=[jax.ShapeDtypeStruct((MOD_ROWS, 1, n_mod), _F32),
                   jax.ShapeDtypeStruct(w_in.shape, _BF16),
                   jax.ShapeDtypeStruct((8, CHUNK, r_width), _F32)],
        compiler_params=pltpu.CompilerParams(dimension_semantics=("arbitrary",),
                                             vmem_limit_bytes=VMEM_LIMIT_BYTES),
        name="prep",
    )(c, c_ctx[None, :], w_mod, b_mod, w_in, logit_f.astype(_F32), logit_b.astype(_F32))


def _context_states(ctx, shift, scale, gain, w_k, w_v, tab_ref, k_scale):
    hc = ((_rms(ctx) * gain) * (1.0 + scale) + shift).astype(_BF16)
    k = (_dot(hc, w_k) * k_scale).astype(_BF16)
    v = _dot(hc, w_v)
    n_chunks = ctx.shape[0] // CHUNK
    s_fwd, s_bwd = [], []
    for h in range(HEADS):
        s_f = jnp.zeros((HEAD_DIM, HEAD_DIM), _F32)
        s_b = jnp.zeros((HEAD_DIM, HEAD_DIM), _F32)
        for n in range(n_chunks):
            rows = slice(n * CHUNK, (n + 1) * CHUNK)
            upd = _dot_tn(_head(k[rows], h), (_head(v[rows], h) * _head(tab_ref[T_ZETA_F], h)).astype(_BF16))
            s_f = _head(tab_ref[T_DECAY_F], h) * s_f + upd
        for n in reversed(range(n_chunks)):
            rows = slice(n * CHUNK, (n + 1) * CHUNK)
            upd = _dot_tn(_head(k[rows], h), (_head(v[rows], h) * _head(tab_ref[T_ZETA_B], h)).astype(_BF16))
            s_b = _head(tab_ref[T_DECAY_B], h) * s_b + upd
        s_fwd.append(s_f)
        s_bwd.append(s_b)
    return s_fwd, s_bwd


def _retention_tile(load, chunk_order, s_ref, tab_ref, t_dmask, t_decay, emit, out):
    keys = [(n, h) for n in chunk_order for h in range(HEADS)]
    q, k, xq, v, vz, gate = {}, {}, {}, {}, {}, {}
    for n in chunk_order:
        q[n], k[n], xq[n], v[n], vz[n], gate[n] = load(n)
    scores = {(n, h): _dot_nt(_head(q[n], h), _head(k[n], h)) for n, h in keys}
    upd = {(n, h): _dot_tn(_head(k[n], h), _head(vz[n], h)) for n, h in keys}
    yield
    state = [s_ref[h] for h in range(HEADS)]
    lhs, rhs = {}, {}
    for n, h in keys:
        p = (scores[n, h] * _head(tab_ref[t_dmask], h)).astype(_BF16)
        lhs[n, h] = jnp.concatenate([p, _head(xq[n], h)], axis=1)
        rhs[n, h] = jnp.concatenate([_head(v[n], h), state[h].astype(_BF16)], axis=0)
        state[h] = _head(tab_ref[t_decay], h) * state[h] + upd[n, h]
    yield
    o = {key: _dot(lhs[key], rhs[key]) for key in keys}
    yield
    for n in chunk_order:
        emit(n, jnp.concatenate([_silu(_head(gate[n], h)) * _rms(o[n, h]) for h in range(HEADS)], axis=-1))
    for h in range(HEADS):
        out.add(s_ref, h, state[h])


def _interleave(*stages):
    stages = list(stages)
    while stages:
        for g in list(stages):
            try:
                next(g)
            except StopIteration:
                stages.remove(g)


class _DeferredStores:
    def __init__(self):
        self._items = []

    def add(self, ref, idx, value):
        self._items.append((ref, idx, value))

    def commit(self):
        for ref, idx, value in self._items:
            ref[idx] = value


def _skewed_steps(first, middle, last):
    s = pl.program_id(1)
    n_tiles = pl.num_programs(1) - 1

    @pl.when(s == 0)
    def _():
        first(0)

    for parity in range(2):
        @pl.when((s > 0) & (s < n_tiles) & ((s & 1) == parity))
        def _():
            middle(parity)

    for parity in range(2):
        @pl.when((s == n_tiles) & ((s & 1) == parity))
        def _():
            last(1 - parity)


def _fwd_kernel(x_ref, sh_ref, sc_ref, n1_ref, wuv_ref, wqk_ref, wvr_ref, wgf_ref, wgb_ref, gain_ref, sgw_ref, sgb_ref,
                rope_ref, tab_ref, ctx_ref, csh_ref, csc_ref, wo32_ref, wg32_ref, wu32_ref, wd32_ref,
                ya_ref, qkvg_ref, scb_ref, wo16_ref, wg16_ref, wu16_ref, wd16_ref, s_ref, f32_ref, b16_ref, *,
                n_chunks, k_scale):
    a_width = GROUPS * LANES
    r_width = HEADS * HEAD_DIM
    U, GF = 0, a_width
    VG, Q, K, XQ, VB, VZ = (i * r_width for i in range(6))
    rows_all = slice(None)

    def convert(out):
        for src, dst in ((wo32_ref, wo16_ref), (wg32_ref, wg16_ref), (wu32_ref, wu16_ref), (wd32_ref, wd16_ref)):
            out.add(dst, (rows_all, rows_all), src[...].astype(_BF16))

    def project(slot, out):
        n_sub = n_chunks // PROJ_CHUNKS
        for blk in range(n_sub):
            rows = slice(blk * PROJ_CHUNKS * CHUNK, (blk + 1) * PROJ_CHUNKS * CHUNK)

            def per_chunk(t, table):
                return jnp.concatenate([t[n * CHUNK:(n + 1) * CHUNK] * table for n in range(PROJ_CHUNKS)], axis=0)

            x = x_ref[rows, :]
            r = lax.rsqrt(jnp.mean(x * x, axis=-1, keepdims=True) + RMS_EPS)
            hx = ((x * r) * (n1_ref[...] * (1.0 + sc_ref[...])) + sh_ref[...]).astype(_BF16)

            def proj(w_ref, lo, width):
                return _dot(hx, w_ref[:, lo:lo + width])

            yield
            out.add(f32_ref, (slot, rows, slice(U, U + a_width)), _gelu_tanh(proj(wuv_ref, 0, a_width)))
            v = _gelu_tanh(proj(wuv_ref, a_width, a_width))
            vg = jnp.concatenate([(_rms(_head(v, g)) * _head(gain_ref[...], g)).astype(_BF16)
                                  for g in range(GROUPS)], axis=-1)
            out.add(b16_ref, (slot, rows, slice(VG, VG + a_width)), vg)
            yield

            def rope(t, table):
                cos = rope_ref[rows, 2 * table * HEAD_DIM:(2 * table + 1) * HEAD_DIM]
                sin = rope_ref[rows, (2 * table + 1) * HEAD_DIM:(2 * table + 2) * HEAD_DIM]
                return jnp.concatenate(
                    [_head(t, h) * cos + pltpu.roll(_head(t, h), HEAD_DIM // 2, 1) * sin for h in range(HEADS)],
                    axis=-1)

            q = rope(proj(wqk_ref, 0, r_width), 0)
            q16 = q.astype(_BF16)
            k16 = rope(proj(wqk_ref, r_width, r_width), 1).astype(_BF16)
            out.add(b16_ref, (slot, rows, slice(Q, Q + r_width)), q16)
            out.add(b16_ref, (slot, rows, slice(K, K + r_width)), k16)
            out.add(b16_ref, (slot, rows, slice(XQ, XQ + r_width)), per_chunk(q, tab_ref[T_XI_F]).astype(_BF16))
            out.add(qkvg_ref, (rows, slice(0, r_width)), q16)
            out.add(qkvg_ref, (rows, slice(r_width, 2 * r_width)), k16)
            yield
            vr = proj(wvr_ref, 0, r_width)
            v16 = vr.astype(_BF16)
            out.add(b16_ref, (slot, rows, slice(VB, VB + r_width)), v16)
            out.add(b16_ref, (slot, rows, slice(VZ, VZ + r_width)), per_chunk(vr, tab_ref[T_ZETA_F]).astype(_BF16))
            out.add(f32_ref, (slot, rows, slice(GF, GF + r_width)), proj(wgf_ref, 0, r_width))
            out.add(qkvg_ref, (rows, slice(2 * r_width, 3 * r_width)), v16)
            out.add(qkvg_ref, (rows, slice(3 * r_width, 4 * r_width)),
                    proj(wgb_ref, 0, r_width).astype(_BF16))

    def gate_mix(slot, out):
        mixed = []
        for g in range(GROUPS):
            vg = jnp.concatenate([b16_ref[slot, n * CHUNK:(n + 1) * CHUNK, VG + g * LANES:VG + (g + 1) * LANES]
                                  for n in range(n_chunks)], axis=1)
            mixed.append(_dot(sgw_ref[g].astype(_BF16), vg))
        yield
        for g in range(GROUPS):
            cols = slice(g * LANES, (g + 1) * LANES)
            for n in range(n_chunks):
                rows = slice(n * CHUNK, (n + 1) * CHUNK)
                m = mixed[g][:, n * LANES:(n + 1) * LANES] + sgb_ref[:, cols]
                u = f32_ref[slot, rows, U + g * LANES:U + (g + 1) * LANES]
                out.add(ya_ref, (rows, cols), (u * m).astype(_BF16))

    def retain(slot, out):
        def load(n):
            rows = slice(n * CHUNK, (n + 1) * CHUNK)
            return tuple(b16_ref[slot, rows, c:c + r_width] for c in (Q, K, XQ, VB, VZ)) + (
                f32_ref[slot, rows, GF:GF + r_width],)

        def emit(n, y):
            out.add(ya_ref, (slice(n * CHUNK, (n + 1) * CHUNK), slice(a_width, a_width + r_width)), y.astype(_BF16))

        return _retention_tile(load, range(n_chunks), s_ref, tab_ref, T_DMASK_F, T_DECAY_F, emit, out)

    def first(slot):
        s_fwd, s_bwd = _context_states(ctx_ref[...], csh_ref[...], csc_ref[...], n1_ref[...],
                                       wqk_ref[:, r_width:], wvr_ref[...], tab_ref, k_scale)
        for h in range(HEADS):
            s_ref[h] = s_fwd[h]
            scb_ref[h] = s_bwd[h]
        out = _DeferredStores()
        convert(out)
        _interleave(project(slot, out))
        out.commit()

    def middle(slot):
        out = _DeferredStores()
        convert(out)
        _interleave(retain(1 - slot, out), gate_mix(1 - slot, out), project(slot, out))
        out.commit()

    def last(slot):
        out = _DeferredStores()
        convert(out)
        _interleave(retain(slot, out), gate_mix(slot, out))
        out.commit()

    _skewed_steps(first, middle, last)


def _forward_pass(x, ctx, mod3, norm1, w16, sg_gain, sg_w, sg_bias, rope, tables, bwd_weights, tile, k_scale):
    b, l, d = x.shape
    nt = l // tile
    a_width, r_width = GROUPS * LANES, HEADS * HEAD_DIM
    ctx_row = b
    state_spec = pl.BlockSpec((None, HEADS, HEAD_DIM, HEAD_DIM), lambda i, s: (i, 0, 0, 0))
    const2 = lambda i, s: (0, 0)
    cur = lambda i, s: (i, jnp.minimum(s, nt - 1), 0)
    prev = lambda i, s: (i, jnp.maximum(s - 1, 0), 0)
    resident = dict(pipeline_mode=pl.Buffered(1))
    n_blk = max(n for n in (1, 2, 4, 8, 16) if n <= b * (nt + 1))
    blk = lambda i, s: (jnp.minimum(i * (nt + 1) + s, n_blk - 1), 0)
    w_specs = [pl.BlockSpec((w.shape[0] // n_blk, w.shape[1]), blk) for w in bwd_weights]
    return pl.pallas_call(
        functools.partial(_fwd_kernel, n_chunks=tile // CHUNK, k_scale=k_scale),
        grid=(b, nt + 1),
        in_specs=[pl.BlockSpec((None, tile, d), cur),
                  pl.BlockSpec((None, 1, d), lambda i, s: (i, 0, 0)),
                  pl.BlockSpec((None, 1, d), lambda i, s: (i, 0, 1)),
                  pl.BlockSpec((1, d), const2),
                  pl.BlockSpec((d, 2 * r_width), lambda i, s: (0, U_BLOCK // 2), **resident),
                  pl.BlockSpec((d, 2 * r_width), lambda i, s: (0, Q_BLOCK // 2), **resident),
                  pl.BlockSpec((d, r_width), lambda i, s: (0, VR_BLOCK), **resident),
                  pl.BlockSpec((d, r_width), lambda i, s: (0, GF_BLOCK), **resident),
                  pl.BlockSpec((d, r_width), lambda i, s: (0, GB_BLOCK), **resident),
                  pl.BlockSpec((1, a_width), const2),
                  pl.BlockSpec(sg_w.shape, lambda i, s: (0, 0, 0)),
                  pl.BlockSpec(sg_bias.shape, const2),
                  pl.BlockSpec((tile, rope.shape[1]), lambda i, s: (jnp.minimum(s, nt - 1), 0)),
                  pl.BlockSpec(tables.shape, lambda i, s: (0, 0, 0)),
                  pl.BlockSpec((None,) + ctx.shape[1:], lambda i, s: (i, 0, 0)),
                  pl.BlockSpec((None, 1, d), lambda i, s: (ctx_row, 0, 0)),
                  pl.BlockSpec((None, 1, d), lambda i, s: (ctx_row, 0, 1))]
                 + w_specs,
        out_specs=[pl.BlockSpec((None, tile, a_width + r_width), prev),
                   pl.BlockSpec((None, tile, 4 * r_width), cur), state_spec] + w_specs,
        out_shape=[jax.ShapeDtypeStruct((b, l, a_width + r_width), _BF16),
                   jax.ShapeDtypeStruct((b, l, 4 * r_width), _BF16),
                   jax.ShapeDtypeStruct((b, HEADS, HEAD_DIM, HEAD_DIM), _F32)]
                  + [jax.ShapeDtypeStruct(w.shape, _BF16) for w in bwd_weights],
        scratch_shapes=[pltpu.VMEM((HEADS, HEAD_DIM, HEAD_DIM), _F32),
                        pltpu.VMEM((2, tile, a_width + r_width), _F32),
                        pltpu.VMEM((2, tile, 6 * r_width), _BF16)],
        compiler_params=pltpu.CompilerParams(dimension_semantics=("arbitrary", "arbitrary"),
                                             vmem_limit_bytes=VMEM_LIMIT_BYTES),
        name="fwd",
    )(x, mod3, mod3, norm1, w16, w16, w16, w16, w16, sg_gain, sg_w, sg_bias, rope, tables, ctx, mod3, mod3,
      *bwd_weights)


def _bwd_kernel(x_ref, ya_ref, qkvg_ref, g1_ref, sh_ref, sc_ref, g2_ref, n2_ref, nf_ref,
                wout_ref, wg_ref, wu_ref, wd_ref, tab_ref, s0_ref, o_ref, s_ref, yb_ref, *, n_chunks):
    a_width = GROUPS * LANES
    r_width = HEADS * HEAD_DIM

    def retain(slot, out):
        def load(n):
            rows = slice(n * CHUNK, (n + 1) * CHUNK)
            q, k, v, gate = (qkvg_ref[rows, i * r_width:(i + 1) * r_width] for i in range(4))
            return (q, k, (q.astype(_F32) * tab_ref[T_XI_B]).astype(_BF16),
                    v, (v.astype(_F32) * tab_ref[T_ZETA_B]).astype(_BF16), gate.astype(_F32))

        def emit(n, y):
            out.add(yb_ref, (slot, slice(n * CHUNK, (n + 1) * CHUNK), slice(None)), y)

        return _retention_tile(load, list(reversed(range(n_chunks))), s_ref, tab_ref, T_DMASK_B, T_DECAY_B,
                               emit, out)

    def finish(slot, out):
        y_r = ya_ref[:, a_width:].astype(_F32) + yb_ref[slot]
        y = _dot(ya_ref[:, :a_width], wout_ref[:a_width, :]) + _dot(y_r.astype(_BF16), wout_ref[a_width:, :])
        x1 = x_ref[...] + g1_ref[...] * y
        h2 = ((_rms(x1) * n2_ref[...]) * (1.0 + sc_ref[...]) + sh_ref[...]).astype(_BF16)
        acc = None
        d_ff = wg_ref.shape[1]
        for lo in range(0, d_ff, FFN_BLOCK):
            yield
            cols = slice(lo, min(lo + FFN_BLOCK, d_ff))
            act = (_silu(_dot(h2, wg_ref[:, cols])) * _dot(h2, wu_ref[:, cols])).astype(_BF16)
            part = _dot(act, wd_ref[cols, :])
            acc = part if acc is None else acc + part
        x2 = x1 + g2_ref[...] * acc
        out.add(o_ref, (slice(None), slice(None)), _rms(x2) * nf_ref[...])

    def first(slot):
        s_ref[...] = s0_ref[...]
        out = _DeferredStores()
        _interleave(retain(slot, out))
        out.commit()

    def middle(slot):
        out = _DeferredStores()
        _interleave(retain(slot, out), finish(1 - slot, out))
        out.commit()

    def last(slot):
        out = _DeferredStores()
        _interleave(finish(slot, out))
        out.commit()

    _skewed_steps(first, middle, last)


def _backward_pass(x, ya, qkvg, mod3, norm2, norm_f, w_out, w_gate, w_up, w_down, tables, s_cb, tile):
    b, l, d = x.shape
    nt = l // tile
    const2 = lambda i, s: (0, 0)
    cur = lambda i, s: (i, nt - 1 - jnp.minimum(s, nt - 1), 0)
    prev = lambda i, s: (i, nt - 1 - jnp.maximum(s - 1, 0), 0)
    resident = dict(pipeline_mode=pl.Buffered(1))
    mod_spec = lambda j: pl.BlockSpec((None, 1, d), lambda i, s: (i, 0, j))
    return pl.pallas_call(
        functools.partial(_bwd_kernel, n_chunks=tile // CHUNK),
        grid=(b, nt + 1),
        in_specs=[pl.BlockSpec((None, tile, d), prev),
                  pl.BlockSpec((None, tile, ya.shape[-1]), prev),
                  pl.BlockSpec((None, tile, qkvg.shape[-1]), cur),
                  mod_spec(2), mod_spec(3), mod_spec(4), mod_spec(5),
                  pl.BlockSpec((1, d), const2),
                  pl.BlockSpec((1, d), const2),
                  pl.BlockSpec(w_out.shape, const2, **resident),
                  pl.BlockSpec(w_gate.shape, const2, **resident),
                  pl.BlockSpec(w_up.shape, const2, **resident),
                  pl.BlockSpec(w_down.shape, const2, **resident),
                  pl.BlockSpec(tables.shape, lambda i, s: (0, 0, 0)),
                  pl.BlockSpec((None, HEADS, HEAD_DIM, HEAD_DIM), lambda i, s: (i, 0, 0, 0))],
        out_specs=pl.BlockSpec((None, tile, d), prev),
        out_shape=jax.ShapeDtypeStruct((b, l, d), x.dtype),
        scratch_shapes=[pltpu.VMEM((HEADS, HEAD_DIM, HEAD_DIM), _F32),
                        pltpu.VMEM((2, tile, HEADS * HEAD_DIM), _F32)],
        compiler_params=pltpu.CompilerParams(dimension_semantics=("arbitrary", "arbitrary"),
                                             vmem_limit_bytes=VMEM_LIMIT_BYTES),
        name="bwd",
    )(x, ya, qkvg, mod3, mod3, mod3, mod3, norm2, norm_f, w_out, w_gate, w_up, w_down, tables, s_cb)


def _rope_tables(l, k_scale):
    n_freq = HEAD_DIM // 4
    inv = np.float32(ROPE_BASE) ** (-np.arange(n_freq, dtype=np.float32) / np.float32(n_freq))
    rows = np.repeat(np.arange(l // GRID_W, dtype=np.float32), GRID_W)
    cols = np.tile(np.arange(GRID_W, dtype=np.float32), l // GRID_W)
    ang_r, ang_c = rows[:, None] * inv[None, :], cols[:, None] * inv[None, :]
    cos = np.concatenate([np.cos(ang_r), np.cos(ang_c)], axis=-1)
    sin = np.concatenate([np.sin(ang_r), np.sin(ang_c)], axis=-1)
    cos2, sin2 = np.concatenate([cos, cos], axis=-1), np.concatenate([-sin, sin], axis=-1)
    return np.concatenate([cos2, sin2, cos2 * k_scale, sin2 * k_scale], axis=-1).astype(np.float32)


def kernel(x, c, ctx, c_ctx, w_mod, b_mod, norm1, w_in, sg_gain, sg_w, sg_b, ret_logit_f, ret_logit_b,
           w_out, norm2, w_gate, w_up, w_down, norm_f):
    b, l, d = x.shape
    a_width, r_width = GROUPS * LANES, HEADS * HEAD_DIM
    tile = min(TILE, l)
    assert w_mod.shape[0] == 1 and b + 1 <= MOD_ROWS and l % tile == 0 and tile % CHUNK == 0
    assert ctx.shape[1] % CHUNK == 0 and l // tile >= 2
    assert a_width == r_width and d == a_width + r_width
    k_scale = HEAD_DIM ** -0.5

    mod3, w16, tables = _prepare(c, c_ctx, w_mod[0], b_mod[0][None, :], w_in[0], ret_logit_f[0], ret_logit_b[0])
    sg_bias = jnp.broadcast_to(jnp.transpose(sg_b[0])[:, :, None], (CHUNK, GROUPS, LANES)).reshape(CHUNK, a_width)
    ya, qkvg, s_cb, w_out_b, w_gate_b, w_up_b, w_down_b = _forward_pass(
        x, ctx, mod3, norm1, w16, sg_gain, sg_w[0], sg_bias, _rope_tables(l, k_scale),
        tables, (w_out[0], w_gate[0], w_up[0], w_down[0]), tile, k_scale)
    return _backward_pass(x, ya, qkvg, mod3, norm2, norm_f[None, :], w_out_b, w_gate_b, w_up_b, w_down_b,
                          tables, s_cb, tile)
```

```python
import functools

import numpy as np
import jax
import jax.numpy as jnp
from jax import lax
from jax.experimental import pallas as pl
from jax.experimental.pallas import tpu as pltpu

CHUNK = 128
GRID_W = 64
GROUPS = 4
HEADS = 4
HEAD_DIM = 128
RMS_EPS = 1e-6
ROPE_BASE = 10000.0
LANES = 128
MOD_ROWS = 8
TILE = 512
PROJ_CHUNKS = 2
FWD_RETAIN_GROUPS = 4
BWD_RETAIN_GROUPS = 2
FFN_BLOCK = 1024
VMEM_LIMIT_BYTES = 56 * 1024 * 1024

IN_BLOCKS = 7
U_BLOCK, V_BLOCK, Q_BLOCK, K_BLOCK, VR_BLOCK, GF_BLOCK, GB_BLOCK = range(IN_BLOCKS)

T_DMASK_F, T_DMASK_B, T_XI_F, T_XI_B, T_ZETA_F, T_ZETA_B, T_DECAY_F, T_DECAY_B = range(8)

_BF16 = jnp.bfloat16
_F32 = jnp.float32


def _dot(a, b):
    return jnp.dot(a, b, preferred_element_type=_F32)


def _dot_nt(a, b):
    return lax.dot_general(a, b, (((1,), (1,)), ((), ())), preferred_element_type=_F32)


def _dot_tn(a, b):
    return lax.dot_general(a, b, (((0,), (0,)), ((), ())), preferred_element_type=_F32)


def _silu(x):
    return x * (1.0 / (1.0 + jnp.exp(-x)))


def _gelu_tanh(x):
    c = float(np.sqrt(2.0 / np.pi))
    half = 0.5 * x
    return half + half * jnp.tanh(x * (c + (0.044715 * c) * (x * x)))


def _rms(x):
    return x * lax.rsqrt(jnp.mean(x * x, axis=-1, keepdims=True) + RMS_EPS)


def _head(a, h):
    return a[:, h * HEAD_DIM:(h + 1) * HEAD_DIM]


def _prep_kernel(c_ref, cctx_ref, wmod_ref, bmod_ref, win_ref, lf_ref, lb_ref, mod_ref, w16_ref, tab_ref):
    j = pl.program_id(0)

    n_batch, d = c_ref.shape
    cc = jnp.concatenate([c_ref[...], cctx_ref[...], jnp.zeros((MOD_ROWS - n_batch - 1, d), _F32)], axis=0)
    mod = _dot(_silu(cc), wmod_ref[...]) + bmod_ref[...]
    for r in range(MOD_ROWS):
        mod_ref[r] = mod[r:r + 1, :]

    def reorder_head_dims(t):
        quarter = HEAD_DIM // 4
        lane = lax.broadcasted_iota(jnp.int32, (t.shape[0], HEAD_DIM), 1)
        heads = []
        for h in range(t.shape[1] // HEAD_DIM):
            th = _head(t, h)
            from_right = pltpu.roll(th, HEAD_DIM - quarter, 1)
            from_left = pltpu.roll(th, quarter, 1)
            heads.append(jnp.where((lane >= quarter) & (lane < 2 * quarter), from_right,
                                   jnp.where((lane >= 2 * quarter) & (lane < 3 * quarter), from_left, th)))
        return jnp.concatenate(heads, axis=-1)

    is_qk = (j == Q_BLOCK) | (j == K_BLOCK)

    @pl.when(is_qk)
    def _():
        w16_ref[...] = reorder_head_dims(win_ref[...]).astype(_BF16)

    @pl.when(jnp.logical_not(is_qk))
    def _():
        w16_ref[...] = win_ref[...].astype(_BF16)

    @pl.when(j == 0)
    def _():
        def log_sigmoid(x):
            return -(jnp.maximum(-x, 0.0) + jnp.log1p(jnp.exp(-jnp.abs(x))))

        shape = (CHUNK, HEADS * HEAD_DIM)
        head = lax.broadcasted_iota(jnp.int32, shape, 1) // HEAD_DIM

        def per_head(logit_ref):
            t = jnp.zeros(shape, _F32)
            for h in range(HEADS):
                t = jnp.where(head == h, logit_ref[h], t)
            return t

        lg_f = log_sigmoid(per_head(lf_ref))
        lg_b = log_sigmoid(per_head(lb_ref))
        i = lax.broadcasted_iota(jnp.int32, shape, 0).astype(_F32)
        jj = (lax.broadcasted_iota(jnp.int32, shape, 1) & (HEAD_DIM - 1)).astype(_F32)
        tab_ref[T_DMASK_F] = jnp.where(i >= jj, jnp.exp(lg_f * jnp.maximum(i - jj, 0.0)), 0.0)
        tab_ref[T_DMASK_B] = jnp.where(jj >= i, jnp.exp(lg_b * jnp.maximum(jj - i, 0.0)), 0.0)
        tab_ref[T_XI_F] = jnp.exp(lg_f * (i + 1.0))
        tab_ref[T_XI_B] = jnp.exp(lg_b * (CHUNK - i))
        tab_ref[T_ZETA_F] = jnp.exp(lg_f * (CHUNK - 1.0 - i))
        tab_ref[T_ZETA_B] = jnp.exp(lg_b * i)
        tab_ref[T_DECAY_F] = jnp.exp(lg_f * CHUNK)
        tab_ref[T_DECAY_B] = jnp.exp(lg_b * CHUNK)


def _prepare(c, c_ctx, w_mod, b_mod, w_in, logit_f, logit_b):
    d, n_mod = w_mod.shape
    r_width = HEADS * HEAD_DIM
    n_mod_blocks = n_mod // d
    assert w_in.shape == (d, IN_BLOCKS * r_width) and n_mod_blocks <= IN_BLOCKS
    mod_blk = lambda j: jnp.minimum(j, n_mod_blocks - 1)
    return pl.pallas_call(
        _prep_kernel,
        grid=(IN_BLOCKS,),
        in_specs=[pl.BlockSpec(c.shape, lambda j: (0, 0)),
                  pl.BlockSpec((1, d), lambda j: (0, 0)),
                  pl.BlockSpec((d, d), lambda j: (0, mod_blk(j))),
                  pl.BlockSpec((1, d), lambda j: (0, mod_blk(j))),
                  pl.BlockSpec((d, r_width), lambda j: (0, j)),
                  pl.BlockSpec(memory_space=pltpu.SMEM),
                  pl.BlockSpec(memory_space=pltpu.SMEM)],
        out_specs=[pl.BlockSpec((MOD_ROWS, 1, d), lambda j: (0, 0, mod_blk(j))),
                   pl.BlockSpec((d, r_width), lambda j: (0, j)),
                   pl.BlockSpec((8, CHUNK, r_width), lambda j: (0, 0, 0))],
        out_shape=[jax.ShapeDtypeStruct((MOD_ROWS, 1, n_mod), _F32),
                   jax.ShapeDtypeStruct(w_in.shape, _BF16),
                   jax.ShapeDtypeStruct((8, CHUNK, r_width), _F32)],
        compiler_params=pltpu.CompilerParams(dimension_semantics=("arbitrary",),
                                             vmem_limit_bytes=VMEM_LIMIT_BYTES),
        name="prep",
    )(c, c_ctx[None, :], w_mod, b_mod, w_in, logit_f.astype(_F32), logit_b.astype(_F32))


def _context_states(ctx, shift, scale, gain, w_k, w_v, tab_ref, k_scale):
    hc = ((_rms(ctx) * gain) * (1.0 + scale) + shift).astype(_BF16)
    k = (_dot(hc, w_k) * k_scale).astype(_BF16)
    v = _dot(hc, w_v)
    n_chunks = ctx.shape[0] // CHUNK
    s_fwd, s_bwd = [], []
    for h in range(HEADS):
        s_f = jnp.zeros((HEAD_DIM, HEAD_DIM), _F32)
        s_b = jnp.zeros((HEAD_DIM, HEAD_DIM), _F32)
        for n in range(n_chunks):
            rows = slice(n * CHUNK, (n + 1) * CHUNK)
            upd = _dot_tn(_head(k[rows], h), (_head(v[rows], h) * _head(tab_ref[T_ZETA_F], h)).astype(_BF16))
            s_f = _head(tab_ref[T_DECAY_F], h) * s_f + upd
        for n in reversed(range(n_chunks)):
            rows = slice(n * CHUNK, (n + 1) * CHUNK)
            upd = _dot_tn(_head(k[rows], h), (_head(v[rows], h) * _head(tab_ref[T_ZETA_B], h)).astype(_BF16))
            s_b = _head(tab_ref[T_DECAY_B], h) * s_b + upd
        s_fwd.append(s_f)
        s_bwd.append(s_b)
    return s_fwd, s_bwd


def _retention_tile(load, chunk_order, state, tab_ref, t_dmask, t_decay, emit):
    keys = [(n, h) for n in chunk_order for h in range(HEADS)]
    q, k, xq, v, vz, gate = {}, {}, {}, {}, {}, {}
    for n in chunk_order:
        q[n], k[n], xq[n], v[n], vz[n], gate[n] = load(n)
    scores = {(n, h): _dot_nt(_head(q[n], h), _head(k[n], h)) for n, h in keys}
    upd = {(n, h): _dot_tn(_head(k[n], h), _head(vz[n], h)) for n, h in keys}
    yield
    lhs, rhs = {}, {}
    for n, h in keys:
        p = (scores[n, h] * _head(tab_ref[t_dmask], h)).astype(_BF16)
        lhs[n, h] = jnp.concatenate([p, _head(xq[n], h)], axis=1)
        rhs[n, h] = jnp.concatenate([_head(v[n], h), state[h].astype(_BF16)], axis=0)
        state[h] = _head(tab_ref[t_decay], h) * state[h] + upd[n, h]
    yield
    o = {key: _dot(lhs[key], rhs[key]) for key in keys}
    yield
    for n in chunk_order:
        emit(n, jnp.concatenate([_silu(_head(gate[n], h)) * _rms(o[n, h]) for h in range(HEADS)], axis=-1))


def _retention_stages(load, chunk_order, n_groups, s_ref, tab_ref, t_dmask, t_decay, emit, out):
    state = [None] * HEADS
    order = list(chunk_order)
    per = len(order) // n_groups
    groups = [order[i * per:(i + 1) * per] for i in range(n_groups)]

    def group_stage(i):
        if i:
            for _ in range(i):
                yield
        inner = _retention_tile(load, groups[i], state, tab_ref, t_dmask, t_decay, emit)
        next(inner)
        yield
        if i == 0:
            for h in range(HEADS):
                state[h] = s_ref[h]
        next(inner)
        yield
        for _ in inner:
            yield
        if i == n_groups - 1:
            for h in range(HEADS):
                out.add(s_ref, h, state[h])

    return [group_stage(i) for i in range(n_groups)]


def _interleave(*stages):
    stages = list(stages)
    while stages:
        for g in list(stages):
            try:
                next(g)
            except StopIteration:
                stages.remove(g)


class _DeferredStores:
    def __init__(self):
        self._items = []

    def add(self, ref, idx, value):
        self._items.append((ref, idx, value))

    def commit(self):
        for ref, idx, value in self._items:
            ref[idx] = value


def _skewed_steps(first, middle, last):
    s = pl.program_id(1)
    n_tiles = pl.num_programs(1) - 1

    @pl.when(s == 0)
    def _():
        first(0)

    for parity in range(2):
        @pl.when((s > 0) & (s < n_tiles) & ((s & 1) == parity))
        def _():
            middle(parity)

    for parity in range(2):
        @pl.when((s == n_tiles) & ((s & 1) == parity))
        def _():
            last(1 - parity)


def _fwd_kernel(x_ref, sh_ref, sc_ref, n1_ref, wuv_ref, wqk_ref, wvr_ref, wgf_ref, wgb_ref, gain_ref, sgw_ref, sgb_ref,
                rope_ref, tab_ref, ctx_ref, csh_ref, csc_ref, wo32_ref, wg32_ref, wu32_ref, wd32_ref,
                ya_ref, qkvg_ref, scb_ref, wo16_ref, wg16_ref, wu16_ref, wd16_ref, s_ref, f32_ref, b16_ref, *,
                n_chunks, k_scale):
    a_width = GROUPS * LANES
    r_width = HEADS * HEAD_DIM
    U, GF = 0, a_width
    VG, Q, K, XQ, VB, VZ = (i * r_width for i in range(6))
    rows_all = slice(None)

    def convert(out):
        for src, dst in ((wo32_ref, wo16_ref), (wg32_ref, wg16_ref), (wu32_ref, wu16_ref), (wd32_ref, wd16_ref)):
            out.add(dst, (rows_all, rows_all), src[...].astype(_BF16))

    def project(slot, out):
        n_sub = n_chunks // PROJ_CHUNKS
        for blk in range(n_sub):
            rows = slice(blk * PROJ_CHUNKS * CHUNK, (blk + 1) * PROJ_CHUNKS * CHUNK)

            def per_chunk(t, table):
                return jnp.concatenate([t[n * CHUNK:(n + 1) * CHUNK] * table for n in range(PROJ_CHUNKS)], axis=0)

            x = x_ref[rows, :]
            r = lax.rsqrt(jnp.mean(x * x, axis=-1, keepdims=True) + RMS_EPS)
            hx = ((x * r) * (n1_ref[...] * (1.0 + sc_ref[...])) + sh_ref[...]).astype(_BF16)

            def proj(w_ref, lo, width):
                return _dot(hx, w_ref[:, lo:lo + width])

            yield
            out.add(f32_ref, (slot, rows, slice(U, U + a_width)), _gelu_tanh(proj(wuv_ref, 0, a_width)))
            v = _gelu_tanh(proj(wuv_ref, a_width, a_width))
            vg = jnp.concatenate([(_rms(_head(v, g)) * _head(gain_ref[...], g)).astype(_BF16)
                                  for g in range(GROUPS)], axis=-1)
            out.add(b16_ref, (slot, rows, slice(VG, VG + a_width)), vg)
            yield

            def rope(t, table):
                cos = rope_ref[rows, 2 * table * HEAD_DIM:(2 * table + 1) * HEAD_DIM]
                sin = rope_ref[rows, (2 * table + 1) * HEAD_DIM:(2 * table + 2) * HEAD_DIM]
                return jnp.concatenate(
                    [_head(t, h) * cos + pltpu.roll(_head(t, h), HEAD_DIM // 2, 1) * sin for h in range(HEADS)],
                    axis=-1)

            q = rope(proj(wqk_ref, 0, r_width), 0)
            q16 = q.astype(_BF16)
            k16 = rope(proj(wqk_ref, r_width, r_width), 1).astype(_BF16)
            out.add(b16_ref, (slot, rows, slice(Q, Q + r_width)), q16)
            out.add(b16_ref, (slot, rows, slice(K, K + r_width)), k16)
            out.add(b16_ref, (slot, rows, slice(XQ, XQ + r_width)), per_chunk(q, tab_ref[T_XI_F]).astype(_BF16))
            out.add(qkvg_ref, (rows, slice(0, r_width)), q16)
            out.add(qkvg_ref, (rows, slice(r_width, 2 * r_width)), k16)
            yield
            vr = proj(wvr_ref, 0, r_width)
            v16 = vr.astype(_BF16)
            out.add(b16_ref, (slot, rows, slice(VB, VB + r_width)), v16)
            out.add(b16_ref, (slot, rows, slice(VZ, VZ + r_width)), per_chunk(vr, tab_ref[T_ZETA_F]).astype(_BF16))
            out.add(f32_ref, (slot, rows, slice(GF, GF + r_width)), proj(wgf_ref, 0, r_width))
            out.add(qkvg_ref, (rows, slice(2 * r_width, 3 * r_width)), v16)
            out.add(qkvg_ref, (rows, slice(3 * r_width, 4 * r_width)),
                    proj(wgb_ref, 0, r_width).astype(_BF16))

    def gate_mix(slot, out):
        mixed = []
        for g in range(GROUPS):
            vg = jnp.concatenate([b16_ref[slot, n * CHUNK:(n + 1) * CHUNK, VG + g * LANES:VG + (g + 1) * LANES]
                                  for n in range(n_chunks)], axis=1)
            mixed.append(_dot(sgw_ref[g].astype(_BF16), vg))
        yield
        for g in range(GROUPS):
            cols = slice(g * LANES, (g + 1) * LANES)
            for n in range(n_chunks):
                rows = slice(n * CHUNK, (n + 1) * CHUNK)
                m = mixed[g][:, n * LANES:(n + 1) * LANES] + sgb_ref[:, cols]
                u = f32_ref[slot, rows, U + g * LANES:U + (g + 1) * LANES]
                out.add(ya_ref, (rows, cols), (u * m).astype(_BF16))

    def retain(slot, out):
        def load(n):
            rows = slice(n * CHUNK, (n + 1) * CHUNK)
            return tuple(b16_ref[slot, rows, c:c + r_width] for c in (Q, K, XQ, VB, VZ)) + (
                f32_ref[slot, rows, GF:GF + r_width],)

        def emit(n, y):
            out.add(ya_ref, (slice(n * CHUNK, (n + 1) * CHUNK), slice(a_width, a_width + r_width)), y.astype(_BF16))

        return _retention_stages(load, range(n_chunks), FWD_RETAIN_GROUPS, s_ref, tab_ref, T_DMASK_F, T_DECAY_F,
                                 emit, out)

    def first(slot):
        s_fwd, s_bwd = _context_states(ctx_ref[...], csh_ref[...], csc_ref[...], n1_ref[...],
                                       wqk_ref[:, r_width:], wvr_ref[...], tab_ref, k_scale)
        for h in range(HEADS):
            s_ref[h] = s_fwd[h]
            scb_ref[h] = s_bwd[h]
        out = _DeferredStores()
        convert(out)
        _interleave(project(slot, out))
        out.commit()

    def middle(slot):
        out = _DeferredStores()
        convert(out)
        _interleave(*retain(1 - slot, out), gate_mix(1 - slot, out), project(slot, out))
        out.commit()

    def last(slot):
        out = _DeferredStores()
        convert(out)
        _interleave(*retain(slot, out), gate_mix(slot, out))
        out.commit()

    _skewed_steps(first, middle, last)


def _forward_pass(x, ctx, mod3, norm1, w16, sg_gain, sg_w, sg_bias, rope, tables, bwd_weights, tile, k_scale):
    b, l, d = x.shape
    nt = l // tile
    a_width, r_width = GROUPS * LANES, HEADS * HEAD_DIM
    ctx_row = b
    state_spec = pl.BlockSpec((None, HEADS, HEAD_DIM, HEAD_DIM), lambda i, s: (i, 0, 0, 0))
    const2 = lambda i, s: (0, 0)
    cur = lambda i, s: (i, jnp.minimum(s, nt - 1), 0)
    prev = lambda i, s: (i, jnp.maximum(s - 1, 0), 0)
    resident = dict(pipeline_mode=pl.Buffered(1))
    n_blk = max(n for n in (1, 2, 4, 8, 16) if n <= b * (nt + 1))
    blk = lambda i, s: (jnp.minimum(i * (nt + 1) + s, n_blk - 1), 0)
    w_specs = [pl.BlockSpec((w.shape[0] // n_blk, w.shape[1]), blk) for w in bwd_weights]
    return pl.pallas_call(
        functools.partial(_fwd_kernel, n_chunks=tile // CHUNK, k_scale=k_scale),
        grid=(b, nt + 1),
        in_specs=[pl.BlockSpec((None, tile, d), cur),
                  pl.BlockSpec((None, 1, d), lambda i, s: (i, 0, 0)),
                  pl.BlockSpec((None, 1, d), lambda i, s: (i, 0, 1)),
                  pl.BlockSpec((1, d), const2),
                  pl.BlockSpec((d, 2 * r_width), lambda i, s: (0, U_BLOCK // 2), **resident),
                  pl.BlockSpec((d, 2 * r_width), lambda i, s: (0, Q_BLOCK // 2), **resident),
                  pl.BlockSpec((d, r_width), lambda i, s: (0, VR_BLOCK), **resident),
                  pl.BlockSpec((d, r_width), lambda i, s: (0, GF_BLOCK), **resident),
                  pl.BlockSpec((d, r_width), lambda i, s: (0, GB_BLOCK), **resident),
                  pl.BlockSpec((1, a_width), const2),
                  pl.BlockSpec(sg_w.shape, lambda i, s: (0, 0, 0)),
                  pl.BlockSpec(sg_bias.shape, const2),
                  pl.BlockSpec((tile, rope.shape[1]), lambda i, s: (jnp.minimum(s, nt - 1), 0)),
                  pl.BlockSpec(tables.shape, lambda i, s: (0, 0, 0)),
                  pl.BlockSpec((None,) + ctx.shape[1:], lambda i, s: (i, 0, 0)),
                  pl.BlockSpec((None, 1, d), lambda i, s: (ctx_row, 0, 0)),
                  pl.BlockSpec((None, 1, d), lambda i, s: (ctx_row, 0, 1))]
                 + w_specs,
        out_specs=[pl.BlockSpec((None, tile, a_width + r_width), prev),
                   pl.BlockSpec((None, tile, 4 * r_width), cur), state_spec] + w_specs,
        out_shape=[jax.ShapeDtypeStruct((b, l, a_width + r_width), _BF16),
                   jax.ShapeDtypeStruct((b, l, 4 * r_width), _BF16),
                   jax.ShapeDtypeStruct((b, HEADS, HEAD_DIM, HEAD_DIM), _F32)]
                  + [jax.ShapeDtypeStruct(w.shape, _BF16) for w in bwd_weights],
        scratch_shapes=[pltpu.VMEM((HEADS, HEAD_DIM, HEAD_DIM), _F32),
                        pltpu.VMEM((2, tile, a_width + r_width), _F32),
                        pltpu.VMEM((2, tile, 6 * r_width), _BF16)],
        compiler_params=pltpu.CompilerParams(dimension_semantics=("arbitrary", "arbitrary"),
                                             vmem_limit_bytes=VMEM_LIMIT_BYTES),
        name="fwd",
    )(x, mod3, mod3, norm1, w16, w16, w16, w16, w16, sg_gain, sg_w, sg_bias, rope, tables, ctx, mod3, mod3,
      *bwd_weights)


def _bwd_kernel(x_ref, ya_ref, qkvg_ref, g1_ref, sh_ref, sc_ref, g2_ref, n2_ref, nf_ref,
                wout_ref, wg_ref, wu_ref, wd_ref, tab_ref, s0_ref, o_ref, s_ref, yb_ref, *, n_chunks):
    a_width = GROUPS * LANES
    r_width = HEADS * HEAD_DIM

    def retain(slot, out):
        def load(n):
            rows = slice(n * CHUNK, (n + 1) * CHUNK)
            q, k, v, gate = (qkvg_ref[rows, i * r_width:(i + 1) * r_width] for i in range(4))
            return (q, k, (q.astype(_F32) * tab_ref[T_XI_B]).astype(_BF16),
                    v, (v.astype(_F32) * tab_ref[T_ZETA_B]).astype(_BF16), gate.astype(_F32))

        def emit(n, y):
            out.add(yb_ref, (slot, slice(n * CHUNK, (n + 1) * CHUNK), slice(None)), y)

        return _retention_stages(load, list(reversed(range(n_chunks))), BWD_RETAIN_GROUPS, s_ref, tab_ref,
                                 T_DMASK_B, T_DECAY_B, emit, out)

    def finish(slot, out):
        y_r = ya_ref[:, a_width:].astype(_F32) + yb_ref[slot]
        y = _dot(ya_ref[:, :a_width], wout_ref[:a_width, :]) + _dot(y_r.astype(_BF16), wout_ref[a_width:, :])
        x1 = x_ref[...] + g1_ref[...] * y
        h2 = ((_rms(x1) * n2_ref[...]) * (1.0 + sc_ref[...]) + sh_ref[...]).astype(_BF16)
        acc = None
        d_ff = wg_ref.shape[1]
        for lo in range(0, d_ff, FFN_BLOCK):
            yield
            cols = slice(lo, min(lo + FFN_BLOCK, d_ff))
            act = (_silu(_dot(h2, wg_ref[:, cols])) * _dot(h2, wu_ref[:, cols])).astype(_BF16)
            part = _dot(act, wd_ref[cols, :])
            acc = part if acc is None else acc + part
        x2 = x1 + g2_ref[...] * acc
        out.add(o_ref, (slice(None), slice(None)), _rms(x2) * nf_ref[...])

    def first(slot):
        s_ref[...] = s0_ref[...]
        out = _DeferredStores()
        _interleave(*retain(slot, out))
        out.commit()

    def middle(slot):
        out = _DeferredStores()
        _interleave(*retain(slot, out), finish(1 - slot, out))
        out.commit()

    def last(slot):
        out = _DeferredStores()
        _interleave(finish(slot, out))
        out.commit()

    _skewed_steps(first, middle, last)


def _backward_pass(x, ya, qkvg, mod3, norm2, norm_f, w_out, w_gate, w_up, w_down, tables, s_cb, tile):
    b, l, d = x.shape
    nt = l // tile
    const2 = lambda i, s: (0, 0)
    cur = lambda i, s: (i, nt - 1 - jnp.minimum(s, nt - 1), 0)
    prev = lambda i, s: (i, nt - 1 - jnp.maximum(s - 1, 0), 0)
    resident = dict(pipeline_mode=pl.Buffered(1))
    mod_spec = lambda j: pl.BlockSpec((None, 1, d), lambda i, s: (i, 0, j))
    return pl.pallas_call(
        functools.partial(_bwd_kernel, n_chunks=tile // CHUNK),
        grid=(b, nt + 1),
        in_specs=[pl.BlockSpec((None, tile, d), prev),
                  pl.BlockSpec((None, tile, ya.shape[-1]), prev),
                  pl.BlockSpec((None, tile, qkvg.shape[-1]), cur),
                  mod_spec(2), mod_spec(3), mod_spec(4), mod_spec(5),
                  pl.BlockSpec((1, d), const2),
                  pl.BlockSpec((1, d), const2),
                  pl.BlockSpec(w_out.shape, const2, **resident),
                  pl.BlockSpec(w_gate.shape, const2, **resident),
                  pl.BlockSpec(w_up.shape, const2, **resident),
                  pl.BlockSpec(w_down.shape, const2, **resident),
                  pl.BlockSpec(tables.shape, lambda i, s: (0, 0, 0)),
                  pl.BlockSpec((None, HEADS, HEAD_DIM, HEAD_DIM), lambda i, s: (i, 0, 0, 0))],
        out_specs=pl.BlockSpec((None, tile, d), prev),
        out_shape=jax.ShapeDtypeStruct((b, l, d), x.dtype),
        scratch_shapes=[pltpu.VMEM((HEADS, HEAD_DIM, HEAD_DIM), _F32),
                        pltpu.VMEM((2, tile, HEADS * HEAD_DIM), _F32)],
        compiler_params=pltpu.CompilerParams(dimension_semantics=("arbitrary", "arbitrary"),
                                             vmem_limit_bytes=VMEM_LIMIT_BYTES),
        name="bwd",
    )(x, ya, qkvg, mod3, mod3, mod3, mod3, norm2, norm_f, w_out, w_gate, w_up, w_down, tables, s_cb)


def _rope_tables(l, k_scale):
    n_freq = HEAD_DIM // 4
    inv = np.float32(ROPE_BASE) ** (-np.arange(n_freq, dtype=np.float32) / np.float32(n_freq))
    rows = np.repeat(np.arange(l // GRID_W, dtype=np.float32), GRID_W)
    cols = np.tile(np.arange(GRID_W, dtype=np.float32), l // GRID_W)
    ang_r, ang_c = rows[:, None] * inv[None, :], cols[:, None] * inv[None, :]
    cos = np.concatenate([np.cos(ang_r), np.cos(ang_c)], axis=-1)
    sin = np.concatenate([np.sin(ang_r), np.sin(ang_c)], axis=-1)
    cos2, sin2 = np.concatenate([cos, cos], axis=-1), np.concatenate([-sin, sin], axis=-1)
    return np.concatenate([cos2, sin2, cos2 * k_scale, sin2 * k_scale], axis=-1).astype(np.float32)


def kernel(x, c, ctx, c_ctx, w_mod, b_mod, norm1, w_in, sg_gain, sg_w, sg_b, ret_logit_f, ret_logit_b,
           w_out, norm2, w_gate, w_up, w_down, norm_f):
    b, l, d = x.shape
    a_width, r_width = GROUPS * LANES, HEADS * HEAD_DIM
    tile = min(TILE, l)
    assert w_mod.shape[0] == 1 and b + 1 <= MOD_ROWS and l % tile == 0 and tile % CHUNK == 0
    assert ctx.shape[1] % CHUNK == 0 and l // tile >= 2
    assert a_width == r_width and d == a_width + r_width
    k_scale = HEAD_DIM ** -0.5

    mod3, w16, tables = _prepare(c, c_ctx, w_mod[0], b_mod[0][None, :], w_in[0], ret_logit_f[0], ret_logit_b[0])
    sg_bias = jnp.broadcast_to(jnp.transpose(sg_b[0])[:, :, None], (CHUNK, GROUPS, LANES)).reshape(CHUNK, a_width)
    ya, qkvg, s_cb, w_out_b, w_gate_b, w_up_b, w_down_b = _forward_pass(
        x, ctx, mod3, norm1, w16, sg_gain, sg_w[0], sg_bias, _rope_tables(l, k_scale),
        tables, (w_out[0], w_gate[0], w_up[0], w_down[0]), tile, k_scale)
    return _backward_pass(x, ya, qkvg, mod3, norm2, norm_f[None, :], w_out_b, w_gate_b, w_up_b, w_down_b,
                          tables, s_cb, tile)
```

```python
import functools

import numpy as np
import jax
import jax.numpy as jnp
from jax import lax
from jax.experimental import pallas as pl
from jax.experimental.pallas import tpu as pltpu

CHUNK = 128
GRID_W = 64
GROUPS = 4
HEADS = 4
HEAD_DIM = 128
RMS_EPS = 1e-6
ROPE_BASE = 10000.0
LANES = 128
MOD_ROWS = 8
TILE = 512
PROJ_CHUNKS = 2
BWD_RETAIN_GROUPS = 2
FFN_BLOCK = 1024
VMEM_LIMIT_BYTES = 56 * 1024 * 1024

IN_BLOCKS = 7
U_BLOCK, V_BLOCK, Q_BLOCK, K_BLOCK, VR_BLOCK, GF_BLOCK, GB_BLOCK = range(IN_BLOCKS)

T_DMASK_F, T_DMASK_B, T_XI_F, T_XI_B, T_ZETA_F, T_ZETA_B, T_DECAY_F, T_DECAY_B = range(8)

_BF16 = jnp.bfloat16
_F32 = jnp.float32


def _dot(a, b):
    return jnp.dot(a, b, preferred_element_type=_F32)


def _dot_nt(a, b):
    return lax.dot_general(a, b, (((1,), (1,)), ((), ())), preferred_element_type=_F32)


def _dot_tn(a, b):
    return lax.dot_general(a, b, (((0,), (0,)), ((), ())), preferred_element_type=_F32)


def _silu(x):
    return x * (1.0 / (1.0 + jnp.exp(-x)))


def _gelu_tanh(x):
    c = float(np.sqrt(2.0 / np.pi))
    half = 0.5 * x
    return half + half * jnp.tanh(x * (c + (0.044715 * c) * (x * x)))


def _rms(x):
    return x * lax.rsqrt(jnp.mean(x * x, axis=-1, keepdims=True) + RMS_EPS)


def _head(a, h):
    return a[:, h * HEAD_DIM:(h + 1) * HEAD_DIM]


def _prep_kernel(c_ref, cctx_ref, wmod_ref, bmod_ref, win_ref, lf_ref, lb_ref, mod_ref, w16_ref, tab_ref):
    j = pl.program_id(0)

    n_batch, d = c_ref.shape
    cc = jnp.concatenate([c_ref[...], cctx_ref[...], jnp.zeros((MOD_ROWS - n_batch - 1, d), _F32)], axis=0)
    mod = _dot(_silu(cc), wmod_ref[...]) + bmod_ref[...]
    for r in range(MOD_ROWS):
        mod_ref[r] = mod[r:r + 1, :]

    def reorder_head_dims(t):
        quarter = HEAD_DIM // 4
        lane = lax.broadcasted_iota(jnp.int32, (t.shape[0], HEAD_DIM), 1)
        heads = []
        for h in range(t.shape[1] // HEAD_DIM):
            th = _head(t, h)
            from_right = pltpu.roll(th, HEAD_DIM - quarter, 1)
            from_left = pltpu.roll(th, quarter, 1)
            heads.append(jnp.where((lane >= quarter) & (lane < 2 * quarter), from_right,
                                   jnp.where((lane >= 2 * quarter) & (lane < 3 * quarter), from_left, th)))
        return jnp.concatenate(heads, axis=-1)

    is_qk = (j == Q_BLOCK) | (j == K_BLOCK)

    @pl.when(is_qk)
    def _():
        w16_ref[...] = reorder_head_dims(win_ref[...]).astype(_BF16)

    @pl.when(jnp.logical_not(is_qk))
    def _():
        w16_ref[...] = win_ref[...].astype(_BF16)

    @pl.when(j == 0)
    def _():
        def log_sigmoid(x):
            return -(jnp.maximum(-x, 0.0) + jnp.log1p(jnp.exp(-jnp.abs(x))))

        shape = (CHUNK, HEADS * HEAD_DIM)
        head = lax.broadcasted_iota(jnp.int32, shape, 1) // HEAD_DIM

        def per_head(logit_ref):
            t = jnp.zeros(shape, _F32)
            for h in range(HEADS):
                t = jnp.where(head == h, logit_ref[h], t)
            return t

        lg_f = log_sigmoid(per_head(lf_ref))
        lg_b = log_sigmoid(per_head(lb_ref))
        i = lax.broadcasted_iota(jnp.int32, shape, 0).astype(_F32)
        jj = (lax.broadcasted_iota(jnp.int32, shape, 1) & (HEAD_DIM - 1)).astype(_F32)
        tab_ref[T_DMASK_F] = jnp.where(i >= jj, jnp.exp(lg_f * jnp.maximum(i - jj, 0.0)), 0.0)
        tab_ref[T_DMASK_B] = jnp.where(jj >= i, jnp.exp(lg_b * jnp.maximum(jj - i, 0.0)), 0.0)
        tab_ref[T_XI_F] = jnp.exp(lg_f * (i + 1.0))
        tab_ref[T_XI_B] = jnp.exp(lg_b * (CHUNK - i))
        tab_ref[T_ZETA_F] = jnp.exp(lg_f * (CHUNK - 1.0 - i))
        tab_ref[T_ZETA_B] = jnp.exp(lg_b * i)
        tab_ref[T_DECAY_F] = jnp.exp(lg_f * CHUNK)
        tab_ref[T_DECAY_B] = jnp.exp(lg_b * CHUNK)


def _prepare(c, c_ctx, w_mod, b_mod, w_in, logit_f, logit_b):
    d, n_mod = w_mod.shape
    r_width = HEADS * HEAD_DIM
    n_mod_blocks = n_mod // d
    assert w_in.shape == (d, IN_BLOCKS * r_width) and n_mod_blocks <= IN_BLOCKS
    mod_blk = lambda j: jnp.minimum(j, n_mod_blocks - 1)
    return pl.pallas_call(
        _prep_kernel,
        grid=(IN_BLOCKS,),
        in_specs=[pl.BlockSpec(c.shape, lambda j: (0, 0)),
                  pl.BlockSpec((1, d), lambda j: (0, 0)),
                  pl.BlockSpec((d, d), lambda j: (0, mod_blk(j))),
                  pl.BlockSpec((1, d), lambda j: (0, mod_blk(j))),
                  pl.BlockSpec((d, r_width), lambda j: (0, j)),
                  pl.BlockSpec(memory_space=pltpu.SMEM),
                  pl.BlockSpec(memory_space=pltpu.SMEM)],
        out_specs=[pl.BlockSpec((MOD_ROWS, 1, d), lambda j: (0, 0, mod_blk(j))),
                   pl.BlockSpec((d, r_width), lambda j: (0, j)),
                   pl.BlockSpec((8, CHUNK, r_width), lambda j: (0, 0, 0))],
        out_shape=[jax.ShapeDtypeStruct((MOD_ROWS, 1, n_mod), _F32),
                   jax.ShapeDtypeStruct(w_in.shape, _BF16),
                   jax.ShapeDtypeStruct((8, CHUNK, r_width), _F32)],
        compiler_params=pltpu.CompilerParams(dimension_semantics=("arbitrary",),
                                             vmem_limit_bytes=VMEM_LIMIT_BYTES),
        name="prep",
    )(c, c_ctx[None, :], w_mod, b_mod, w_in, logit_f.astype(_F32), logit_b.astype(_F32))


def _context_states(ctx, shift, scale, gain, w_k, w_v, tab_ref, k_scale):
    hc = ((_rms(ctx) * gain) * (1.0 + scale) + shift).astype(_BF16)
    k = (_dot(hc, w_k) * k_scale).astype(_BF16)
    v = _dot(hc, w_v)
    n_chunks = ctx.shape[0] // CHUNK
    s_fwd, s_bwd = [], []
    for h in range(HEADS):
        s_f = jnp.zeros((HEAD_DIM, HEAD_DIM), _F32)
        s_b = jnp.zeros((HEAD_DIM, HEAD_DIM), _F32)
        for n in range(n_chunks):
            rows = slice(n * CHUNK, (n + 1) * CHUNK)
            upd = _dot_tn(_head(k[rows], h), (_head(v[rows], h) * _head(tab_ref[T_ZETA_F], h)).astype(_BF16))
            s_f = _head(tab_ref[T_DECAY_F], h) * s_f + upd
        for n in reversed(range(n_chunks)):
            rows = slice(n * CHUNK, (n + 1) * CHUNK)
            upd = _dot_tn(_head(k[rows], h), (_head(v[rows], h) * _head(tab_ref[T_ZETA_B], h)).astype(_BF16))
            s_b = _head(tab_ref[T_DECAY_B], h) * s_b + upd
        s_fwd.append(s_f)
        s_bwd.append(s_b)
    return s_fwd, s_bwd


def _retention_tile(load, chunk_order, state, tab_ref, emit):
    keys = [(n, h) for n in chunk_order for h in range(HEADS)]
    c = {n: load(n) for n in chunk_order}
    scores = {(n, h): _dot_nt(_head(c[n]["q"], h), _head(c[n]["k"], h)) for n, h in keys}
    upd = {(n, h): _dot_tn(_head(c[n]["k"], h), _head(c[n]["vz_b"], h)) for n, h in keys}
    yield
    lhs_f, lhs_b, rhs_f, rhs_b = {}, {}, {}, {}
    for n, h in keys:
        v = _head(c[n]["v"], h)
        lhs_f[n, h] = jnp.concatenate([(scores[n, h] * _head(tab_ref[T_DMASK_F], h)).astype(_BF16),
                                       _head(c[n]["xq_f"], h)], axis=1)
        lhs_b[n, h] = jnp.concatenate([(scores[n, h] * _head(tab_ref[T_DMASK_B], h)).astype(_BF16),
                                       _head(c[n]["xq_b"], h)], axis=1)
        rhs_f[n, h] = jnp.concatenate([v, c[n]["s_f"][h]], axis=0)
        rhs_b[n, h] = jnp.concatenate([v, state[h].astype(_BF16)], axis=0)
        state[h] = _head(tab_ref[T_DECAY_B], h) * state[h] + upd[n, h]
    yield
    o_f = {key: _dot(lhs_f[key], rhs_f[key]) for key in keys}
    o_b = {key: _dot(lhs_b[key], rhs_b[key]) for key in keys}
    yield
    for n in chunk_order:
        emit(n, jnp.concatenate([_silu(_head(c[n]["gate_f"], h)) * _rms(o_f[n, h])
                                 + _silu(_head(c[n]["gate_b"], h)) * _rms(o_b[n, h]) for h in range(HEADS)],
                                axis=-1))


def _retention_stages(load, chunk_order, n_groups, s_ref, tab_ref, emit, out):
    state = [None] * HEADS
    order = list(chunk_order)
    per = len(order) // n_groups
    groups = [order[i * per:(i + 1) * per] for i in range(n_groups)]

    def group_stage(i):
        if i:
            for _ in range(i):
                yield
        inner = _retention_tile(load, groups[i], state, tab_ref, emit)
        next(inner)
        yield
        if i == 0:
            for h in range(HEADS):
                state[h] = s_ref[h]
        next(inner)
        yield
        for _ in inner:
            yield
        if i == n_groups - 1:
            for h in range(HEADS):
                out.add(s_ref, h, state[h])

    return [group_stage(i) for i in range(n_groups)]


def _interleave(*stages):
    stages = list(stages)
    while stages:
        for g in list(stages):
            try:
                next(g)
            except StopIteration:
                stages.remove(g)


class _DeferredStores:
    def __init__(self):
        self._items = []

    def add(self, ref, idx, value):
        self._items.append((ref, idx, value))

    def commit(self):
        for ref, idx, value in self._items:
            ref[idx] = value


def _skewed_steps(first, middle, last):
    s = pl.program_id(1)
    n_tiles = pl.num_programs(1) - 1

    @pl.when(s == 0)
    def _():
        first(0)

    for parity in range(2):
        @pl.when((s > 0) & (s < n_tiles) & ((s & 1) == parity))
        def _():
            middle(parity)

    for parity in range(2):
        @pl.when((s == n_tiles) & ((s & 1) == parity))
        def _():
            last(1 - parity)


def _fwd_kernel(x_ref, sh_ref, sc_ref, n1_ref, wuv_ref, wqk_ref, wvr_ref, wgf_ref, wgb_ref, gain_ref, sgw_ref, sgb_ref,
                rope_ref, tab_ref, ctx_ref, csh_ref, csc_ref, wo32_ref, wg32_ref, wu32_ref, wd32_ref,
                ya_ref, qkvg_ref, sfin_ref, scb_ref, wo16_ref, wg16_ref, wu16_ref, wd16_ref, s_ref, f32_ref, b16_ref, *,
                n_chunks, k_scale):
    a_width = GROUPS * LANES
    r_width = HEADS * HEAD_DIM
    U = 0
    VG, K, VZ = (i * r_width for i in range(3))
    rows_all = slice(None)

    def convert(out):
        for src, dst in ((wo32_ref, wo16_ref), (wg32_ref, wg16_ref), (wu32_ref, wu16_ref), (wd32_ref, wd16_ref)):
            out.add(dst, (rows_all, rows_all), src[...].astype(_BF16))

    def project(slot, out):
        n_sub = n_chunks // PROJ_CHUNKS
        for blk in range(n_sub):
            rows = slice(blk * PROJ_CHUNKS * CHUNK, (blk + 1) * PROJ_CHUNKS * CHUNK)

            def per_chunk(t, table):
                return jnp.concatenate([t[n * CHUNK:(n + 1) * CHUNK] * table for n in range(PROJ_CHUNKS)], axis=0)

            x = x_ref[rows, :]
            r = lax.rsqrt(jnp.mean(x * x, axis=-1, keepdims=True) + RMS_EPS)
            hx = ((x * r) * (n1_ref[...] * (1.0 + sc_ref[...])) + sh_ref[...]).astype(_BF16)

            def proj(w_ref, lo, width):
                return _dot(hx, w_ref[:, lo:lo + width])

            yield
            out.add(f32_ref, (slot, rows, slice(U, U + a_width)), _gelu_tanh(proj(wuv_ref, 0, a_width)))
            v = _gelu_tanh(proj(wuv_ref, a_width, a_width))
            vg = jnp.concatenate([(_rms(_head(v, g)) * _head(gain_ref[...], g)).astype(_BF16)
                                  for g in range(GROUPS)], axis=-1)
            out.add(b16_ref, (slot, rows, slice(VG, VG + a_width)), vg)
            yield

            def rope(t, table):
                cos = rope_ref[rows, 2 * table * HEAD_DIM:(2 * table + 1) * HEAD_DIM]
                sin = rope_ref[rows, (2 * table + 1) * HEAD_DIM:(2 * table + 2) * HEAD_DIM]
                return jnp.concatenate(
                    [_head(t, h) * cos + pltpu.roll(_head(t, h), HEAD_DIM // 2, 1) * sin for h in range(HEADS)],
                    axis=-1)

            q16 = rope(proj(wqk_ref, 0, r_width), 0).astype(_BF16)
            k16 = rope(proj(wqk_ref, r_width, r_width), 1).astype(_BF16)
            out.add(b16_ref, (slot, rows, slice(K, K + r_width)), k16)
            out.add(qkvg_ref, (rows, slice(0, r_width)), q16)
            out.add(qkvg_ref, (rows, slice(r_width, 2 * r_width)), k16)
            yield
            vr = proj(wvr_ref, 0, r_width)
            out.add(b16_ref, (slot, rows, slice(VZ, VZ + r_width)), per_chunk(vr, tab_ref[T_ZETA_F]).astype(_BF16))
            out.add(qkvg_ref, (rows, slice(2 * r_width, 3 * r_width)), vr.astype(_BF16))
            out.add(qkvg_ref, (rows, slice(3 * r_width, 4 * r_width)), proj(wgb_ref, 0, r_width).astype(_BF16))
            out.add(qkvg_ref, (rows, slice(4 * r_width, 5 * r_width)), proj(wgf_ref, 0, r_width).astype(_BF16))

    def gate_mix(slot, out):
        mixed = []
        for g in range(GROUPS):
            vg = jnp.concatenate([b16_ref[slot, n * CHUNK:(n + 1) * CHUNK, VG + g * LANES:VG + (g + 1) * LANES]
                                  for n in range(n_chunks)], axis=1)
            mixed.append(_dot(sgw_ref[g].astype(_BF16), vg))
        yield
        for g in range(GROUPS):
            cols = slice(g * LANES, (g + 1) * LANES)
            for n in range(n_chunks):
                rows = slice(n * CHUNK, (n + 1) * CHUNK)
                m = mixed[g][:, n * LANES:(n + 1) * LANES] + sgb_ref[:, cols]
                u = f32_ref[slot, rows, U + g * LANES:U + (g + 1) * LANES]
                out.add(ya_ref, (rows, cols), (u * m).astype(_BF16))

    def scan(slot, out):
        upd = {(n, h): _dot_tn(b16_ref[slot, n * CHUNK:(n + 1) * CHUNK, K + h * HEAD_DIM:K + (h + 1) * HEAD_DIM],
                               b16_ref[slot, n * CHUNK:(n + 1) * CHUNK, VZ + h * HEAD_DIM:VZ + (h + 1) * HEAD_DIM])
               for n in range(n_chunks) for h in range(HEADS)}
        yield
        for h in range(HEADS):
            state = s_ref[h]
            for n in range(n_chunks):
                out.add(sfin_ref, (n, h), state.astype(_BF16))
                state = _head(tab_ref[T_DECAY_F], h) * state + upd[n, h]
            out.add(s_ref, h, state)

    def first(slot):
        s_fwd, s_bwd = _context_states(ctx_ref[...], csh_ref[...], csc_ref[...], n1_ref[...],
                                       wqk_ref[:, r_width:], wvr_ref[...], tab_ref, k_scale)
        for h in range(HEADS):
            s_ref[h] = s_fwd[h]
            scb_ref[h] = s_bwd[h]
        out = _DeferredStores()
        convert(out)
        _interleave(project(slot, out))
        out.commit()

    def middle(slot):
        out = _DeferredStores()
        convert(out)
        _interleave(scan(1 - slot, out), gate_mix(1 - slot, out), project(slot, out))
        out.commit()

    def last(slot):
        out = _DeferredStores()
        convert(out)
        _interleave(scan(slot, out), gate_mix(slot, out))
        out.commit()

    _skewed_steps(first, middle, last)


def _forward_pass(x, ctx, mod3, norm1, w16, sg_gain, sg_w, sg_bias, rope, tables, bwd_weights, tile, k_scale):
    b, l, d = x.shape
    nt = l // tile
    a_width, r_width = GROUPS * LANES, HEADS * HEAD_DIM
    ctx_row = b
    state_spec = pl.BlockSpec((None, HEADS, HEAD_DIM, HEAD_DIM), lambda i, s: (i, 0, 0, 0))
    const2 = lambda i, s: (0, 0)
    cur = lambda i, s: (i, jnp.minimum(s, nt - 1), 0)
    prev = lambda i, s: (i, jnp.maximum(s - 1, 0), 0)
    resident = dict(pipeline_mode=pl.Buffered(1))
    n_blk = max(n for n in (1, 2, 4, 8, 16) if n <= b * (nt + 1))
    blk = lambda i, s: (jnp.minimum(i * (nt + 1) + s, n_blk - 1), 0)
    w_specs = [pl.BlockSpec((w.shape[0] // n_blk, w.shape[1]), blk) for w in bwd_weights]
    return pl.pallas_call(
        functools.partial(_fwd_kernel, n_chunks=tile // CHUNK, k_scale=k_scale),
        grid=(b, nt + 1),
        in_specs=[pl.BlockSpec((None, tile, d), cur),
                  pl.BlockSpec((None, 1, d), lambda i, s: (i, 0, 0)),
                  pl.BlockSpec((None, 1, d), lambda i, s: (i, 0, 1)),
                  pl.BlockSpec((1, d), const2),
                  pl.BlockSpec((d, 2 * r_width), lambda i, s: (0, U_BLOCK // 2), **resident),
                  pl.BlockSpec((d, 2 * r_width), lambda i, s: (0, Q_BLOCK // 2), **resident),
                  pl.BlockSpec((d, r_width), lambda i, s: (0, VR_BLOCK), **resident),
                  pl.BlockSpec((d, r_width), lambda i, s: (0, GF_BLOCK), **resident),
                  pl.BlockSpec((d, r_width), lambda i, s: (0, GB_BLOCK), **resident),
                  pl.BlockSpec((1, a_width), const2),
                  pl.BlockSpec(sg_w.shape, lambda i, s: (0, 0, 0)),
                  pl.BlockSpec(sg_bias.shape, const2),
                  pl.BlockSpec((tile, rope.shape[1]), lambda i, s: (jnp.minimum(s, nt - 1), 0)),
                  pl.BlockSpec(tables.shape, lambda i, s: (0, 0, 0)),
                  pl.BlockSpec((None,) + ctx.shape[1:], lambda i, s: (i, 0, 0)),
                  pl.BlockSpec((None, 1, d), lambda i, s: (ctx_row, 0, 0)),
                  pl.BlockSpec((None, 1, d), lambda i, s: (ctx_row, 0, 1))]
                 + w_specs,
        out_specs=[pl.BlockSpec((None, tile, a_width), prev),
                   pl.BlockSpec((None, tile, 5 * r_width), cur),
                   pl.BlockSpec((None, tile // CHUNK, HEADS, HEAD_DIM, HEAD_DIM),
                                lambda i, s: (i, jnp.maximum(s - 1, 0), 0, 0, 0)),
                   state_spec] + w_specs,
        out_shape=[jax.ShapeDtypeStruct((b, l, a_width), _BF16),
                   jax.ShapeDtypeStruct((b, l, 5 * r_width), _BF16),
                   jax.ShapeDtypeStruct((b, l // CHUNK, HEADS, HEAD_DIM, HEAD_DIM), _BF16),
                   jax.ShapeDtypeStruct((b, HEADS, HEAD_DIM, HEAD_DIM), _F32)]
                  + [jax.ShapeDtypeStruct(w.shape, _BF16) for w in bwd_weights],
        scratch_shapes=[pltpu.VMEM((HEADS, HEAD_DIM, HEAD_DIM), _F32),
                        pltpu.VMEM((2, tile, a_width), _F32),
                        pltpu.VMEM((2, tile, 3 * r_width), _BF16)],
        compiler_params=pltpu.CompilerParams(dimension_semantics=("arbitrary", "arbitrary"),
                                             vmem_limit_bytes=VMEM_LIMIT_BYTES),
        name="fwd",
    )(x, mod3, mod3, norm1, w16, w16, w16, w16, w16, sg_gain, sg_w, sg_bias, rope, tables, ctx, mod3, mod3,
      *bwd_weights)


def _bwd_kernel(x_ref, ya_ref, qkvg_ref, sfin_ref, g1_ref, sh_ref, sc_ref, g2_ref, n2_ref, nf_ref,
                wout_ref, wg_ref, wu_ref, wd_ref, tab_ref, s0_ref, o_ref, s_ref, yb_ref, *, n_chunks):
    a_width = GROUPS * LANES
    r_width = HEADS * HEAD_DIM

    def retain(slot, out):
        def load(n):
            rows = slice(n * CHUNK, (n + 1) * CHUNK)
            q, k, v, gate_b, gate_f = (qkvg_ref[rows, i * r_width:(i + 1) * r_width] for i in range(5))
            q32, v32 = q.astype(_F32), v.astype(_F32)
            return dict(q=q, k=k, v=v, gate_f=gate_f.astype(_F32), gate_b=gate_b.astype(_F32),
                        xq_f=(q32 * tab_ref[T_XI_F]).astype(_BF16), xq_b=(q32 * tab_ref[T_XI_B]).astype(_BF16),
                        vz_b=(v32 * tab_ref[T_ZETA_B]).astype(_BF16),
                        s_f=[sfin_ref[n, h] for h in range(HEADS)])

        def emit(n, y):
            out.add(yb_ref, (slot, slice(n * CHUNK, (n + 1) * CHUNK), slice(None)), y)

        return _retention_stages(load, list(reversed(range(n_chunks))), BWD_RETAIN_GROUPS, s_ref, tab_ref, emit, out)

    def finish(slot, out):
        y = _dot(ya_ref[...], wout_ref[:a_width, :]) + _dot(yb_ref[slot].astype(_BF16), wout_ref[a_width:, :])
        x1 = x_ref[...] + g1_ref[...] * y
        h2 = ((_rms(x1) * n2_ref[...]) * (1.0 + sc_ref[...]) + sh_ref[...]).astype(_BF16)
        acc = None
        d_ff = wg_ref.shape[1]
        for lo in range(0, d_ff, FFN_BLOCK):
            yield
            cols = slice(lo, min(lo + FFN_BLOCK, d_ff))
            act = (_silu(_dot(h2, wg_ref[:, cols])) * _dot(h2, wu_ref[:, cols])).astype(_BF16)
            part = _dot(act, wd_ref[cols, :])
            acc = part if acc is None else acc + part
        x2 = x1 + g2_ref[...] * acc
        out.add(o_ref, (slice(None), slice(None)), _rms(x2) * nf_ref[...])

    def first(slot):
        s_ref[...] = s0_ref[...]
        out = _DeferredStores()
        _interleave(*retain(slot, out))
        out.commit()

    def middle(slot):
        out = _DeferredStores()
        _interleave(*retain(slot, out), finish(1 - slot, out))
        out.commit()

    def last(slot):
        out = _DeferredStores()
        _interleave(finish(slot, out))
        out.commit()

    _skewed_steps(first, middle, last)


def _backward_pass(x, ya, qkvg, sfin, mod3, norm2, norm_f, w_out, w_gate, w_up, w_down, tables, s_cb, tile):
    b, l, d = x.shape
    nt = l // tile
    const2 = lambda i, s: (0, 0)
    cur = lambda i, s: (i, nt - 1 - jnp.minimum(s, nt - 1), 0)
    prev = lambda i, s: (i, nt - 1 - jnp.maximum(s - 1, 0), 0)
    resident = dict(pipeline_mode=pl.Buffered(1))
    mod_spec = lambda j: pl.BlockSpec((None, 1, d), lambda i, s: (i, 0, j))
    return pl.pallas_call(
        functools.partial(_bwd_kernel, n_chunks=tile // CHUNK),
        grid=(b, nt + 1),
        in_specs=[pl.BlockSpec((None, tile, d), prev),
                  pl.BlockSpec((None, tile, ya.shape[-1]), prev),
                  pl.BlockSpec((None, tile, qkvg.shape[-1]), cur),
                  pl.BlockSpec((None, tile // CHUNK, HEADS, HEAD_DIM, HEAD_DIM),
                               lambda i, s: (i, nt - 1 - jnp.minimum(s, nt - 1), 0, 0, 0)),
                  mod_spec(2), mod_spec(3), mod_spec(4), mod_spec(5),
                  pl.BlockSpec((1, d), const2),
                  pl.BlockSpec((1, d), const2),
                  pl.BlockSpec(w_out.shape, const2, **resident),
                  pl.BlockSpec(w_gate.shape, const2, **resident),
                  pl.BlockSpec(w_up.shape, const2, **resident),
                  pl.BlockSpec(w_down.shape, const2, **resident),
                  pl.BlockSpec(tables.shape, lambda i, s: (0, 0, 0)),
                  pl.BlockSpec((None, HEADS, HEAD_DIM, HEAD_DIM), lambda i, s: (i, 0, 0, 0))],
        out_specs=pl.BlockSpec((None, tile, d), prev),
        out_shape=jax.ShapeDtypeStruct((b, l, d), x.dtype),
        scratch_shapes=[pltpu.VMEM((HEADS, HEAD_DIM, HEAD_DIM), _F32),
                        pltpu.VMEM((2, tile, HEADS * HEAD_DIM), _F32)],
        compiler_params=pltpu.CompilerParams(dimension_semantics=("arbitrary", "arbitrary"),
                                             vmem_limit_bytes=VMEM_LIMIT_BYTES),
        name="bwd",
    )(x, ya, qkvg, sfin, mod3, mod3, mod3, mod3, norm2, norm_f, w_out, w_gate, w_up, w_down, tables, s_cb)


def _rope_tables(l, k_scale):
    n_freq = HEAD_DIM // 4
    inv = np.float32(ROPE_BASE) ** (-np.arange(n_freq, dtype=np.float32) / np.float32(n_freq))
    rows = np.repeat(np.arange(l // GRID_W, dtype=np.float32), GRID_W)
    cols = np.tile(np.arange(GRID_W, dtype=np.float32), l // GRID_W)
    ang_r, ang_c = rows[:, None] * inv[None, :], cols[:, None] * inv[None, :]
    cos = np.concatenate([np.cos(ang_r), np.cos(ang_c)], axis=-1)
    sin = np.concatenate([np.sin(ang_r), np.sin(ang_c)], axis=-1)
    cos2, sin2 = np.concatenate([cos, cos], axis=-1), np.concatenate([-sin, sin], axis=-1)
    return np.concatenate([cos2, sin2, cos2 * k_scale, sin2 * k_scale], axis=-1).astype(np.float32)


def kernel(x, c, ctx, c_ctx, w_mod, b_mod, norm1, w_in, sg_gain, sg_w, sg_b, ret_logit_f, ret_logit_b,
           w_out, norm2, w_gate, w_up, w_down, norm_f):
    b, l, d = x.shape
    a_width, r_width = GROUPS * LANES, HEADS * HEAD_DIM
    tile = min(TILE, l)
    assert w_mod.shape[0] == 1 and b + 1 <= MOD_ROWS and l % tile == 0 and tile % CHUNK == 0
    assert ctx.shape[1] % CHUNK == 0 and l // tile >= 2
    assert a_width == r_width and d == a_width + r_width
    k_scale = HEAD_DIM ** -0.5

    mod3, w16, tables = _prepare(c, c_ctx, w_mod[0], b_mod[0][None, :], w_in[0], ret_logit_f[0], ret_logit_b[0])
    sg_bias = jnp.broadcast_to(jnp.transpose(sg_b[0])[:, :, None], (CHUNK, GROUPS, LANES)).reshape(CHUNK, a_width)
    ya, qkvg, sfin, s_cb, w_out_b, w_gate_b, w_up_b, w_down_b = _forward_pass(
        x, ctx, mod3, norm1, w16, sg_gain, sg_w[0], sg_bias, _rope_tables(l, k_scale),
        tables, (w_out[0], w_gate[0], w_up[0], w_down[0]), tile, k_scale)
    return _backward_pass(x, ya, qkvg, sfin, mod3, norm2, norm_f[None, :], w_out_b, w_gate_b, w_up_b, w_down_b,
                          tables, s_cb, tile)
```

```python
import functools

import numpy as np
import jax
import jax.numpy as jnp
from jax import lax
from jax.experimental import pallas as pl
from jax.experimental.pallas import tpu as pltpu

CHUNK = 128
GRID_W = 64
GROUPS = 4
HEADS = 4
HEAD_DIM = 128
RMS_EPS = 1e-6
ROPE_BASE = 10000.0
LANES = 128
MOD_ROWS = 8
TILE = 512
PROJ_CHUNKS = 2
BWD_RETAIN_GROUPS = 1
FFN_BLOCK = 1024
VMEM_LIMIT_BYTES = 56 * 1024 * 1024

IN_BLOCKS = 7
U_BLOCK, V_BLOCK, Q_BLOCK, K_BLOCK, VR_BLOCK, GF_BLOCK, GB_BLOCK = range(IN_BLOCKS)

T_DMASK_F, T_DMASK_B, T_XI_F, T_XI_B, T_ZETA_F, T_ZETA_B, T_DECAY_F, T_DECAY_B = range(8)

_BF16 = jnp.bfloat16
_F32 = jnp.float32


def _dot(a, b):
    return jnp.dot(a, b, preferred_element_type=_F32)


def _dot_nt(a, b):
    return lax.dot_general(a, b, (((1,), (1,)), ((), ())), preferred_element_type=_F32)


def _dot_tn(a, b):
    return lax.dot_general(a, b, (((0,), (0,)), ((), ())), preferred_element_type=_F32)


def _silu(x):
    return x * (1.0 / (1.0 + jnp.exp(-x)))


def _gelu_tanh(x):
    c = float(np.sqrt(2.0 / np.pi))
    half = 0.5 * x
    return half + half * jnp.tanh(x * (c + (0.044715 * c) * (x * x)))


def _rms(x):
    return x * lax.rsqrt(jnp.mean(x * x, axis=-1, keepdims=True) + RMS_EPS)


def _head(a, h):
    return a[:, h * HEAD_DIM:(h + 1) * HEAD_DIM]


def _prep_kernel(c_ref, cctx_ref, wmod_ref, bmod_ref, win_ref, lf_ref, lb_ref, mod_ref, w16_ref, tab_ref):
    j = pl.program_id(0)

    n_batch, d = c_ref.shape
    cc = jnp.concatenate([c_ref[...], cctx_ref[...], jnp.zeros((MOD_ROWS - n_batch - 1, d), _F32)], axis=0)
    mod = _dot(_silu(cc), wmod_ref[...]) + bmod_ref[...]
    for r in range(MOD_ROWS):
        mod_ref[r] = mod[r:r + 1, :]

    def reorder_head_dims(t):
        quarter = HEAD_DIM // 4
        lane = lax.broadcasted_iota(jnp.int32, (t.shape[0], HEAD_DIM), 1)
        heads = []
        for h in range(t.shape[1] // HEAD_DIM):
            th = _head(t, h)
            from_right = pltpu.roll(th, HEAD_DIM - quarter, 1)
            from_left = pltpu.roll(th, quarter, 1)
            heads.append(jnp.where((lane >= quarter) & (lane < 2 * quarter), from_right,
                                   jnp.where((lane >= 2 * quarter) & (lane < 3 * quarter), from_left, th)))
        return jnp.concatenate(heads, axis=-1)

    is_qk = (j == Q_BLOCK) | (j == K_BLOCK)

    @pl.when(is_qk)
    def _():
        w16_ref[...] = reorder_head_dims(win_ref[...]).astype(_BF16)

    @pl.when(jnp.logical_not(is_qk))
    def _():
        w16_ref[...] = win_ref[...].astype(_BF16)

    @pl.when(j == 0)
    def _():
        def log_sigmoid(x):
            return -(jnp.maximum(-x, 0.0) + jnp.log1p(jnp.exp(-jnp.abs(x))))

        shape = (CHUNK, HEADS * HEAD_DIM)
        head = lax.broadcasted_iota(jnp.int32, shape, 1) // HEAD_DIM

        def per_head(logit_ref):
            t = jnp.zeros(shape, _F32)
            for h in range(HEADS):
                t = jnp.where(head == h, logit_ref[h], t)
            return t

        lg_f = log_sigmoid(per_head(lf_ref))
        lg_b = log_sigmoid(per_head(lb_ref))
        i = lax.broadcasted_iota(jnp.int32, shape, 0).astype(_F32)
        jj = (lax.broadcasted_iota(jnp.int32, shape, 1) & (HEAD_DIM - 1)).astype(_F32)
        tab_ref[T_DMASK_F] = jnp.where(i >= jj, jnp.exp(lg_f * jnp.maximum(i - jj, 0.0)), 0.0)
        tab_ref[T_DMASK_B] = jnp.where(jj >= i, jnp.exp(lg_b * jnp.maximum(jj - i, 0.0)), 0.0)
        tab_ref[T_XI_F] = jnp.exp(lg_f * (i + 1.0))
        tab_ref[T_XI_B] = jnp.exp(lg_b * (CHUNK - i))
        tab_ref[T_ZETA_F] = jnp.exp(lg_f * (CHUNK - 1.0 - i))
        tab_ref[T_ZETA_B] = jnp.exp(lg_b * i)
        tab_ref[T_DECAY_F] = jnp.exp(lg_f * CHUNK)
        tab_ref[T_DECAY_B] = jnp.exp(lg_b * CHUNK)


def _prepare(c, c_ctx, w_mod, b_mod, w_in, logit_f, logit_b):
    d, n_mod = w_mod.shape
    r_width = HEADS * HEAD_DIM
    n_mod_blocks = n_mod // d
    assert w_in.shape == (d, IN_BLOCKS * r_width) and n_mod_blocks <= IN_BLOCKS
    mod_blk = lambda j: jnp.minimum(j, n_mod_blocks - 1)
    return pl.pallas_call(
        _prep_kernel,
        grid=(IN_BLOCKS,),
        in_specs=[pl.BlockSpec(c.shape, lambda j: (0, 0)),
                  pl.BlockSpec((1, d), lambda j: (0, 0)),
                  pl.BlockSpec((d, d), lambda j: (0, mod_blk(j))),
                  pl.BlockSpec((1, d), lambda j: (0, mod_blk(j))),
                  pl.BlockSpec((d, r_width), lambda j: (0, j)),
                  pl.BlockSpec(memory_space=pltpu.SMEM),
                  pl.BlockSpec(memory_space=pltpu.SMEM)],
        out_specs=[pl.BlockSpec((MOD_ROWS, 1, d), lambda j: (0, 0, mod_blk(j))),
                   pl.BlockSpec((d, r_width), lambda j: (0, j)),
                   pl.BlockSpec((8, CHUNK, r_width), lambda j: (0, 0, 0))],
        out_shape=[jax.ShapeDtypeStruct((MOD_ROWS, 1, n_mod), _F32),
                   jax.ShapeDtypeStruct(w_in.shape, _BF16),
                   jax.ShapeDtypeStruct((8, CHUNK, r_width), _F32)],
        compiler_params=pltpu.CompilerParams(dimension_semantics=("arbitrary",),
                                             vmem_limit_bytes=VMEM_LIMIT_BYTES),
        name="prep",
    )(c, c_ctx[None, :], w_mod, b_mod, w_in, logit_f.astype(_F32), logit_b.astype(_F32))


def _context_states(ctx, shift, scale, gain, w_k, w_v, tab_ref, k_scale):
    hc = ((_rms(ctx) * gain) * (1.0 + scale) + shift).astype(_BF16)
    k = (_dot(hc, w_k) * k_scale).astype(_BF16)
    v = _dot(hc, w_v)
    n_chunks = ctx.shape[0] // CHUNK
    s_fwd, s_bwd = [], []
    for h in range(HEADS):
        s_f = jnp.zeros((HEAD_DIM, HEAD_DIM), _F32)
        s_b = jnp.zeros((HEAD_DIM, HEAD_DIM), _F32)
        for n in range(n_chunks):
            rows = slice(n * CHUNK, (n + 1) * CHUNK)
            upd = _dot_tn(_head(k[rows], h), (_head(v[rows], h) * _head(tab_ref[T_ZETA_F], h)).astype(_BF16))
            s_f = _head(tab_ref[T_DECAY_F], h) * s_f + upd
        for n in reversed(range(n_chunks)):
            rows = slice(n * CHUNK, (n + 1) * CHUNK)
            upd = _dot_tn(_head(k[rows], h), (_head(v[rows], h) * _head(tab_ref[T_ZETA_B], h)).astype(_BF16))
            s_b = _head(tab_ref[T_DECAY_B], h) * s_b + upd
        s_fwd.append(s_f)
        s_bwd.append(s_b)
    return s_fwd, s_bwd


def _retention_tile(load, chunk_order, state, tab_ref, emit):
    keys = [(n, h) for n in chunk_order for h in range(HEADS)]
    c = {n: load(n) for n in chunk_order}
    scores = {(n, h): _dot_nt(_head(c[n]["q"], h), _head(c[n]["k"], h)) for n, h in keys}
    upd = {(n, h): _dot_tn(_head(c[n]["k"], h), _head(c[n]["vz_b"], h)) for n, h in keys}
    yield
    lhs_f, lhs_b, rhs_f, rhs_b = {}, {}, {}, {}
    for n, h in keys:
        v = _head(c[n]["v"], h)
        lhs_f[n, h] = jnp.concatenate([(scores[n, h] * _head(tab_ref[T_DMASK_F], h)).astype(_BF16),
                                       _head(c[n]["xq_f"], h)], axis=1)
        lhs_b[n, h] = jnp.concatenate([(scores[n, h] * _head(tab_ref[T_DMASK_B], h)).astype(_BF16),
                                       _head(c[n]["xq_b"], h)], axis=1)
        rhs_f[n, h] = jnp.concatenate([v, c[n]["s_f"][h]], axis=0)
        rhs_b[n, h] = jnp.concatenate([v, state[h].astype(_BF16)], axis=0)
        state[h] = _head(tab_ref[T_DECAY_B], h) * state[h] + upd[n, h]
    yield
    o_f = {key: _dot(lhs_f[key], rhs_f[key]) for key in keys}
    o_b = {key: _dot(lhs_b[key], rhs_b[key]) for key in keys}
    yield
    for n in chunk_order:
        emit(n, jnp.concatenate([_silu(_head(c[n]["gate_f"], h)) * _rms(o_f[n, h])
                                 + _silu(_head(c[n]["gate_b"], h)) * _rms(o_b[n, h]) for h in range(HEADS)],
                                axis=-1))


def _retention_stages(load, chunk_order, n_groups, s_ref, tab_ref, emit, out):
    state = [None] * HEADS
    order = list(chunk_order)
    per = len(order) // n_groups
    groups = [order[i * per:(i + 1) * per] for i in range(n_groups)]

    def group_stage(i):
        if i:
            for _ in range(i):
                yield
        inner = _retention_tile(load, groups[i], state, tab_ref, emit)
        next(inner)
        yield
        if i == 0:
            for h in range(HEADS):
                state[h] = s_ref[h]
        next(inner)
        yield
        for _ in inner:
            yield
        if i == n_groups - 1:
            for h in range(HEADS):
                out.add(s_ref, h, state[h])

    return [group_stage(i) for i in range(n_groups)]


def _interleave(*stages):
    stages = list(stages)
    while stages:
        for g in list(stages):
            try:
                next(g)
            except StopIteration:
                stages.remove(g)


class _DeferredStores:
    def __init__(self):
        self._items = []

    def add(self, ref, idx, value):
        self._items.append((ref, idx, value))

    def commit(self):
        for ref, idx, value in self._items:
            ref[idx] = value


def _skewed_steps(first, middle, last):
    s = pl.program_id(1)
    n_tiles = pl.num_programs(1) - 1

    @pl.when(s == 0)
    def _():
        first(0)

    for parity in range(2):
        @pl.when((s > 0) & (s < n_tiles) & ((s & 1) == parity))
        def _():
            middle(parity)

    for parity in range(2):
        @pl.when((s == n_tiles) & ((s & 1) == parity))
        def _():
            last(1 - parity)


def _fwd_kernel(x_ref, sh_ref, sc_ref, n1_ref, wuv_ref, wqk_ref, wvr_ref, wgf_ref, wgb_ref, gain_ref, sgw_ref, sgb_ref,
                rope_ref, tab_ref, ctx_ref, csh_ref, csc_ref, wo32_ref, wg32_ref, wu32_ref, wd32_ref,
                ya_ref, qkvg_ref, sfin_ref, scb_ref, wo16_ref, wg16_ref, wu16_ref, wd16_ref, s_ref, f32_ref, b16_ref, *,
                n_chunks, k_scale):
    a_width = GROUPS * LANES
    r_width = HEADS * HEAD_DIM
    U = 0
    VG, K, VZ = (i * r_width for i in range(3))
    rows_all = slice(None)

    def convert(out):
        for src, dst in ((wo32_ref, wo16_ref), (wg32_ref, wg16_ref), (wu32_ref, wu16_ref), (wd32_ref, wd16_ref)):
            out.add(dst, (rows_all, rows_all), src[...].astype(_BF16))

    def project(slot, out):
        n_sub = n_chunks // PROJ_CHUNKS
        for blk in range(n_sub):
            rows = slice(blk * PROJ_CHUNKS * CHUNK, (blk + 1) * PROJ_CHUNKS * CHUNK)

            def per_chunk(t, table):
                return jnp.concatenate([t[n * CHUNK:(n + 1) * CHUNK] * table for n in range(PROJ_CHUNKS)], axis=0)

            x = x_ref[rows, :]
            r = lax.rsqrt(jnp.mean(x * x, axis=-1, keepdims=True) + RMS_EPS)
            hx = ((x * r) * (n1_ref[...] * (1.0 + sc_ref[...])) + sh_ref[...]).astype(_BF16)

            def proj(w_ref, lo, width):
                return _dot(hx, w_ref[:, lo:lo + width])

            yield
            out.add(f32_ref, (slot, rows, slice(U, U + a_width)), _gelu_tanh(proj(wuv_ref, 0, a_width)))
            v = _gelu_tanh(proj(wuv_ref, a_width, a_width))
            vg = jnp.concatenate([(_rms(_head(v, g)) * _head(gain_ref[...], g)).astype(_BF16)
                                  for g in range(GROUPS)], axis=-1)
            out.add(b16_ref, (slot, rows, slice(VG, VG + a_width)), vg)
            yield

            def rope(t, table):
                cos = rope_ref[rows, 2 * table * HEAD_DIM:(2 * table + 1) * HEAD_DIM]
                sin = rope_ref[rows, (2 * table + 1) * HEAD_DIM:(2 * table + 2) * HEAD_DIM]
                return jnp.concatenate(
                    [_head(t, h) * cos + pltpu.roll(_head(t, h), HEAD_DIM // 2, 1) * sin for h in range(HEADS)],
                    axis=-1)

            q16 = rope(proj(wqk_ref, 0, r_width), 0).astype(_BF16)
            k16 = rope(proj(wqk_ref, r_width, r_width), 1).astype(_BF16)
            out.add(b16_ref, (slot, rows, slice(K, K + r_width)), k16)
            out.add(qkvg_ref, (rows, slice(0, r_width)), q16)
            out.add(qkvg_ref, (rows, slice(r_width, 2 * r_width)), k16)
            yield
            vr = proj(wvr_ref, 0, r_width)
            out.add(b16_ref, (slot, rows, slice(VZ, VZ + r_width)), per_chunk(vr, tab_ref[T_ZETA_F]).astype(_BF16))
            out.add(qkvg_ref, (rows, slice(2 * r_width, 3 * r_width)), vr.astype(_BF16))
            out.add(qkvg_ref, (rows, slice(3 * r_width, 4 * r_width)), proj(wgb_ref, 0, r_width).astype(_BF16))
            out.add(qkvg_ref, (rows, slice(4 * r_width, 5 * r_width)), proj(wgf_ref, 0, r_width).astype(_BF16))

    def gate_mix(slot, out):
        mixed = []
        for g in range(GROUPS):
            vg = jnp.concatenate([b16_ref[slot, n * CHUNK:(n + 1) * CHUNK, VG + g * LANES:VG + (g + 1) * LANES]
                                  for n in range(n_chunks)], axis=1)
            mixed.append(_dot(sgw_ref[g].astype(_BF16), vg))
        yield
        for g in range(GROUPS):
            cols = slice(g * LANES, (g + 1) * LANES)
            for n in range(n_chunks):
                rows = slice(n * CHUNK, (n + 1) * CHUNK)
                m = mixed[g][:, n * LANES:(n + 1) * LANES] + sgb_ref[:, cols]
                u = f32_ref[slot, rows, U + g * LANES:U + (g + 1) * LANES]
                out.add(ya_ref, (rows, cols), (u * m).astype(_BF16))

    def scan(slot, out):
        upd = {(n, h): _dot_tn(b16_ref[slot, n * CHUNK:(n + 1) * CHUNK, K + h * HEAD_DIM:K + (h + 1) * HEAD_DIM],
                               b16_ref[slot, n * CHUNK:(n + 1) * CHUNK, VZ + h * HEAD_DIM:VZ + (h + 1) * HEAD_DIM])
               for n in range(n_chunks) for h in range(HEADS)}
        yield
        for h in range(HEADS):
            state = s_ref[h]
            for n in range(n_chunks):
                out.add(sfin_ref, (n, h), state.astype(_BF16))
                state = _head(tab_ref[T_DECAY_F], h) * state + upd[n, h]
            out.add(s_ref, h, state)

    def first(slot):
        s_fwd, s_bwd = _context_states(ctx_ref[...], csh_ref[...], csc_ref[...], n1_ref[...],
                                       wqk_ref[:, r_width:], wvr_ref[...], tab_ref, k_scale)
        for h in range(HEADS):
            s_ref[h] = s_fwd[h]
            scb_ref[h] = s_bwd[h]
        out = _DeferredStores()
        convert(out)
        _interleave(project(slot, out))
        out.commit()

    def middle(slot):
        out = _DeferredStores()
        convert(out)
        _interleave(scan(1 - slot, out), gate_mix(1 - slot, out), project(slot, out))
        out.commit()

    def last(slot):
        out = _DeferredStores()
        convert(out)
        _interleave(scan(slot, out), gate_mix(slot, out))
        out.commit()

    _skewed_steps(first, middle, last)


def _forward_pass(x, ctx, mod3, norm1, w16, sg_gain, sg_w, sg_bias, rope, tables, bwd_weights, tile, k_scale):
    b, l, d = x.shape
    nt = l // tile
    a_width, r_width = GROUPS * LANES, HEADS * HEAD_DIM
    ctx_row = b
    state_spec = pl.BlockSpec((None, HEADS, HEAD_DIM, HEAD_DIM), lambda i, s: (i, 0, 0, 0))
    const2 = lambda i, s: (0, 0)
    cur = lambda i, s: (i, jnp.minimum(s, nt - 1), 0)
    prev = lambda i, s: (i, jnp.maximum(s - 1, 0), 0)
    resident = dict(pipeline_mode=pl.Buffered(1))
    n_blk = max(n for n in (1, 2, 4, 8, 16) if n <= b * (nt + 1))
    blk = lambda i, s: (jnp.minimum(i * (nt + 1) + s, n_blk - 1), 0)
    w_specs = [pl.BlockSpec((w.shape[0] // n_blk, w.shape[1]), blk) for w in bwd_weights]
    return pl.pallas_call(
        functools.partial(_fwd_kernel, n_chunks=tile // CHUNK, k_scale=k_scale),
        grid=(b, nt + 1),
        in_specs=[pl.BlockSpec((None, tile, d), cur),
                  pl.BlockSpec((None, 1, d), lambda i, s: (i, 0, 0)),
                  pl.BlockSpec((None, 1, d), lambda i, s: (i, 0, 1)),
                  pl.BlockSpec((1, d), const2),
                  pl.BlockSpec((d, 2 * r_width), lambda i, s: (0, U_BLOCK // 2), **resident),
                  pl.BlockSpec((d, 2 * r_width), lambda i, s: (0, Q_BLOCK // 2), **resident),
                  pl.BlockSpec((d, r_width), lambda i, s: (0, VR_BLOCK), **resident),
                  pl.BlockSpec((d, r_width), lambda i, s: (0, GF_BLOCK), **resident),
                  pl.BlockSpec((d, r_width), lambda i, s: (0, GB_BLOCK), **resident),
                  pl.BlockSpec((1, a_width), const2),
                  pl.BlockSpec(sg_w.shape, lambda i, s: (0, 0, 0)),
                  pl.BlockSpec(sg_bias.shape, const2),
                  pl.BlockSpec((tile, rope.shape[1]), lambda i, s: (jnp.minimum(s, nt - 1), 0)),
                  pl.BlockSpec(tables.shape, lambda i, s: (0, 0, 0)),
                  pl.BlockSpec((None,) + ctx.shape[1:], lambda i, s: (i, 0, 0)),
                  pl.BlockSpec((None, 1, d), lambda i, s: (ctx_row, 0, 0)),
                  pl.BlockSpec((None, 1, d), lambda i, s: (ctx_row, 0, 1))]
                 + w_specs,
        out_specs=[pl.BlockSpec((None, tile, a_width), prev),
                   pl.BlockSpec((None, tile, 5 * r_width), cur),
                   pl.BlockSpec((None, tile // CHUNK, HEADS, HEAD_DIM, HEAD_DIM),
                                lambda i, s: (i, jnp.maximum(s - 1, 0), 0, 0, 0)),
                   state_spec] + w_specs,
        out_shape=[jax.ShapeDtypeStruct((b, l, a_width), _BF16),
                   jax.ShapeDtypeStruct((b, l, 5 * r_width), _BF16),
                   jax.ShapeDtypeStruct((b, l // CHUNK, HEADS, HEAD_DIM, HEAD_DIM), _BF16),
                   jax.ShapeDtypeStruct((b, HEADS, HEAD_DIM, HEAD_DIM), _F32)]
                  + [jax.ShapeDtypeStruct(w.shape, _BF16) for w in bwd_weights],
        scratch_shapes=[pltpu.VMEM((HEADS, HEAD_DIM, HEAD_DIM), _F32),
                        pltpu.VMEM((2, tile, a_width), _F32),
                        pltpu.VMEM((2, tile, 3 * r_width), _BF16)],
        compiler_params=pltpu.CompilerParams(dimension_semantics=("arbitrary", "arbitrary"),
                                             vmem_limit_bytes=VMEM_LIMIT_BYTES),
        name="fwd",
    )(x, mod3, mod3, norm1, w16, w16, w16, w16, w16, sg_gain, sg_w, sg_bias, rope, tables, ctx, mod3, mod3,
      *bwd_weights)


def _bwd_kernel(x_ref, ya_ref, qkvg_ref, sfin_ref, g1_ref, sh_ref, sc_ref, g2_ref, n2_ref, nf_ref,
                wout_ref, wg_ref, wu_ref, wd_ref, tab_ref, s0_ref, o_ref, s_ref, yb_ref, *, n_chunks):
    a_width = GROUPS * LANES
    r_width = HEADS * HEAD_DIM

    def retain(slot, out):
        def load(n):
            rows = slice(n * CHUNK, (n + 1) * CHUNK)
            q, k, v, gate_b, gate_f = (qkvg_ref[rows, i * r_width:(i + 1) * r_width] for i in range(5))
            q32, v32 = q.astype(_F32), v.astype(_F32)
            return dict(q=q, k=k, v=v, gate_f=gate_f.astype(_F32), gate_b=gate_b.astype(_F32),
                        xq_f=(q32 * tab_ref[T_XI_F]).astype(_BF16), xq_b=(q32 * tab_ref[T_XI_B]).astype(_BF16),
                        vz_b=(v32 * tab_ref[T_ZETA_B]).astype(_BF16),
                        s_f=[sfin_ref[n, h] for h in range(HEADS)])

        def emit(n, y):
            out.add(yb_ref, (slot, slice(n * CHUNK, (n + 1) * CHUNK), slice(None)), y)

        return _retention_stages(load, list(reversed(range(n_chunks))), BWD_RETAIN_GROUPS, s_ref, tab_ref, emit, out)

    def finish(slot, out):
        y = _dot(ya_ref[...], wout_ref[:a_width, :]) + _dot(yb_ref[slot].astype(_BF16), wout_ref[a_width:, :])
        x1 = x_ref[...] + g1_ref[...] * y
        h2 = ((_rms(x1) * n2_ref[...]) * (1.0 + sc_ref[...]) + sh_ref[...]).astype(_BF16)
        acc = None
        d_ff = wg_ref.shape[1]
        for lo in range(0, d_ff, FFN_BLOCK):
            yield
            cols = slice(lo, min(lo + FFN_BLOCK, d_ff))
            act = (_silu(_dot(h2, wg_ref[:, cols])) * _dot(h2, wu_ref[:, cols])).astype(_BF16)
            part = _dot(act, wd_ref[cols, :])
            acc = part if acc is None else acc + part
        x2 = x1 + g2_ref[...] * acc
        out.add(o_ref, (slice(None), slice(None)), _rms(x2) * nf_ref[...])

    def first(slot):
        s_ref[...] = s0_ref[...]
        out = _DeferredStores()
        _interleave(*retain(slot, out))
        out.commit()

    def middle(slot):
        out = _DeferredStores()
        _interleave(*retain(slot, out), finish(1 - slot, out))
        out.commit()

    def last(slot):
        out = _DeferredStores()
        _interleave(finish(slot, out))
        out.commit()

    _skewed_steps(first, middle, last)


def _backward_pass(x, ya, qkvg, sfin, mod3, norm2, norm_f, w_out, w_gate, w_up, w_down, tables, s_cb, tile):
    b, l, d = x.shape
    nt = l // tile
    const2 = lambda i, s: (0, 0)
    cur = lambda i, s: (i, nt - 1 - jnp.minimum(s, nt - 1), 0)
    prev = lambda i, s: (i, nt - 1 - jnp.maximum(s - 1, 0), 0)
    resident = dict(pipeline_mode=pl.Buffered(1))
    mod_spec = lambda j: pl.BlockSpec((None, 1, d), lambda i, s: (i, 0, j))
    return pl.pallas_call(
        functools.partial(_bwd_kernel, n_chunks=tile // CHUNK),
        grid=(b, nt + 1),
        in_specs=[pl.BlockSpec((None, tile, d), prev),
                  pl.BlockSpec((None, tile, ya.shape[-1]), prev),
                  pl.BlockSpec((None, tile, qkvg.shape[-1]), cur),
                  pl.BlockSpec((None, tile // CHUNK, HEADS, HEAD_DIM, HEAD_DIM),
                               lambda i, s: (i, nt - 1 - jnp.minimum(s, nt - 1), 0, 0, 0)),
                  mod_spec(2), mod_spec(3), mod_spec(4), mod_spec(5),
                  pl.BlockSpec((1, d), const2),
                  pl.BlockSpec((1, d), const2),
                  pl.BlockSpec(w_out.shape, const2, **resident),
                  pl.BlockSpec(w_gate.shape, const2, **resident),
                  pl.BlockSpec(w_up.shape, const2, **resident),
                  pl.BlockSpec(w_down.shape, const2, **resident),
                  pl.BlockSpec(tables.shape, lambda i, s: (0, 0, 0)),
                  pl.BlockSpec((None, HEADS, HEAD_DIM, HEAD_DIM), lambda i, s: (i, 0, 0, 0))],
        out_specs=pl.BlockSpec((None, tile, d), prev),
        out_shape=jax.ShapeDtypeStruct((b, l, d), x.dtype),
        scratch_shapes=[pltpu.VMEM((HEADS, HEAD_DIM, HEAD_DIM), _F32),
                        pltpu.VMEM((2, tile, HEADS * HEAD_DIM), _F32)],
        compiler_params=pltpu.CompilerParams(dimension_semantics=("arbitrary", "arbitrary"),
                                             vmem_limit_bytes=VMEM_LIMIT_BYTES),
        name="bwd",
    )(x, ya, qkvg, sfin, mod3, mod3, mod3, mod3, norm2, norm_f, w_out, w_gate, w_up, w_down, tables, s_cb)


def _rope_tables(l, k_scale):
    n_freq = HEAD_DIM // 4
    inv = np.float32(ROPE_BASE) ** (-np.arange(n_freq, dtype=np.float32) / np.float32(n_freq))
    rows = np.repeat(np.arange(l // GRID_W, dtype=np.float32), GRID_W)
    cols = np.tile(np.arange(GRID_W, dtype=np.float32), l // GRID_W)
    ang_r, ang_c = rows[:, None] * inv[None, :], cols[:, None] * inv[None, :]
    cos = np.concatenate([np.cos(ang_r), np.cos(ang_c)], axis=-1)
    sin = np.concatenate([np.sin(ang_r), np.sin(ang_c)], axis=-1)
    cos2, sin2 = np.concatenate([cos, cos], axis=-1), np.concatenate([-sin, sin], axis=-1)
    return np.concatenate([cos2, sin2, cos2 * k_scale, sin2 * k_scale], axis=-1).astype(np.float32)


def kernel(x, c, ctx, c_ctx, w_mod, b_mod, norm1, w_in, sg_gain, sg_w, sg_b, ret_logit_f, ret_logit_b,
           w_out, norm2, w_gate, w_up, w_down, norm_f):
    b, l, d = x.shape
    a_width, r_width = GROUPS * LANES, HEADS * HEAD_DIM
    tile = min(TILE, l)
    assert w_mod.shape[0] == 1 and b + 1 <= MOD_ROWS and l % tile == 0 and tile % CHUNK == 0
    assert ctx.shape[1] % CHUNK == 0 and l // tile >= 2
    assert a_width == r_width and d == a_width + r_width
    k_scale = HEAD_DIM ** -0.5

    mod3, w16, tables = _prepare(c, c_ctx, w_mod[0], b_mod[0][None, :], w_in[0], ret_logit_f[0], ret_logit_b[0])
    sg_bias = jnp.broadcast_to(jnp.transpose(sg_b[0])[:, :, None], (CHUNK, GROUPS, LANES)).reshape(CHUNK, a_width)
    ya, qkvg, sfin, s_cb, w_out_b, w_gate_b, w_up_b, w_down_b = _forward_pass(
        x, ctx, mod3, norm1, w16, sg_gain, sg_w[0], sg_bias, _rope_tables(l, k_scale),
        tables, (w_out[0], w_gate[0], w_up[0], w_down[0]), tile, k_scale)
    return _backward_pass(x, ya, qkvg, sfin, mod3, norm2, norm_f[None, :], w_out_b, w_gate_b, w_up_b, w_down_b,
                          tables, s_cb, tile)
```

```python
import functools

import numpy as np
import jax
import jax.numpy as jnp
from jax import lax
from jax.experimental import pallas as pl
from jax.experimental.pallas import tpu as pltpu

CHUNK = 128
GRID_W = 64
GROUPS = 4
HEADS = 4
HEAD_DIM = 128
RMS_EPS = 1e-6
ROPE_BASE = 10000.0
LANES = 128
MOD_ROWS = 8
TILE = 512
PROJ_CHUNKS = 2
FWD_RETAIN_GROUPS = 4
BWD_RETAIN_GROUPS = 2
FFN_BLOCK = 1024
VMEM_LIMIT_BYTES = 56 * 1024 * 1024

IN_BLOCKS = 7
U_BLOCK, V_BLOCK, Q_BLOCK, K_BLOCK, VR_BLOCK, GF_BLOCK, GB_BLOCK = range(IN_BLOCKS)

T_DMASK_F, T_DMASK_B, T_XI_F, T_XI_B, T_ZETA_F, T_ZETA_B, T_DECAY_F, T_DECAY_B = range(8)

_BF16 = jnp.bfloat16
_F32 = jnp.float32


def _dot(a, b):
    return jnp.dot(a, b, preferred_element_type=_F32)


def _dot_nt(a, b):
    return lax.dot_general(a, b, (((1,), (1,)), ((), ())), preferred_element_type=_F32)


def _dot_tn(a, b):
    return lax.dot_general(a, b, (((0,), (0,)), ((), ())), preferred_element_type=_F32)


def _silu(x):
    return x * (1.0 / (1.0 + jnp.exp(-x)))


def _gelu_tanh(x):
    c = float(np.sqrt(2.0 / np.pi))
    half = 0.5 * x
    return half + half * jnp.tanh(x * (c + (0.044715 * c) * (x * x)))


def _rms(x):
    return x * lax.rsqrt(jnp.mean(x * x, axis=-1, keepdims=True) + RMS_EPS)


def _head(a, h):
    return a[:, h * HEAD_DIM:(h + 1) * HEAD_DIM]


def _prep_kernel(c_ref, cctx_ref, wmod_ref, bmod_ref, win_ref, lf_ref, lb_ref, mod_ref, w16_ref, tab_ref):
    j = pl.program_id(0)

    n_batch, d = c_ref.shape
    cc = jnp.concatenate([c_ref[...], cctx_ref[...], jnp.zeros((MOD_ROWS - n_batch - 1, d), _F32)], axis=0)
    mod = _dot(_silu(cc), wmod_ref[...]) + bmod_ref[...]
    for r in range(MOD_ROWS):
        mod_ref[r] = mod[r:r + 1, :]

    def reorder_head_dims(t):
        quarter = HEAD_DIM // 4
        lane = lax.broadcasted_iota(jnp.int32, (t.shape[0], HEAD_DIM), 1)
        heads = []
        for h in range(t.shape[1] // HEAD_DIM):
            th = _head(t, h)
            from_right = pltpu.roll(th, HEAD_DIM - quarter, 1)
            from_left = pltpu.roll(th, quarter, 1)
            heads.append(jnp.where((lane >= quarter) & (lane < 2 * quarter), from_right,
                                   jnp.where((lane >= 2 * quarter) & (lane < 3 * quarter), from_left, th)))
        return jnp.concatenate(heads, axis=-1)

    is_qk = (j == Q_BLOCK) | (j == K_BLOCK)

    @pl.when(is_qk)
    def _():
        w16_ref[...] = reorder_head_dims(win_ref[...]).astype(_BF16)

    @pl.when(jnp.logical_not(is_qk))
    def _():
        w16_ref[...] = win_ref[...].astype(_BF16)

    @pl.when(j == 0)
    def _():
        def log_sigmoid(x):
            return -(jnp.maximum(-x, 0.0) + jnp.log1p(jnp.exp(-jnp.abs(x))))

        shape = (CHUNK, HEADS * HEAD_DIM)
        head = lax.broadcasted_iota(jnp.int32, shape, 1) // HEAD_DIM

        def per_head(logit_ref):
            t = jnp.zeros(shape, _F32)
            for h in range(HEADS):
                t = jnp.where(head == h, logit_ref[h], t)
            return t

        lg_f = log_sigmoid(per_head(lf_ref))
        lg_b = log_sigmoid(per_head(lb_ref))
        i = lax.broadcasted_iota(jnp.int32, shape, 0).astype(_F32)
        jj = (lax.broadcasted_iota(jnp.int32, shape, 1) & (HEAD_DIM - 1)).astype(_F32)
        tab_ref[T_DMASK_F] = jnp.where(i >= jj, jnp.exp(lg_f * jnp.maximum(i - jj, 0.0)), 0.0)
        tab_ref[T_DMASK_B] = jnp.where(jj >= i, jnp.exp(lg_b * jnp.maximum(jj - i, 0.0)), 0.0)
        tab_ref[T_XI_F] = jnp.exp(lg_f * (i + 1.0))
        tab_ref[T_XI_B] = jnp.exp(lg_b * (CHUNK - i))
        tab_ref[T_ZETA_F] = jnp.exp(lg_f * (CHUNK - 1.0 - i))
        tab_ref[T_ZETA_B] = jnp.exp(lg_b * i)
        tab_ref[T_DECAY_F] = jnp.exp(lg_f * CHUNK)
        tab_ref[T_DECAY_B] = jnp.exp(lg_b * CHUNK)


def _prepare(c, c_ctx, w_mod, b_mod, w_in, logit_f, logit_b):
    d, n_mod = w_mod.shape
    r_width = HEADS * HEAD_DIM
    n_mod_blocks = n_mod // d
    assert w_in.shape == (d, IN_BLOCKS * r_width) and n_mod_blocks <= IN_BLOCKS
    mod_blk = lambda j: jnp.minimum(j, n_mod_blocks - 1)
    return pl.pallas_call(
        _prep_kernel,
        grid=(IN_BLOCKS,),
        in_specs=[pl.BlockSpec(c.shape, lambda j: (0, 0)),
                  pl.BlockSpec((1, d), lambda j: (0, 0)),
                  pl.BlockSpec((d, d), lambda j: (0, mod_blk(j))),
                  pl.BlockSpec((1, d), lambda j: (0, mod_blk(j))),
                  pl.BlockSpec((d, r_width), lambda j: (0, j)),
                  pl.BlockSpec(memory_space=pltpu.SMEM),
                  pl.BlockSpec(memory_space=pltpu.SMEM)],
        out_specs=[pl.BlockSpec((MOD_ROWS, 1, d), lambda j: (0, 0, mod_blk(j))),
                   pl.BlockSpec((d, r_width), lambda j: (0, j)),
                   pl.BlockSpec((8, CHUNK, r_width), lambda j: (0, 0, 0))],
        out_shape=[jax.ShapeDtypeStruct((MOD_ROWS, 1, n_mod), _F32),
                   jax.ShapeDtypeStruct(w_in.shape, _BF16),
                   jax.ShapeDtypeStruct((8, CHUNK, r_width), _F32)],
        compiler_params=pltpu.CompilerParams(dimension_semantics=("arbitrary",),
                                             vmem_limit_bytes=VMEM_LIMIT_BYTES),
        name="prep",
    )(c, c_ctx[None, :], w_mod, b_mod, w_in, logit_f.astype(_F32), logit_b.astype(_F32))


def _context_states(ctx, shift, scale, gain, w_k, w_v, tab_ref, k_scale):
    hc = ((_rms(ctx) * gain) * (1.0 + scale) + shift).astype(_BF16)
    k = (_dot(hc, w_k) * k_scale).astype(_BF16)
    v = _dot(hc, w_v)
    n_chunks = ctx.shape[0] // CHUNK
    s_fwd, s_bwd = [], []
    for h in range(HEADS):
        s_f = jnp.zeros((HEAD_DIM, HEAD_DIM), _F32)
        s_b = jnp.zeros((HEAD_DIM, HEAD_DIM), _F32)
        for n in range(n_chunks):
            rows = slice(n * CHUNK, (n + 1) * CHUNK)
            upd = _dot_tn(_head(k[rows], h), (_head(v[rows], h) * _head(tab_ref[T_ZETA_F], h)).astype(_BF16))
            s_f = _head(tab_ref[T_DECAY_F], h) * s_f + upd
        for n in reversed(range(n_chunks)):
            rows = slice(n * CHUNK, (n + 1) * CHUNK)
            upd = _dot_tn(_head(k[rows], h), (_head(v[rows], h) * _head(tab_ref[T_ZETA_B], h)).astype(_BF16))
            s_b = _head(tab_ref[T_DECAY_B], h) * s_b + upd
        s_fwd.append(s_f)
        s_bwd.append(s_b)
    return s_fwd, s_bwd


def _retention_tile(load, chunk_order, state, tab_ref, t_dmask, t_decay, emit):
    keys = [(n, h) for n in chunk_order for h in range(HEADS)]
    q, k, xq, v, vz, gate = {}, {}, {}, {}, {}, {}
    for n in chunk_order:
        q[n], k[n], xq[n], v[n], vz[n], gate[n] = load(n)
    scores = {(n, h): _dot_nt(_head(q[n], h), _head(k[n], h)) for n, h in keys}
    upd = {(n, h): _dot_tn(_head(k[n], h), _head(vz[n], h)) for n, h in keys}
    yield
    lhs, rhs = {}, {}
    for n, h in keys:
        p = (scores[n, h] * _head(tab_ref[t_dmask], h)).astype(_BF16)
        lhs[n, h] = jnp.concatenate([p, _head(xq[n], h)], axis=1)
        rhs[n, h] = jnp.concatenate([_head(v[n], h), state[h].astype(_BF16)], axis=0)
        state[h] = _head(tab_ref[t_decay], h) * state[h] + upd[n, h]
    yield
    o = {key: _dot(lhs[key], rhs[key]) for key in keys}
    yield
    for n in chunk_order:
        emit(n, jnp.concatenate([_silu(_head(gate[n], h)) * _rms(o[n, h]) for h in range(HEADS)], axis=-1))


def _retention_stages(load, chunk_order, n_groups, s_ref, tab_ref, t_dmask, t_decay, emit, out):
    state = [None] * HEADS
    order = list(chunk_order)
    per = len(order) // n_groups
    groups = [order[i * per:(i + 1) * per] for i in range(n_groups)]

    def group_stage(i):
        if i:
            for _ in range(i):
                yield
        inner = _retention_tile(load, groups[i], state, tab_ref, t_dmask, t_decay, emit)
        next(inner)
        yield
        if i == 0:
            for h in range(HEADS):
                state[h] = s_ref[h]
        next(inner)
        yield
        for _ in inner:
            yield
        if i == n_groups - 1:
            for h in range(HEADS):
                out.add(s_ref, h, state[h])

    return [group_stage(i) for i in range(n_groups)]


def _interleave(*stages):
    stages = list(stages)
    while stages:
        for g in list(stages):
            try:
                next(g)
            except StopIteration:
                stages.remove(g)


class _DeferredStores:
    def __init__(self):
        self._items = []

    def add(self, ref, idx, value):
        self._items.append((ref, idx, value))

    def commit(self):
        for ref, idx, value in self._items:
            ref[idx] = value


def _skewed_steps(first, middle, last):
    s = pl.program_id(1)
    n_tiles = pl.num_programs(1) - 1

    @pl.when(s == 0)
    def _():
        first(0)

    @pl.when((s > 0) & (s < n_tiles))
    def _():
        middle(s & 1)

    @pl.when(s == n_tiles)
    def _():
        last((s - 1) & 1)


def _fwd_kernel(x_ref, sh_ref, sc_ref, n1_ref, wuv_ref, wqk_ref, wvr_ref, wgf_ref, wgb_ref, gain_ref, sgw_ref, sgb_ref,
                rope_ref, tab_ref, ctx_ref, csh_ref, csc_ref, wo32_ref, wg32_ref, wu32_ref, wd32_ref,
                ya_ref, qkvg_ref, scb_ref, wo16_ref, wg16_ref, wu16_ref, wd16_ref, s_ref, f32_ref, b16_ref, *,
                n_chunks, k_scale):
    a_width = GROUPS * LANES
    r_width = HEADS * HEAD_DIM
    U, GF = 0, a_width
    VG, Q, K, XQ, VB, VZ = (i * r_width for i in range(6))
    rows_all = slice(None)

    def convert(out):
        for src, dst in ((wo32_ref, wo16_ref), (wg32_ref, wg16_ref), (wu32_ref, wu16_ref), (wd32_ref, wd16_ref)):
            out.add(dst, (rows_all, rows_all), src[...].astype(_BF16))

    def project(slot, out):
        n_sub = n_chunks // PROJ_CHUNKS
        for blk in range(n_sub):
            rows = slice(blk * PROJ_CHUNKS * CHUNK, (blk + 1) * PROJ_CHUNKS * CHUNK)

            def per_chunk(t, table):
                return jnp.concatenate([t[n * CHUNK:(n + 1) * CHUNK] * table for n in range(PROJ_CHUNKS)], axis=0)

            x = x_ref[rows, :]
            r = lax.rsqrt(jnp.mean(x * x, axis=-1, keepdims=True) + RMS_EPS)
            hx = ((x * r) * (n1_ref[...] * (1.0 + sc_ref[...])) + sh_ref[...]).astype(_BF16)

            def proj(w_ref, lo, width):
                return _dot(hx, w_ref[:, lo:lo + width])

            yield
            out.add(f32_ref, (slot, rows, slice(U, U + a_width)), _gelu_tanh(proj(wuv_ref, 0, a_width)))
            v = _gelu_tanh(proj(wuv_ref, a_width, a_width))
            vg = jnp.concatenate([(_rms(_head(v, g)) * _head(gain_ref[...], g)).astype(_BF16)
                                  for g in range(GROUPS)], axis=-1)
            out.add(b16_ref, (slot, rows, slice(VG, VG + a_width)), vg)
            yield

            def rope(t, table):
                cos = rope_ref[rows, 2 * table * HEAD_DIM:(2 * table + 1) * HEAD_DIM]
                sin = rope_ref[rows, (2 * table + 1) * HEAD_DIM:(2 * table + 2) * HEAD_DIM]
                return jnp.concatenate(
                    [_head(t, h) * cos + pltpu.roll(_head(t, h), HEAD_DIM // 2, 1) * sin for h in range(HEADS)],
                    axis=-1)

            q = rope(proj(wqk_ref, 0, r_width), 0)
            q16 = q.astype(_BF16)
            k16 = rope(proj(wqk_ref, r_width, r_width), 1).astype(_BF16)
            out.add(b16_ref, (slot, rows, slice(Q, Q + r_width)), q16)
            out.add(b16_ref, (slot, rows, slice(K, K + r_width)), k16)
            out.add(b16_ref, (slot, rows, slice(XQ, XQ + r_width)), per_chunk(q, tab_ref[T_XI_F]).astype(_BF16))
            out.add(qkvg_ref, (rows, slice(0, r_width)), q16)
            out.add(qkvg_ref, (rows, slice(r_width, 2 * r_width)), k16)
            yield
            vr = proj(wvr_ref, 0, r_width)
            v16 = vr.astype(_BF16)
            out.add(b16_ref, (slot, rows, slice(VB, VB + r_width)), v16)
            out.add(b16_ref, (slot, rows, slice(VZ, VZ + r_width)), per_chunk(vr, tab_ref[T_ZETA_F]).astype(_BF16))
            out.add(f32_ref, (slot, rows, slice(GF, GF + r_width)), proj(wgf_ref, 0, r_width))
            out.add(qkvg_ref, (rows, slice(2 * r_width, 3 * r_width)), v16)
            out.add(qkvg_ref, (rows, slice(3 * r_width, 4 * r_width)),
                    proj(wgb_ref, 0, r_width).astype(_BF16))

    def gate_mix(slot, out):
        mixed = []
        for g in range(GROUPS):
            vg = jnp.concatenate([b16_ref[slot, n * CHUNK:(n + 1) * CHUNK, VG + g * LANES:VG + (g + 1) * LANES]
                                  for n in range(n_chunks)], axis=1)
            mixed.append(_dot(sgw_ref[g].astype(_BF16), vg))
        yield
        for g in range(GROUPS):
            cols = slice(g * LANES, (g + 1) * LANES)
            for n in range(n_chunks):
                rows = slice(n * CHUNK, (n + 1) * CHUNK)
                m = mixed[g][:, n * LANES:(n + 1) * LANES] + sgb_ref[:, cols]
                u = f32_ref[slot, rows, U + g * LANES:U + (g + 1) * LANES]
                out.add(ya_ref, (rows, cols), (u * m).astype(_BF16))

    def retain(slot, out):
        def load(n):
            rows = slice(n * CHUNK, (n + 1) * CHUNK)
            return tuple(b16_ref[slot, rows, c:c + r_width] for c in (Q, K, XQ, VB, VZ)) + (
                f32_ref[slot, rows, GF:GF + r_width],)

        def emit(n, y):
            out.add(ya_ref, (slice(n * CHUNK, (n + 1) * CHUNK), slice(a_width, a_width + r_width)), y.astype(_BF16))

        return _retention_stages(load, range(n_chunks), FWD_RETAIN_GROUPS, s_ref, tab_ref, T_DMASK_F, T_DECAY_F,
                                 emit, out)

    def first(slot):
        s_fwd, s_bwd = _context_states(ctx_ref[...], csh_ref[...], csc_ref[...], n1_ref[...],
                                       wqk_ref[:, r_width:], wvr_ref[...], tab_ref, k_scale)
        for h in range(HEADS):
            s_ref[h] = s_fwd[h]
            scb_ref[h] = s_bwd[h]
        out = _DeferredStores()
        convert(out)
        _interleave(project(slot, out))
        out.commit()

    def middle(slot):
        out = _DeferredStores()
        convert(out)
        _interleave(*retain(1 - slot, out), gate_mix(1 - slot, out), project(slot, out))
        out.commit()

    def last(slot):
        out = _DeferredStores()
        convert(out)
        _interleave(*retain(slot, out), gate_mix(slot, out))
        out.commit()

    _skewed_steps(first, middle, last)


def _forward_pass(x, ctx, mod3, norm1, w16, sg_gain, sg_w, sg_bias, rope, tables, bwd_weights, tile, k_scale):
    b, l, d = x.shape
    nt = l // tile
    a_width, r_width = GROUPS * LANES, HEADS * HEAD_DIM
    ctx_row = b
    state_spec = pl.BlockSpec((None, HEADS, HEAD_DIM, HEAD_DIM), lambda i, s: (i, 0, 0, 0))
    const2 = lambda i, s: (0, 0)
    cur = lambda i, s: (i, jnp.minimum(s, nt - 1), 0)
    prev = lambda i, s: (i, jnp.maximum(s - 1, 0), 0)
    resident = dict(pipeline_mode=pl.Buffered(1))
    n_blk = max(n for n in (1, 2, 4, 8, 16) if n <= b * (nt + 1))
    blk = lambda i, s: (jnp.minimum(i * (nt + 1) + s, n_blk - 1), 0)
    w_specs = [pl.BlockSpec((w.shape[0] // n_blk, w.shape[1]), blk) for w in bwd_weights]
    return pl.pallas_call(
        functools.partial(_fwd_kernel, n_chunks=tile // CHUNK, k_scale=k_scale),
        grid=(b, nt + 1),
        in_specs=[pl.BlockSpec((None, tile, d), cur),
                  pl.BlockSpec((None, 1, d), lambda i, s: (i, 0, 0)),
                  pl.BlockSpec((None, 1, d), lambda i, s: (i, 0, 1)),
                  pl.BlockSpec((1, d), const2),
                  pl.BlockSpec((d, 2 * r_width), lambda i, s: (0, U_BLOCK // 2), **resident),
                  pl.BlockSpec((d, 2 * r_width), lambda i, s: (0, Q_BLOCK // 2), **resident),
                  pl.BlockSpec((d, r_width), lambda i, s: (0, VR_BLOCK), **resident),
                  pl.BlockSpec((d, r_width), lambda i, s: (0, GF_BLOCK), **resident),
                  pl.BlockSpec((d, r_width), lambda i, s: (0, GB_BLOCK), **resident),
                  pl.BlockSpec((1, a_width), const2),
                  pl.BlockSpec(sg_w.shape, lambda i, s: (0, 0, 0)),
                  pl.BlockSpec(sg_bias.shape, const2),
                  pl.BlockSpec((tile, rope.shape[1]), lambda i, s: (jnp.minimum(s, nt - 1), 0)),
                  pl.BlockSpec(tables.shape, lambda i, s: (0, 0, 0)),
                  pl.BlockSpec((None,) + ctx.shape[1:], lambda i, s: (i, 0, 0)),
                  pl.BlockSpec((None, 1, d), lambda i, s: (ctx_row, 0, 0)),
                  pl.BlockSpec((None, 1, d), lambda i, s: (ctx_row, 0, 1))]
                 + w_specs,
        out_specs=[pl.BlockSpec((None, tile, a_width + r_width), prev),
                   pl.BlockSpec((None, tile, 4 * r_width), cur), state_spec] + w_specs,
        out_shape=[jax.ShapeDtypeStruct((b, l, a_width + r_width), _BF16),
                   jax.ShapeDtypeStruct((b, l, 4 * r_width), _BF16),
                   jax.ShapeDtypeStruct((b, HEADS, HEAD_DIM, HEAD_DIM), _F32)]
                  + [jax.ShapeDtypeStruct(w.shape, _BF16) for w in bwd_weights],
        scratch_shapes=[pltpu.VMEM((HEADS, HEAD_DIM, HEAD_DIM), _F32),
                        pltpu.VMEM((2, tile, a_width + r_width), _F32),
                        pltpu.VMEM((2, tile, 6 * r_width), _BF16)],
        compiler_params=pltpu.CompilerParams(dimension_semantics=("arbitrary", "arbitrary"),
                                             vmem_limit_bytes=VMEM_LIMIT_BYTES),
        name="fwd",
    )(x, mod3, mod3, norm1, w16, w16, w16, w16, w16, sg_gain, sg_w, sg_bias, rope, tables, ctx, mod3, mod3,
      *bwd_weights)


def _bwd_kernel(x_ref, ya_ref, qkvg_ref, g1_ref, sh_ref, sc_ref, g2_ref, n2_ref, nf_ref,
                wout_ref, wg_ref, wu_ref, wd_ref, tab_ref, s0_ref, o_ref, s_ref, yb_ref, *, n_chunks):
    a_width = GROUPS * LANES
    r_width = HEADS * HEAD_DIM

    def retain(slot, out):
        def load(n):
            rows = slice(n * CHUNK, (n + 1) * CHUNK)
            q, k, v, gate = (qkvg_ref[rows, i * r_width:(i + 1) * r_width] for i in range(4))
            return (q, k, (q.astype(_F32) * tab_ref[T_XI_B]).astype(_BF16),
                    v, (v.astype(_F32) * tab_ref[T_ZETA_B]).astype(_BF16), gate.astype(_F32))

        def emit(n, y):
            out.add(yb_ref, (slot, slice(n * CHUNK, (n + 1) * CHUNK), slice(None)), y)

        return _retention_stages(load, list(reversed(range(n_chunks))), BWD_RETAIN_GROUPS, s_ref, tab_ref,
                                 T_DMASK_B, T_DECAY_B, emit, out)

    def finish(slot, out):
        y_r = ya_ref[:, a_width:].astype(_F32) + yb_ref[slot]
        y = _dot(ya_ref[:, :a_width], wout_ref[:a_width, :]) + _dot(y_r.astype(_BF16), wout_ref[a_width:, :])
        x1 = x_ref[...] + g1_ref[...] * y
        h2 = ((_rms(x1) * n2_ref[...]) * (1.0 + sc_ref[...]) + sh_ref[...]).astype(_BF16)
        acc = None
        d_ff = wg_ref.shape[1]
        for lo in range(0, d_ff, FFN_BLOCK):
            yield
            cols = slice(lo, min(lo + FFN_BLOCK, d_ff))
            act = (_silu(_dot(h2, wg_ref[:, cols])) * _dot(h2, wu_ref[:, cols])).astype(_BF16)
            part = _dot(act, wd_ref[cols, :])
            acc = part if acc is None else acc + part
        x2 = x1 + g2_ref[...] * acc
        out.add(o_ref, (slice(None), slice(None)), _rms(x2) * nf_ref[...])

    def first(slot):
        s_ref[...] = s0_ref[...]
        out = _DeferredStores()
        _interleave(*retain(slot, out))
        out.commit()

    def middle(slot):
        out = _DeferredStores()
        _interleave(*retain(slot, out), finish(1 - slot, out))
        out.commit()

    def last(slot):
        out = _DeferredStores()
        _interleave(finish(slot, out))
        out.commit()

    _skewed_steps(first, middle, last)


def _backward_pass(x, ya, qkvg, mod3, norm2, norm_f, w_out, w_gate, w_up, w_down, tables, s_cb, tile):
    b, l, d = x.shape
    nt = l // tile
    const2 = lambda i, s: (0, 0)
    cur = lambda i, s: (i, nt - 1 - jnp.minimum(s, nt - 1), 0)
    prev = lambda i, s: (i, nt - 1 - jnp.maximum(s - 1, 0), 0)
    resident = dict(pipeline_mode=pl.Buffered(1))
    mod_spec = lambda j: pl.BlockSpec((None, 1, d), lambda i, s: (i, 0, j))
    return pl.pallas_call(
        functools.partial(_bwd_kernel, n_chunks=tile // CHUNK),
        grid=(b, nt + 1),
        in_specs=[pl.BlockSpec((None, tile, d), prev),
                  pl.BlockSpec((None, tile, ya.shape[-1]), prev),
                  pl.BlockSpec((None, tile, qkvg.shape[-1]), cur),
                  mod_spec(2), mod_spec(3), mod_spec(4), mod_spec(5),
                  pl.BlockSpec((1, d), const2),
                  pl.BlockSpec((1, d), const2),
                  pl.BlockSpec(w_out.shape, const2, **resident),
                  pl.BlockSpec(w_gate.shape, const2, **resident),
                  pl.BlockSpec(w_up.shape, const2, **resident),
                  pl.BlockSpec(w_down.shape, const2, **resident),
                  pl.BlockSpec(tables.shape, lambda i, s: (0, 0, 0)),
                  pl.BlockSpec((None, HEADS, HEAD_DIM, HEAD_DIM), lambda i, s: (i, 0, 0, 0))],
        out_specs=pl.BlockSpec((None, tile, d), prev),
        out_shape=jax.ShapeDtypeStruct((b, l, d), x.dtype),
        scratch_shapes=[pltpu.VMEM((HEADS, HEAD_DIM, HEAD_DIM), _F32),
                        pltpu.VMEM((2, tile, HEADS * HEAD_DIM), _F32)],
        compiler_params=pltpu.CompilerParams(dimension_semantics=("arbitrary", "arbitrary"),
                                             vmem_limit_bytes=VMEM_LIMIT_BYTES),
        name="bwd",
    )(x, ya, qkvg, mod3, mod3, mod3, mod3, norm2, norm_f, w_out, w_gate, w_up, w_down, tables, s_cb)


def _rope_tables(l, k_scale):
    n_freq = HEAD_DIM // 4
    inv = np.float32(ROPE_BASE) ** (-np.arange(n_freq, dtype=np.float32) / np.float32(n_freq))
    rows = np.repeat(np.arange(l // GRID_W, dtype=np.float32), GRID_W)
    cols = np.tile(np.arange(GRID_W, dtype=np.float32), l // GRID_W)
    ang_r, ang_c = rows[:, None] * inv[None, :], cols[:, None] * inv[None, :]
    cos = np.concatenate([np.cos(ang_r), np.cos(ang_c)], axis=-1)
    sin = np.concatenate([np.sin(ang_r), np.sin(ang_c)], axis=-1)
    cos2, sin2 = np.concatenate([cos, cos], axis=-1), np.concatenate([-sin, sin], axis=-1)
    return np.concatenate([cos2, sin2, cos2 * k_scale, sin2 * k_scale], axis=-1).astype(np.float32)


def kernel(x, c, ctx, c_ctx, w_mod, b_mod, norm1, w_in, sg_gain, sg_w, sg_b, ret_logit_f, ret_logit_b,
           w_out, norm2, w_gate, w_up, w_down, norm_f):
    b, l, d = x.shape
    a_width, r_width = GROUPS * LANES, HEADS * HEAD_DIM
    tile = min(TILE, l)
    assert w_mod.shape[0] == 1 and b + 1 <= MOD_ROWS and l % tile == 0 and tile % CHUNK == 0
    assert ctx.shape[1] % CHUNK == 0 and l // tile >= 2
    assert a_width == r_width and d == a_width + r_width
    k_scale = HEAD_DIM ** -0.5

    mod3, w16, tables = _prepare(c, c_ctx, w_mod[0], b_mod[0][None, :], w_in[0], ret_logit_f[0], ret_logit_b[0])
    sg_bias = jnp.broadcast_to(jnp.transpose(sg_b[0])[:, :, None], (CHUNK, GROUPS, LANES)).reshape(CHUNK, a_width)
    ya, qkvg, s_cb, w_out_b, w_gate_b, w_up_b, w_down_b = _forward_pass(
        x, ctx, mod3, norm1, w16, sg_gain, sg_w[0], sg_bias, _rope_tables(l, k_scale),
        tables, (w_out[0], w_gate[0], w_up[0], w_down[0]), tile, k_scale)
    return _backward_pass(x, ya, qkvg, mod3, norm2, norm_f[None, :], w_out_b, w_gate_b, w_up_b, w_down_b,
                          tables, s_cb, tile)
```

```python
import functools

import numpy as np
import jax
import jax.numpy as jnp
from jax import lax
from jax.experimental import pallas as pl
from jax.experimental.pallas import tpu as pltpu

CHUNK = 128
GRID_W = 64
GROUPS = 4
HEADS = 4
HEAD_DIM = 128
RMS_EPS = 1e-6
ROPE_BASE = 10000.0
LANES = 128
MOD_ROWS = 8
TILE = 512
PROJ_CHUNKS = 2
BWD_RETAIN_GROUPS = 2
FFN_BLOCK = 1024
VMEM_LIMIT_BYTES = 56 * 1024 * 1024

IN_BLOCKS = 7
U_BLOCK, V_BLOCK, Q_BLOCK, K_BLOCK, VR_BLOCK, GF_BLOCK, GB_BLOCK = range(IN_BLOCKS)

T_DMASK_F, T_DMASK_B, T_XI_F, T_XI_B, T_ZETA_F, T_ZETA_B, T_DECAY_F, T_DECAY_B = range(8)

_BF16 = jnp.bfloat16
_F32 = jnp.float32


def _dot(a, b):
    return jnp.dot(a, b, preferred_element_type=_F32)


def _dot_nt(a, b):
    return lax.dot_general(a, b, (((1,), (1,)), ((), ())), preferred_element_type=_F32)


def _dot_tn(a, b):
    return lax.dot_general(a, b, (((0,), (0,)), ((), ())), preferred_element_type=_F32)


def _silu(x):
    return x * (1.0 / (1.0 + jnp.exp(-x)))


def _gelu_tanh(x):
    c = float(np.sqrt(2.0 / np.pi))
    half = 0.5 * x
    return half + half * jnp.tanh(x * (c + (0.044715 * c) * (x * x)))


def _rms(x):
    return x * lax.rsqrt(jnp.mean(x * x, axis=-1, keepdims=True) + RMS_EPS)


def _head(a, h):
    return a[:, h * HEAD_DIM:(h + 1) * HEAD_DIM]


def _prep_kernel(c_ref, cctx_ref, wmod_ref, bmod_ref, win_ref, lf_ref, lb_ref, mod_ref, w16_ref, tab_ref):
    j = pl.program_id(0)

    n_batch, d = c_ref.shape
    cc = jnp.concatenate([c_ref[...], cctx_ref[...], jnp.zeros((MOD_ROWS - n_batch - 1, d), _F32)], axis=0)
    mod = _dot(_silu(cc), wmod_ref[...]) + bmod_ref[...]
    for r in range(MOD_ROWS):
        mod_ref[r] = mod[r:r + 1, :]

    def reorder_head_dims(t):
        quarter = HEAD_DIM // 4
        lane = lax.broadcasted_iota(jnp.int32, (t.shape[0], HEAD_DIM), 1)
        heads = []
        for h in range(t.shape[1] // HEAD_DIM):
            th = _head(t, h)
            from_right = pltpu.roll(th, HEAD_DIM - quarter, 1)
            from_left = pltpu.roll(th, quarter, 1)
            heads.append(jnp.where((lane >= quarter) & (lane < 2 * quarter), from_right,
                                   jnp.where((lane >= 2 * quarter) & (lane < 3 * quarter), from_left, th)))
        return jnp.concatenate(heads, axis=-1)

    is_qk = (j == Q_BLOCK) | (j == K_BLOCK)

    @pl.when(is_qk)
    def _():
        w16_ref[...] = reorder_head_dims(win_ref[...]).astype(_BF16)

    @pl.when(jnp.logical_not(is_qk))
    def _():
        w16_ref[...] = win_ref[...].astype(_BF16)

    @pl.when(j == 0)
    def _():
        def log_sigmoid(x):
            return -(jnp.maximum(-x, 0.0) + jnp.log1p(jnp.exp(-jnp.abs(x))))

        shape = (CHUNK, HEADS * HEAD_DIM)
        head = lax.broadcasted_iota(jnp.int32, shape, 1) // HEAD_DIM

        def per_head(logit_ref):
            t = jnp.zeros(shape, _F32)
            for h in range(HEADS):
                t = jnp.where(head == h, logit_ref[h], t)
            return t

        lg_f = log_sigmoid(per_head(lf_ref))
        lg_b = log_sigmoid(per_head(lb_ref))
        i = lax.broadcasted_iota(jnp.int32, shape, 0).astype(_F32)
        jj = (lax.broadcasted_iota(jnp.int32, shape, 1) & (HEAD_DIM - 1)).astype(_F32)
        tab_ref[T_DMASK_F] = jnp.where(i >= jj, jnp.exp(lg_f * jnp.maximum(i - jj, 0.0)), 0.0)
        tab_ref[T_DMASK_B] = jnp.where(jj >= i, jnp.exp(lg_b * jnp.maximum(jj - i, 0.0)), 0.0)
        tab_ref[T_XI_F] = jnp.exp(lg_f * (i + 1.0))
        tab_ref[T_XI_B] = jnp.exp(lg_b * (CHUNK - i))
        tab_ref[T_ZETA_F] = jnp.exp(lg_f * (CHUNK - 1.0 - i))
        tab_ref[T_ZETA_B] = jnp.exp(lg_b * i)
        tab_ref[T_DECAY_F] = jnp.exp(lg_f * CHUNK)
        tab_ref[T_DECAY_B] = jnp.exp(lg_b * CHUNK)


def _prepare(c, c_ctx, w_mod, b_mod, w_in, logit_f, logit_b):
    d, n_mod = w_mod.shape
    r_width = HEADS * HEAD_DIM
    n_mod_blocks = n_mod // d
    assert w_in.shape == (d, IN_BLOCKS * r_width) and n_mod_blocks <= IN_BLOCKS
    mod_blk = lambda j: jnp.minimum(j, n_mod_blocks - 1)
    return pl.pallas_call(
        _prep_kernel,
        grid=(IN_BLOCKS,),
        in_specs=[pl.BlockSpec(c.shape, lambda j: (0, 0)),
                  pl.BlockSpec((1, d), lambda j: (0, 0)),
                  pl.BlockSpec((d, d), lambda j: (0, mod_blk(j))),
                  pl.BlockSpec((1, d), lambda j: (0, mod_blk(j))),
                  pl.BlockSpec((d, r_width), lambda j: (0, j)),
                  pl.BlockSpec(memory_space=pltpu.SMEM),
                  pl.BlockSpec(memory_space=pltpu.SMEM)],
        out_specs=[pl.BlockSpec((MOD_ROWS, 1, d), lambda j: (0, 0, mod_blk(j))),
                   pl.BlockSpec((d, r_width), lambda j: (0, j)),
                   pl.BlockSpec((8, CHUNK, r_width), lambda j: (0, 0, 0))],
        out_shape=[jax.ShapeDtypeStruct((MOD_ROWS, 1, n_mod), _F32),
                   jax.ShapeDtypeStruct(w_in.shape, _BF16),
                   jax.ShapeDtypeStruct((8, CHUNK, r_width), _F32)],
        compiler_params=pltpu.CompilerParams(dimension_semantics=("arbitrary",),
                                             vmem_limit_bytes=VMEM_LIMIT_BYTES),
        name="prep",
    )(c, c_ctx[None, :], w_mod, b_mod, w_in, logit_f.astype(_F32), logit_b.astype(_F32))


def _context_states(ctx, shift, scale, gain, w_k, w_v, tab_ref, k_scale):
    hc = ((_rms(ctx) * gain) * (1.0 + scale) + shift).astype(_BF16)
    k = (_dot(hc, w_k) * k_scale).astype(_BF16)
    v = _dot(hc, w_v)
    n_chunks = ctx.shape[0] // CHUNK
    s_fwd, s_bwd = [], []
    for h in range(HEADS):
        s_f = jnp.zeros((HEAD_DIM, HEAD_DIM), _F32)
        s_b = jnp.zeros((HEAD_DIM, HEAD_DIM), _F32)
        for n in range(n_chunks):
            rows = slice(n * CHUNK, (n + 1) * CHUNK)
            upd = _dot_tn(_head(k[rows], h), (_head(v[rows], h) * _head(tab_ref[T_ZETA_F], h)).astype(_BF16))
            s_f = _head(tab_ref[T_DECAY_F], h) * s_f + upd
        for n in reversed(range(n_chunks)):
            rows = slice(n * CHUNK, (n + 1) * CHUNK)
            upd = _dot_tn(_head(k[rows], h), (_head(v[rows], h) * _head(tab_ref[T_ZETA_B], h)).astype(_BF16))
            s_b = _head(tab_ref[T_DECAY_B], h) * s_b + upd
        s_fwd.append(s_f)
        s_bwd.append(s_b)
    return s_fwd, s_bwd


def _retention_tile(load, chunk_order, state, tab_ref, emit):
    keys = [(n, h) for n in chunk_order for h in range(HEADS)]
    c = {n: load(n) for n in chunk_order}
    scores = {(n, h): _dot_nt(_head(c[n]["q"], h), _head(c[n]["k"], h)) for n, h in keys}
    upd = {(n, h): _dot_tn(_head(c[n]["k"], h), _head(c[n]["vz_b"], h)) for n, h in keys}
    yield
    lhs_f, lhs_b, rhs_f, rhs_b = {}, {}, {}, {}
    for n, h in keys:
        v = _head(c[n]["v"], h)
        lhs_f[n, h] = jnp.concatenate([(scores[n, h] * _head(tab_ref[T_DMASK_F], h)).astype(_BF16),
                                       _head(c[n]["xq_f"], h)], axis=1)
        lhs_b[n, h] = jnp.concatenate([(scores[n, h] * _head(tab_ref[T_DMASK_B], h)).astype(_BF16),
                                       _head(c[n]["xq_b"], h)], axis=1)
        rhs_f[n, h] = jnp.concatenate([v, c[n]["s_f"][h]], axis=0)
        rhs_b[n, h] = jnp.concatenate([v, state[h].astype(_BF16)], axis=0)
        state[h] = _head(tab_ref[T_DECAY_B], h) * state[h] + upd[n, h]
    yield
    o_f = {key: _dot(lhs_f[key], rhs_f[key]) for key in keys}
    o_b = {key: _dot(lhs_b[key], rhs_b[key]) for key in keys}
    yield
    for n in chunk_order:
        emit(n, jnp.concatenate([_silu(_head(c[n]["gate_f"], h)) * _rms(o_f[n, h])
                                 + _silu(_head(c[n]["gate_b"], h)) * _rms(o_b[n, h]) for h in range(HEADS)],
                                axis=-1))


def _retention_stages(load, chunk_order, n_groups, s_ref, tab_ref, emit, out):
    state = [None] * HEADS
    order = list(chunk_order)
    per = len(order) // n_groups
    groups = [order[i * per:(i + 1) * per] for i in range(n_groups)]

    def group_stage(i):
        if i:
            for _ in range(i):
                yield
        inner = _retention_tile(load, groups[i], state, tab_ref, emit)
        next(inner)
        yield
        if i == 0:
            for h in range(HEADS):
                state[h] = s_ref[h]
        next(inner)
        yield
        for _ in inner:
            yield
        if i == n_groups - 1:
            for h in range(HEADS):
                out.add(s_ref, h, state[h])

    return [group_stage(i) for i in range(n_groups)]


def _interleave(*stages):
    stages = list(stages)
    while stages:
        for g in list(stages):
            try:
                next(g)
            except StopIteration:
                stages.remove(g)


class _DeferredStores:
    def __init__(self):
        self._items = []

    def add(self, ref, idx, value):
        self._items.append((ref, idx, value))

    def commit(self):
        for ref, idx, value in self._items:
            ref[idx] = value


def _skewed_steps(first, middle, last):
    s = pl.program_id(1)
    n_tiles = pl.num_programs(1) - 1

    @pl.when(s == 0)
    def _():
        first(0)

    @pl.when((s > 0) & (s < n_tiles))
    def _():
        middle(s & 1)

    @pl.when(s == n_tiles)
    def _():
        last((s - 1) & 1)


def _fwd_kernel(x_ref, sh_ref, sc_ref, n1_ref, wuv_ref, wqk_ref, wvr_ref, wgf_ref, wgb_ref, gain_ref, sgw_ref, sgb_ref,
                rope_ref, tab_ref, ctx_ref, csh_ref, csc_ref, wo32_ref, wg32_ref, wu32_ref, wd32_ref,
                ya_ref, qkvg_ref, sfin_ref, scb_ref, wo16_ref, wg16_ref, wu16_ref, wd16_ref, s_ref, f32_ref, b16_ref, *,
                n_chunks, k_scale):
    a_width = GROUPS * LANES
    r_width = HEADS * HEAD_DIM
    U = 0
    VG, K, VZ = (i * r_width for i in range(3))
    rows_all = slice(None)

    def convert(out):
        for src, dst in ((wo32_ref, wo16_ref), (wg32_ref, wg16_ref), (wu32_ref, wu16_ref), (wd32_ref, wd16_ref)):
            out.add(dst, (rows_all, rows_all), src[...].astype(_BF16))

    def project(slot, out):
        n_sub = n_chunks // PROJ_CHUNKS
        for blk in range(n_sub):
            rows = slice(blk * PROJ_CHUNKS * CHUNK, (blk + 1) * PROJ_CHUNKS * CHUNK)

            def per_chunk(t, table):
                return jnp.concatenate([t[n * CHUNK:(n + 1) * CHUNK] * table for n in range(PROJ_CHUNKS)], axis=0)

            x = x_ref[rows, :]
            r = lax.rsqrt(jnp.mean(x * x, axis=-1, keepdims=True) + RMS_EPS)
            hx = ((x * r) * (n1_ref[...] * (1.0 + sc_ref[...])) + sh_ref[...]).astype(_BF16)

            def proj(w_ref, lo, width):
                return _dot(hx, w_ref[:, lo:lo + width])

            yield
            out.add(f32_ref, (slot, rows, slice(U, U + a_width)), _gelu_tanh(proj(wuv_ref, 0, a_width)))
            v = _gelu_tanh(proj(wuv_ref, a_width, a_width))
            vg = jnp.concatenate([(_rms(_head(v, g)) * _head(gain_ref[...], g)).astype(_BF16)
                                  for g in range(GROUPS)], axis=-1)
            out.add(b16_ref, (slot, rows, slice(VG, VG + a_width)), vg)
            yield

            def rope(t, table):
                cos = rope_ref[rows, 2 * table * HEAD_DIM:(2 * table + 1) * HEAD_DIM]
                sin = rope_ref[rows, (2 * table + 1) * HEAD_DIM:(2 * table + 2) * HEAD_DIM]
                return jnp.concatenate(
                    [_head(t, h) * cos + pltpu.roll(_head(t, h), HEAD_DIM // 2, 1) * sin for h in range(HEADS)],
                    axis=-1)

            q16 = rope(proj(wqk_ref, 0, r_width), 0).astype(_BF16)
            k16 = rope(proj(wqk_ref, r_width, r_width), 1).astype(_BF16)
            out.add(b16_ref, (slot, rows, slice(K, K + r_width)), k16)
            out.add(qkvg_ref, (rows, slice(0, r_width)), q16)
            out.add(qkvg_ref, (rows, slice(r_width, 2 * r_width)), k16)
            yield
            vr = proj(wvr_ref, 0, r_width)
            out.add(b16_ref, (slot, rows, slice(VZ, VZ + r_width)), per_chunk(vr, tab_ref[T_ZETA_F]).astype(_BF16))
            out.add(qkvg_ref, (rows, slice(2 * r_width, 3 * r_width)), vr.astype(_BF16))
            out.add(qkvg_ref, (rows, slice(3 * r_width, 4 * r_width)), proj(wgb_ref, 0, r_width).astype(_BF16))
            out.add(qkvg_ref, (rows, slice(4 * r_width, 5 * r_width)), proj(wgf_ref, 0, r_width).astype(_BF16))

    def gate_mix(slot, out):
        mixed = []
        for g in range(GROUPS):
            vg = jnp.concatenate([b16_ref[slot, n * CHUNK:(n + 1) * CHUNK, VG + g * LANES:VG + (g + 1) * LANES]
                                  for n in range(n_chunks)], axis=1)
            mixed.append(_dot(sgw_ref[g].astype(_BF16), vg))
        yield
        for g in range(GROUPS):
            cols = slice(g * LANES, (g + 1) * LANES)
            for n in range(n_chunks):
                rows = slice(n * CHUNK, (n + 1) * CHUNK)
                m = mixed[g][:, n * LANES:(n + 1) * LANES] + sgb_ref[:, cols]
                u = f32_ref[slot, rows, U + g * LANES:U + (g + 1) * LANES]
                out.add(ya_ref, (rows, cols), (u * m).astype(_BF16))

    def scan(slot, out):
        upd = {(n, h): _dot_tn(b16_ref[slot, n * CHUNK:(n + 1) * CHUNK, K + h * HEAD_DIM:K + (h + 1) * HEAD_DIM],
                               b16_ref[slot, n * CHUNK:(n + 1) * CHUNK, VZ + h * HEAD_DIM:VZ + (h + 1) * HEAD_DIM])
               for n in range(n_chunks) for h in range(HEADS)}
        yield
        for h in range(HEADS):
            state = s_ref[h]
            for n in range(n_chunks):
                out.add(sfin_ref, (n, h), state.astype(_BF16))
                state = _head(tab_ref[T_DECAY_F], h) * state + upd[n, h]
            out.add(s_ref, h, state)

    def first(slot):
        s_fwd, s_bwd = _context_states(ctx_ref[...], csh_ref[...], csc_ref[...], n1_ref[...],
                                       wqk_ref[:, r_width:], wvr_ref[...], tab_ref, k_scale)
        for h in range(HEADS):
            s_ref[h] = s_fwd[h]
            scb_ref[h] = s_bwd[h]
        out = _DeferredStores()
        convert(out)
        _interleave(project(slot, out))
        out.commit()

    def middle(slot):
        out = _DeferredStores()
        convert(out)
        _interleave(scan(1 - slot, out), gate_mix(1 - slot, out), project(slot, out))
        out.commit()

    def last(slot):
        out = _DeferredStores()
        convert(out)
        _interleave(scan(slot, out), gate_mix(slot, out))
        out.commit()

    _skewed_steps(first, middle, last)


def _forward_pass(x, ctx, mod3, norm1, w16, sg_gain, sg_w, sg_bias, rope, tables, bwd_weights, tile, k_scale):
    b, l, d = x.shape
    nt = l // tile
    a_width, r_width = GROUPS * LANES, HEADS * HEAD_DIM
    ctx_row = b
    state_spec = pl.BlockSpec((None, HEADS, HEAD_DIM, HEAD_DIM), lambda i, s: (i, 0, 0, 0))
    const2 = lambda i, s: (0, 0)
    cur = lambda i, s: (i, jnp.minimum(s, nt - 1), 0)
    prev = lambda i, s: (i, jnp.maximum(s - 1, 0), 0)
    resident = dict(pipeline_mode=pl.Buffered(1))
    n_blk = max(n for n in (1, 2, 4, 8, 16) if n <= b * (nt + 1))
    blk = lambda i, s: (jnp.minimum(i * (nt + 1) + s, n_blk - 1), 0)
    w_specs = [pl.BlockSpec((w.shape[0] // n_blk, w.shape[1]), blk) for w in bwd_weights]
    return pl.pallas_call(
        functools.partial(_fwd_kernel, n_chunks=tile // CHUNK, k_scale=k_scale),
        grid=(b, nt + 1),
        in_specs=[pl.BlockSpec((None, tile, d), cur),
                  pl.BlockSpec((None, 1, d), lambda i, s: (i, 0, 0)),
                  pl.BlockSpec((None, 1, d), lambda i, s: (i, 0, 1)),
                  pl.BlockSpec((1, d), const2),
                  pl.BlockSpec((d, 2 * r_width), lambda i, s: (0, U_BLOCK // 2), **resident),
                  pl.BlockSpec((d, 2 * r_width), lambda i, s: (0, Q_BLOCK // 2), **resident),
                  pl.BlockSpec((d, r_width), lambda i, s: (0, VR_BLOCK), **resident),
                  pl.BlockSpec((d, r_width), lambda i, s: (0, GF_BLOCK), **resident),
                  pl.BlockSpec((d, r_width), lambda i, s: (0, GB_BLOCK), **resident),
                  pl.BlockSpec((1, a_width), const2),
                  pl.BlockSpec(sg_w.shape, lambda i, s: (0, 0, 0)),
                  pl.BlockSpec(sg_bias.shape, const2),
                  pl.BlockSpec((tile, rope.shape[1]), lambda i, s: (jnp.minimum(s, nt - 1), 0)),
                  pl.BlockSpec(tables.shape, lambda i, s: (0, 0, 0)),
                  pl.BlockSpec((None,) + ctx.shape[1:], lambda i, s: (i, 0, 0)),
                  pl.BlockSpec((None, 1, d), lambda i, s: (ctx_row, 0, 0)),
                  pl.BlockSpec((None, 1, d), lambda i, s: (ctx_row, 0, 1))]
                 + w_specs,
        out_specs=[pl.BlockSpec((None, tile, a_width), prev),
                   pl.BlockSpec((None, tile, 5 * r_width), cur),
                   pl.BlockSpec((None, tile // CHUNK, HEADS, HEAD_DIM, HEAD_DIM),
                                lambda i, s: (i, jnp.maximum(s - 1, 0), 0, 0, 0)),
                   state_spec] + w_specs,
        out_shape=[jax.ShapeDtypeStruct((b, l, a_width), _BF16),
                   jax.ShapeDtypeStruct((b, l, 5 * r_width), _BF16),
                   jax.ShapeDtypeStruct((b, l // CHUNK, HEADS, HEAD_DIM, HEAD_DIM), _BF16),
                   jax.ShapeDtypeStruct((b, HEADS, HEAD_DIM, HEAD_DIM), _F32)]
                  + [jax.ShapeDtypeStruct(w.shape, _BF16) for w in bwd_weights],
        scratch_shapes=[pltpu.VMEM((HEADS, HEAD_DIM, HEAD_DIM), _F32),
                        pltpu.VMEM((2, tile, a_width), _F32),
                        pltpu.VMEM((2, tile, 3 * r_width), _BF16)],
        compiler_params=pltpu.CompilerParams(dimension_semantics=("arbitrary", "arbitrary"),
                                             vmem_limit_bytes=VMEM_LIMIT_BYTES),
        name="fwd",
    )(x, mod3, mod3, norm1, w16, w16, w16, w16, w16, sg_gain, sg_w, sg_bias, rope, tables, ctx, mod3, mod3,
      *bwd_weights)


def _bwd_kernel(x_ref, ya_ref, qkvg_ref, sfin_ref, g1_ref, sh_ref, sc_ref, g2_ref, n2_ref, nf_ref,
                wout_ref, wg_ref, wu_ref, wd_ref, tab_ref, s0_ref, o_ref, s_ref, yb_ref, *, n_chunks):
    a_width = GROUPS * LANES
    r_width = HEADS * HEAD_DIM

    def retain(slot, out):
        def load(n):
            rows = slice(n * CHUNK, (n + 1) * CHUNK)
            q, k, v, gate_b, gate_f = (qkvg_ref[rows, i * r_width:(i + 1) * r_width] for i in range(5))
            q32, v32 = q.astype(_F32), v.astype(_F32)
            return dict(q=q, k=k, v=v, gate_f=gate_f.astype(_F32), gate_b=gate_b.astype(_F32),
                        xq_f=(q32 * tab_ref[T_XI_F]).astype(_BF16), xq_b=(q32 * tab_ref[T_XI_B]).astype(_BF16),
                        vz_b=(v32 * tab_ref[T_ZETA_B]).astype(_BF16),
                        s_f=[sfin_ref[n, h] for h in range(HEADS)])

        def emit(n, y):
            out.add(yb_ref, (slot, slice(n * CHUNK, (n + 1) * CHUNK), slice(None)), y)

        return _retention_stages(load, list(reversed(range(n_chunks))), BWD_RETAIN_GROUPS, s_ref, tab_ref, emit, out)

    def finish(slot, out):
        y = _dot(ya_ref[...], wout_ref[:a_width, :]) + _dot(yb_ref[slot].astype(_BF16), wout_ref[a_width:, :])
        x1 = x_ref[...] + g1_ref[...] * y
        h2 = ((_rms(x1) * n2_ref[...]) * (1.0 + sc_ref[...]) + sh_ref[...]).astype(_BF16)
        acc = None
        d_ff = wg_ref.shape[1]
        for lo in range(0, d_ff, FFN_BLOCK):
            yield
            cols = slice(lo, min(lo + FFN_BLOCK, d_ff))
            act = (_silu(_dot(h2, wg_ref[:, cols])) * _dot(h2, wu_ref[:, cols])).astype(_BF16)
            part = _dot(act, wd_ref[cols, :])
            acc = part if acc is None else acc + part
        x2 = x1 + g2_ref[...] * acc
        out.add(o_ref, (slice(None), slice(None)), _rms(x2) * nf_ref[...])

    def first(slot):
        s_ref[...] = s0_ref[...]
        out = _DeferredStores()
        _interleave(*retain(slot, out))
        out.commit()

    def middle(slot):
        out = _DeferredStores()
        _interleave(*retain(slot, out), finish(1 - slot, out))
        out.commit()

    def last(slot):
        out = _DeferredStores()
        _interleave(finish(slot, out))
        out.commit()

    _skewed_steps(first, middle, last)


def _backward_pass(x, ya, qkvg, sfin, mod3, norm2, norm_f, w_out, w_gate, w_up, w_down, tables, s_cb, tile):
    b, l, d = x.shape
    nt = l // tile
    const2 = lambda i, s: (0, 0)
    cur = lambda i, s: (i, nt - 1 - jnp.minimum(s, nt - 1), 0)
    prev = lambda i, s: (i, nt - 1 - jnp.maximum(s - 1, 0), 0)
    resident = dict(pipeline_mode=pl.Buffered(1))
    mod_spec = lambda j: pl.BlockSpec((None, 1, d), lambda i, s: (i, 0, j))
    return pl.pallas_call(
        functools.partial(_bwd_kernel, n_chunks=tile // CHUNK),
        grid=(b, nt + 1),
        in_specs=[pl.BlockSpec((None, tile, d), prev),
                  pl.BlockSpec((None, tile, ya.shape[-1]), prev),
                  pl.BlockSpec((None, tile, qkvg.shape[-1]), cur),
                  pl.BlockSpec((None, tile // CHUNK, HEADS, HEAD_DIM, HEAD_DIM),
                               lambda i, s: (i, nt - 1 - jnp.minimum(s, nt - 1), 0, 0, 0)),
                  mod_spec(2), mod_spec(3), mod_spec(4), mod_spec(5),
                  pl.BlockSpec((1, d), const2),
                  pl.BlockSpec((1, d), const2),
                  pl.BlockSpec(w_out.shape, const2, **resident),
                  pl.BlockSpec(w_gate.shape, const2, **resident),
                  pl.BlockSpec(w_up.shape, const2, **resident),
                  pl.BlockSpec(w_down.shape, const2, **resident),
                  pl.BlockSpec(tables.shape, lambda i, s: (0, 0, 0)),
                  pl.BlockSpec((None, HEADS, HEAD_DIM, HEAD_DIM), lambda i, s: (i, 0, 0, 0))],
        out_specs=pl.BlockSpec((None, tile, d), prev),
        out_shape=jax.ShapeDtypeStruct((b, l, d), x.dtype),
        scratch_shapes=[pltpu.VMEM((HEADS, HEAD_DIM, HEAD_DIM), _F32),
                        pltpu.VMEM((2, tile, HEADS * HEAD_DIM), _F32)],
        compiler_params=pltpu.CompilerParams(dimension_semantics=("arbitrary", "arbitrary"),
                                             vmem_limit_bytes=VMEM_LIMIT_BYTES),
        name="bwd",
    )(x, ya, qkvg, sfin, mod3, mod3, mod3, mod3, norm2, norm_f, w_out, w_gate, w_up, w_down, tables, s_cb)


def _rope_tables(l, k_scale):
    n_freq = HEAD_DIM // 4
    inv = np.float32(ROPE_BASE) ** (-np.arange(n_freq, dtype=np.float32) / np.float32(n_freq))
    rows = np.repeat(np.arange(l // GRID_W, dtype=np.float32), GRID_W)
    cols = np.tile(np.arange(GRID_W, dtype=np.float32), l // GRID_W)
    ang_r, ang_c = rows[:, None] * inv[None, :], cols[:, None] * inv[None, :]
    cos = np.concatenate([np.cos(ang_r), np.cos(ang_c)], axis=-1)
    sin = np.concatenate([np.sin(ang_r), np.sin(ang_c)], axis=-1)
    cos2, sin2 = np.concatenate([cos, cos], axis=-1), np.concatenate([-sin, sin], axis=-1)
    return np.concatenate([cos2, sin2, cos2 * k_scale, sin2 * k_scale], axis=-1).astype(np.float32)


def kernel(x, c, ctx, c_ctx, w_mod, b_mod, norm1, w_in, sg_gain, sg_w, sg_b, ret_logit_f, ret_logit_b,
           w_out, norm2, w_gate, w_up, w_down, norm_f):
    b, l, d = x.shape
    a_width, r_width = GROUPS * LANES, HEADS * HEAD_DIM
    tile = min(TILE, l)
    assert w_mod.shape[0] == 1 and b + 1 <= MOD_ROWS and l % tile == 0 and tile % CHUNK == 0
    assert ctx.shape[1] % CHUNK == 0 and l // tile >= 2
    assert a_width == r_width and d == a_width + r_width
    k_scale = HEAD_DIM ** -0.5

    mod3, w16, tables = _prepare(c, c_ctx, w_mod[0], b_mod[0][None, :], w_in[0], ret_logit_f[0], ret_logit_b[0])
    sg_bias = jnp.broadcast_to(jnp.transpose(sg_b[0])[:, :, None], (CHUNK, GROUPS, LANES)).reshape(CHUNK, a_width)
    ya, qkvg, sfin, s_cb, w_out_b, w_gate_b, w_up_b, w_down_b = _forward_pass(
        x, ctx, mod3, norm1, w16, sg_gain, sg_w[0], sg_bias, _rope_tables(l, k_scale),
        tables, (w_out[0], w_gate[0], w_up[0], w_down[0]), tile, k_scale)
    return _backward_pass(x, ya, qkvg, sfin, mod3, norm2, norm_f[None, :], w_out_b, w_gate_b, w_up_b, w_down_b,
                          tables, s_cb, tile)
```

```python
import functools

import numpy as np
import jax
import jax.numpy as jnp
from jax import lax
from jax.experimental import pallas as pl
from jax.experimental.pallas import tpu as pltpu

CHUNK = 128
GRID_W = 64
GROUPS = 4
HEADS = 4
HEAD_DIM = 128
RMS_EPS = 1e-6
ROPE_BASE = 10000.0
LANES = 128
MOD_ROWS = 8
TILE = 512
BWD_TILE = 256
PROJ_CHUNKS = 2
BWD_RETAIN_GROUPS = 2
FFN_BLOCK = 1024
VMEM_LIMIT_BYTES = 56 * 1024 * 1024

IN_BLOCKS = 7
U_BLOCK, V_BLOCK, Q_BLOCK, K_BLOCK, VR_BLOCK, GF_BLOCK, GB_BLOCK = range(IN_BLOCKS)

T_DMASK_F, T_DMASK_B, T_XI_F, T_XI_B, T_ZETA_F, T_ZETA_B, T_DECAY_F, T_DECAY_B = range(8)

_BF16 = jnp.bfloat16
_F32 = jnp.float32


def _dot(a, b):
    return jnp.dot(a, b, preferred_element_type=_F32)


def _dot_nt(a, b):
    return lax.dot_general(a, b, (((1,), (1,)), ((), ())), preferred_element_type=_F32)


def _dot_tn(a, b):
    return lax.dot_general(a, b, (((0,), (0,)), ((), ())), preferred_element_type=_F32)


def _silu(x):
    return x * (1.0 / (1.0 + jnp.exp(-x)))


def _gelu_tanh(x):
    c = float(np.sqrt(2.0 / np.pi))
    half = 0.5 * x
    return half + half * jnp.tanh(x * (c + (0.044715 * c) * (x * x)))


def _rms(x):
    return x * lax.rsqrt(jnp.mean(x * x, axis=-1, keepdims=True) + RMS_EPS)


def _head(a, h):
    return a[:, h * HEAD_DIM:(h + 1) * HEAD_DIM]


def _prep_kernel(c_ref, cctx_ref, wmod_ref, bmod_ref, win_ref, lf_ref, lb_ref, mod_ref, w16_ref, tab_ref):
    j = pl.program_id(0)

    n_batch, d = c_ref.shape
    cc = jnp.concatenate([c_ref[...], cctx_ref[...], jnp.zeros((MOD_ROWS - n_batch - 1, d), _F32)], axis=0)
    mod = _dot(_silu(cc), wmod_ref[...]) + bmod_ref[...]
    for r in range(MOD_ROWS):
        mod_ref[r] = mod[r:r + 1, :]

    def reorder_head_dims(t):
        quarter = HEAD_DIM // 4
        lane = lax.broadcasted_iota(jnp.int32, (t.shape[0], HEAD_DIM), 1)
        heads = []
        for h in range(t.shape[1] // HEAD_DIM):
            th = _head(t, h)
            from_right = pltpu.roll(th, HEAD_DIM - quarter, 1)
            from_left = pltpu.roll(th, quarter, 1)
            heads.append(jnp.where((lane >= quarter) & (lane < 2 * quarter), from_right,
                                   jnp.where((lane >= 2 * quarter) & (lane < 3 * quarter), from_left, th)))
        return jnp.concatenate(heads, axis=-1)

    is_qk = (j == Q_BLOCK) | (j == K_BLOCK)

    @pl.when(is_qk)
    def _():
        w16_ref[...] = reorder_head_dims(win_ref[...]).astype(_BF16)

    @pl.when(jnp.logical_not(is_qk))
    def _():
        w16_ref[...] = win_ref[...].astype(_BF16)

    @pl.when(j == 0)
    def _():
        def log_sigmoid(x):
            return -(jnp.maximum(-x, 0.0) + jnp.log1p(jnp.exp(-jnp.abs(x))))

        shape = (CHUNK, HEADS * HEAD_DIM)
        head = lax.broadcasted_iota(jnp.int32, shape, 1) // HEAD_DIM

        def per_head(logit_ref):
            t = jnp.zeros(shape, _F32)
            for h in range(HEADS):
                t = jnp.where(head == h, logit_ref[h], t)
            return t

        lg_f = log_sigmoid(per_head(lf_ref))
        lg_b = log_sigmoid(per_head(lb_ref))
        i = lax.broadcasted_iota(jnp.int32, shape, 0).astype(_F32)
        jj = (lax.broadcasted_iota(jnp.int32, shape, 1) & (HEAD_DIM - 1)).astype(_F32)
        tab_ref[T_DMASK_F] = jnp.where(i >= jj, jnp.exp(lg_f * jnp.maximum(i - jj, 0.0)), 0.0)
        tab_ref[T_DMASK_B] = jnp.where(jj >= i, jnp.exp(lg_b * jnp.maximum(jj - i, 0.0)), 0.0)
        tab_ref[T_XI_F] = jnp.exp(lg_f * (i + 1.0))
        tab_ref[T_XI_B] = jnp.exp(lg_b * (CHUNK - i))
        tab_ref[T_ZETA_F] = jnp.exp(lg_f * (CHUNK - 1.0 - i))
        tab_ref[T_ZETA_B] = jnp.exp(lg_b * i)
        tab_ref[T_DECAY_F] = jnp.exp(lg_f * CHUNK)
        tab_ref[T_DECAY_B] = jnp.exp(lg_b * CHUNK)


def _prepare(c, c_ctx, w_mod, b_mod, w_in, logit_f, logit_b):
    d, n_mod = w_mod.shape
    r_width = HEADS * HEAD_DIM
    n_mod_blocks = n_mod // d
    assert w_in.shape == (d, IN_BLOCKS * r_width) and n_mod_blocks <= IN_BLOCKS
    mod_blk = lambda j: jnp.minimum(j, n_mod_blocks - 1)
    return pl.pallas_call(
        _prep_kernel,
        grid=(IN_BLOCKS,),
        in_specs=[pl.BlockSpec(c.shape, lambda j: (0, 0)),
                  pl.BlockSpec((1, d), lambda j: (0, 0)),
                  pl.BlockSpec((d, d), lambda j: (0, mod_blk(j))),
                  pl.BlockSpec((1, d), lambda j: (0, mod_blk(j))),
                  pl.BlockSpec((d, r_width), lambda j: (0, j)),
                  pl.BlockSpec(memory_space=pltpu.SMEM),
                  pl.BlockSpec(memory_space=pltpu.SMEM)],
        out_specs=[pl.BlockSpec((MOD_ROWS, 1, d), lambda j: (0, 0, mod_blk(j))),
                   pl.BlockSpec((d, r_width), lambda j: (0, j)),
                   pl.BlockSpec((8, CHUNK, r_width), lambda j: (0, 0, 0))],
        out_shape=[jax.ShapeDtypeStruct((MOD_ROWS, 1, n_mod), _F32),
                   jax.ShapeDtypeStruct(w_in.shape, _BF16),
                   jax.ShapeDtypeStruct((8, CHUNK, r_width), _F32)],
        compiler_params=pltpu.CompilerParams(dimension_semantics=("arbitrary",),
                                             vmem_limit_bytes=VMEM_LIMIT_BYTES),
        name="prep",
    )(c, c_ctx[None, :], w_mod, b_mod, w_in, logit_f.astype(_F32), logit_b.astype(_F32))


def _context_states(ctx, shift, scale, gain, w_k, w_v, tab_ref, k_scale):
    hc = ((_rms(ctx) * gain) * (1.0 + scale) + shift).astype(_BF16)
    k = (_dot(hc, w_k) * k_scale).astype(_BF16)
    v = _dot(hc, w_v)
    n_chunks = ctx.shape[0] // CHUNK
    s_fwd, s_bwd = [], []
    for h in range(HEADS):
        s_f = jnp.zeros((HEAD_DIM, HEAD_DIM), _F32)
        s_b = jnp.zeros((HEAD_DIM, HEAD_DIM), _F32)
        for n in range(n_chunks):
            rows = slice(n * CHUNK, (n + 1) * CHUNK)
            upd = _dot_tn(_head(k[rows], h), (_head(v[rows], h) * _head(tab_ref[T_ZETA_F], h)).astype(_BF16))
            s_f = _head(tab_ref[T_DECAY_F], h) * s_f + upd
        for n in reversed(range(n_chunks)):
            rows = slice(n * CHUNK, (n + 1) * CHUNK)
            upd = _dot_tn(_head(k[rows], h), (_head(v[rows], h) * _head(tab_ref[T_ZETA_B], h)).astype(_BF16))
            s_b = _head(tab_ref[T_DECAY_B], h) * s_b + upd
        s_fwd.append(s_f)
        s_bwd.append(s_b)
    return s_fwd, s_bwd


def _retention_tile(load, chunk_order, state, tab_ref, emit):
    keys = [(n, h) for n in chunk_order for h in range(HEADS)]
    c = {n: load(n) for n in chunk_order}
    scores = {(n, h): _dot_nt(_head(c[n]["q"], h), _head(c[n]["k"], h)) for n, h in keys}
    upd = {(n, h): _dot_tn(_head(c[n]["k"], h), _head(c[n]["vz_b"], h)) for n, h in keys}
    yield
    lhs_f, lhs_b, rhs_f, rhs_b = {}, {}, {}, {}
    for n, h in keys:
        v = _head(c[n]["v"], h)
        lhs_f[n, h] = jnp.concatenate([(scores[n, h] * _head(tab_ref[T_DMASK_F], h)).astype(_BF16),
                                       _head(c[n]["xq_f"], h)], axis=1)
        lhs_b[n, h] = jnp.concatenate([(scores[n, h] * _head(tab_ref[T_DMASK_B], h)).astype(_BF16),
                                       _head(c[n]["xq_b"], h)], axis=1)
        rhs_f[n, h] = jnp.concatenate([v, c[n]["s_f"][h]], axis=0)
        rhs_b[n, h] = jnp.concatenate([v, state[h].astype(_BF16)], axis=0)
        state[h] = _head(tab_ref[T_DECAY_B], h) * state[h] + upd[n, h]
    yield
    o_f = {key: _dot(lhs_f[key], rhs_f[key]) for key in keys}
    o_b = {key: _dot(lhs_b[key], rhs_b[key]) for key in keys}
    yield
    for n in chunk_order:
        emit(n, jnp.concatenate([_silu(_head(c[n]["gate_f"], h)) * _rms(o_f[n, h])
                                 + _silu(_head(c[n]["gate_b"], h)) * _rms(o_b[n, h]) for h in range(HEADS)],
                                axis=-1))


def _retention_stages(load, chunk_order, n_groups, s_ref, tab_ref, emit, out):
    state = [None] * HEADS
    order = list(chunk_order)
    per = len(order) // n_groups
    groups = [order[i * per:(i + 1) * per] for i in range(n_groups)]

    def group_stage(i):
        if i:
            for _ in range(i):
                yield
        inner = _retention_tile(load, groups[i], state, tab_ref, emit)
        next(inner)
        yield
        if i == 0:
            for h in range(HEADS):
                state[h] = s_ref[h]
        next(inner)
        yield
        for _ in inner:
            yield
        if i == n_groups - 1:
            for h in range(HEADS):
                out.add(s_ref, h, state[h])

    return [group_stage(i) for i in range(n_groups)]


def _interleave(*stages):
    stages = list(stages)
    while stages:
        for g in list(stages):
            try:
                next(g)
            except StopIteration:
                stages.remove(g)


class _DeferredStores:
    def __init__(self):
        self._items = []

    def add(self, ref, idx, value):
        self._items.append((ref, idx, value))

    def commit(self):
        for ref, idx, value in self._items:
            ref[idx] = value


def _skewed_steps(first, middle, last):
    s = pl.program_id(1)
    n_tiles = pl.num_programs(1) - 1

    @pl.when(s == 0)
    def _():
        first(0)

    @pl.when((s > 0) & (s < n_tiles))
    def _():
        middle(s & 1)

    @pl.when(s == n_tiles)
    def _():
        last((s - 1) & 1)


def _fwd_kernel(x_ref, sh_ref, sc_ref, n1_ref, wuv_ref, wqk_ref, wvr_ref, wgf_ref, wgb_ref, gain_ref, sgw_ref, sgb_ref,
                rope_ref, tab_ref, ctx_ref, csh_ref, csc_ref, wo32_ref, wg32_ref, wu32_ref, wd32_ref,
                ya_ref, qkvg_ref, sfin_ref, scb_ref, wo16_ref, wg16_ref, wu16_ref, wd16_ref, s_ref, f32_ref, b16_ref, *,
                n_chunks, k_scale):
    a_width = GROUPS * LANES
    r_width = HEADS * HEAD_DIM
    U = 0
    VG, K, VZ = (i * r_width for i in range(3))
    rows_all = slice(None)

    def convert(out):
        for src, dst in ((wo32_ref, wo16_ref), (wg32_ref, wg16_ref), (wu32_ref, wu16_ref), (wd32_ref, wd16_ref)):
            out.add(dst, (rows_all, rows_all), src[...].astype(_BF16))

    def project(slot, out):
        n_sub = n_chunks // PROJ_CHUNKS
        for blk in range(n_sub):
            rows = slice(blk * PROJ_CHUNKS * CHUNK, (blk + 1) * PROJ_CHUNKS * CHUNK)

            def per_chunk(t, table):
                return jnp.concatenate([t[n * CHUNK:(n + 1) * CHUNK] * table for n in range(PROJ_CHUNKS)], axis=0)

            x = x_ref[rows, :]
            r = lax.rsqrt(jnp.mean(x * x, axis=-1, keepdims=True) + RMS_EPS)
            hx = ((x * r) * (n1_ref[...] * (1.0 + sc_ref[...])) + sh_ref[...]).astype(_BF16)

            def proj(w_ref, lo, width):
                return _dot(hx, w_ref[:, lo:lo + width])

            yield
            out.add(f32_ref, (slot, rows, slice(U, U + a_width)), _gelu_tanh(proj(wuv_ref, 0, a_width)))
            v = _gelu_tanh(proj(wuv_ref, a_width, a_width))
            vg = jnp.concatenate([(_rms(_head(v, g)) * _head(gain_ref[...], g)).astype(_BF16)
                                  for g in range(GROUPS)], axis=-1)
            out.add(b16_ref, (slot, rows, slice(VG, VG + a_width)), vg)
            yield

            def rope(t, table):
                cos = rope_ref[rows, 2 * table * HEAD_DIM:(2 * table + 1) * HEAD_DIM]
                sin = rope_ref[rows, (2 * table + 1) * HEAD_DIM:(2 * table + 2) * HEAD_DIM]
                return jnp.concatenate(
                    [_head(t, h) * cos + pltpu.roll(_head(t, h), HEAD_DIM // 2, 1) * sin for h in range(HEADS)],
                    axis=-1)

            q16 = rope(proj(wqk_ref, 0, r_width), 0).astype(_BF16)
            k16 = rope(proj(wqk_ref, r_width, r_width), 1).astype(_BF16)
            out.add(b16_ref, (slot, rows, slice(K, K + r_width)), k16)
            out.add(qkvg_ref, (rows, slice(0, r_width)), q16)
            out.add(qkvg_ref, (rows, slice(r_width, 2 * r_width)), k16)
            yield
            vr = proj(wvr_ref, 0, r_width)
            out.add(b16_ref, (slot, rows, slice(VZ, VZ + r_width)), per_chunk(vr, tab_ref[T_ZETA_F]).astype(_BF16))
            out.add(qkvg_ref, (rows, slice(2 * r_width, 3 * r_width)), vr.astype(_BF16))
            out.add(qkvg_ref, (rows, slice(3 * r_width, 4 * r_width)), proj(wgb_ref, 0, r_width).astype(_BF16))
            out.add(qkvg_ref, (rows, slice(4 * r_width, 5 * r_width)), proj(wgf_ref, 0, r_width).astype(_BF16))

    def gate_mix(slot, out):
        mixed = []
        for g in range(GROUPS):
            vg = jnp.concatenate([b16_ref[slot, n * CHUNK:(n + 1) * CHUNK, VG + g * LANES:VG + (g + 1) * LANES]
                                  for n in range(n_chunks)], axis=1)
            mixed.append(_dot(sgw_ref[g].astype(_BF16), vg))
        yield
        for g in range(GROUPS):
            cols = slice(g * LANES, (g + 1) * LANES)
            for n in range(n_chunks):
                rows = slice(n * CHUNK, (n + 1) * CHUNK)
                m = mixed[g][:, n * LANES:(n + 1) * LANES] + sgb_ref[:, cols]
                u = f32_ref[slot, rows, U + g * LANES:U + (g + 1) * LANES]
                out.add(ya_ref, (rows, cols), (u * m).astype(_BF16))

    def scan(slot, out):
        upd = {(n, h): _dot_tn(b16_ref[slot, n * CHUNK:(n + 1) * CHUNK, K + h * HEAD_DIM:K + (h + 1) * HEAD_DIM],
                               b16_ref[slot, n * CHUNK:(n + 1) * CHUNK, VZ + h * HEAD_DIM:VZ + (h + 1) * HEAD_DIM])
               for n in range(n_chunks) for h in range(HEADS)}
        yield
        for h in range(HEADS):
            state = s_ref[h]
            for n in range(n_chunks):
                out.add(sfin_ref, (n, h), state.astype(_BF16))
                state = _head(tab_ref[T_DECAY_F], h) * state + upd[n, h]
            out.add(s_ref, h, state)

    def first(slot):
        s_fwd, s_bwd = _context_states(ctx_ref[...], csh_ref[...], csc_ref[...], n1_ref[...],
                                       wqk_ref[:, r_width:], wvr_ref[...], tab_ref, k_scale)
        for h in range(HEADS):
            s_ref[h] = s_fwd[h]
            scb_ref[h] = s_bwd[h]
        out = _DeferredStores()
        convert(out)
        _interleave(project(slot, out))
        out.commit()

    def middle(slot):
        out = _DeferredStores()
        convert(out)
        _interleave(scan(1 - slot, out), gate_mix(1 - slot, out), project(slot, out))
        out.commit()

    def last(slot):
        out = _DeferredStores()
        convert(out)
        _interleave(scan(slot, out), gate_mix(slot, out))
        out.commit()

    _skewed_steps(first, middle, last)


def _forward_pass(x, ctx, mod3, norm1, w16, sg_gain, sg_w, sg_bias, rope, tables, bwd_weights, tile, k_scale):
    b, l, d = x.shape
    nt = l // tile
    a_width, r_width = GROUPS * LANES, HEADS * HEAD_DIM
    ctx_row = b
    state_spec = pl.BlockSpec((None, HEADS, HEAD_DIM, HEAD_DIM), lambda i, s: (i, 0, 0, 0))
    const2 = lambda i, s: (0, 0)
    cur = lambda i, s: (i, jnp.minimum(s, nt - 1), 0)
    prev = lambda i, s: (i, jnp.maximum(s - 1, 0), 0)
    resident = dict(pipeline_mode=pl.Buffered(1))
    n_blk = max(n for n in (1, 2, 4, 8, 16) if n <= b * (nt + 1))
    blk = lambda i, s: (jnp.minimum(i * (nt + 1) + s, n_blk - 1), 0)
    w_specs = [pl.BlockSpec((w.shape[0] // n_blk, w.shape[1]), blk) for w in bwd_weights]
    return pl.pallas_call(
        functools.partial(_fwd_kernel, n_chunks=tile // CHUNK, k_scale=k_scale),
        grid=(b, nt + 1),
        in_specs=[pl.BlockSpec((None, tile, d), cur),
                  pl.BlockSpec((None, 1, d), lambda i, s: (i, 0, 0)),
                  pl.BlockSpec((None, 1, d), lambda i, s: (i, 0, 1)),
                  pl.BlockSpec((1, d), const2),
                  pl.BlockSpec((d, 2 * r_width), lambda i, s: (0, U_BLOCK // 2), **resident),
                  pl.BlockSpec((d, 2 * r_width), lambda i, s: (0, Q_BLOCK // 2), **resident),
                  pl.BlockSpec((d, r_width), lambda i, s: (0, VR_BLOCK), **resident),
                  pl.BlockSpec((d, r_width), lambda i, s: (0, GF_BLOCK), **resident),
                  pl.BlockSpec((d, r_width), lambda i, s: (0, GB_BLOCK), **resident),
                  pl.BlockSpec((1, a_width), const2),
                  pl.BlockSpec(sg_w.shape, lambda i, s: (0, 0, 0)),
                  pl.BlockSpec(sg_bias.shape, const2),
                  pl.BlockSpec((tile, rope.shape[1]), lambda i, s: (jnp.minimum(s, nt - 1), 0)),
                  pl.BlockSpec(tables.shape, lambda i, s: (0, 0, 0)),
                  pl.BlockSpec((None,) + ctx.shape[1:], lambda i, s: (i, 0, 0)),
                  pl.BlockSpec((None, 1, d), lambda i, s: (ctx_row, 0, 0)),
                  pl.BlockSpec((None, 1, d), lambda i, s: (ctx_row, 0, 1))]
                 + w_specs,
        out_specs=[pl.BlockSpec((None, tile, a_width), prev),
                   pl.BlockSpec((None, tile, 5 * r_width), cur),
                   pl.BlockSpec((None, tile // CHUNK, HEADS, HEAD_DIM, HEAD_DIM),
                                lambda i, s: (i, jnp.maximum(s - 1, 0), 0, 0, 0)),
                   state_spec] + w_specs,
        out_shape=[jax.ShapeDtypeStruct((b, l, a_width), _BF16),
                   jax.ShapeDtypeStruct((b, l, 5 * r_width), _BF16),
                   jax.ShapeDtypeStruct((b, l // CHUNK, HEADS, HEAD_DIM, HEAD_DIM), _BF16),
                   jax.ShapeDtypeStruct((b, HEADS, HEAD_DIM, HEAD_DIM), _F32)]
                  + [jax.ShapeDtypeStruct(w.shape, _BF16) for w in bwd_weights],
        scratch_shapes=[pltpu.VMEM((HEADS, HEAD_DIM, HEAD_DIM), _F32),
                        pltpu.VMEM((2, tile, a_width), _F32),
                        pltpu.VMEM((2, tile, 3 * r_width), _BF16)],
        compiler_params=pltpu.CompilerParams(dimension_semantics=("arbitrary", "arbitrary"),
                                             vmem_limit_bytes=VMEM_LIMIT_BYTES),
        name="fwd",
    )(x, mod3, mod3, norm1, w16, w16, w16, w16, w16, sg_gain, sg_w, sg_bias, rope, tables, ctx, mod3, mod3,
      *bwd_weights)


def _bwd_kernel(x_ref, ya_ref, qkvg_ref, sfin_ref, g1_ref, sh_ref, sc_ref, g2_ref, n2_ref, nf_ref,
                wout_ref, wg_ref, wu_ref, wd_ref, tab_ref, s0_ref, o_ref, s_ref, yb_ref, *, n_chunks):
    a_width = GROUPS * LANES
    r_width = HEADS * HEAD_DIM

    def retain(slot, out):
        def load(n):
            rows = slice(n * CHUNK, (n + 1) * CHUNK)
            q, k, v, gate_b, gate_f = (qkvg_ref[rows, i * r_width:(i + 1) * r_width] for i in range(5))
            q32, v32 = q.astype(_F32), v.astype(_F32)
            return dict(q=q, k=k, v=v, gate_f=gate_f.astype(_F32), gate_b=gate_b.astype(_F32),
                        xq_f=(q32 * tab_ref[T_XI_F]).astype(_BF16), xq_b=(q32 * tab_ref[T_XI_B]).astype(_BF16),
                        vz_b=(v32 * tab_ref[T_ZETA_B]).astype(_BF16),
                        s_f=[sfin_ref[n, h] for h in range(HEADS)])

        def emit(n, y):
            out.add(yb_ref, (slot, slice(n * CHUNK, (n + 1) * CHUNK), slice(None)), y)

        return _retention_stages(load, list(reversed(range(n_chunks))), BWD_RETAIN_GROUPS, s_ref, tab_ref, emit, out)

    def finish(slot, out):
        y = _dot(ya_ref[...], wout_ref[:a_width, :]) + _dot(yb_ref[slot].astype(_BF16), wout_ref[a_width:, :])
        x1 = x_ref[...] + g1_ref[...] * y
        h2 = ((_rms(x1) * n2_ref[...]) * (1.0 + sc_ref[...]) + sh_ref[...]).astype(_BF16)
        acc = None
        d_ff = wg_ref.shape[1]
        for lo in range(0, d_ff, FFN_BLOCK):
            yield
            cols = slice(lo, min(lo + FFN_BLOCK, d_ff))
            act = (_silu(_dot(h2, wg_ref[:, cols])) * _dot(h2, wu_ref[:, cols])).astype(_BF16)
            part = _dot(act, wd_ref[cols, :])
            acc = part if acc is None else acc + part
        x2 = x1 + g2_ref[...] * acc
        out.add(o_ref, (slice(None), slice(None)), _rms(x2) * nf_ref[...])

    def first(slot):
        s_ref[...] = s0_ref[...]
        out = _DeferredStores()
        _interleave(*retain(slot, out))
        out.commit()

    def middle(slot):
        out = _DeferredStores()
        _interleave(*retain(slot, out), finish(1 - slot, out))
        out.commit()

    def last(slot):
        out = _DeferredStores()
        _interleave(finish(slot, out))
        out.commit()

    _skewed_steps(first, middle, last)


def _backward_pass(x, ya, qkvg, sfin, mod3, norm2, norm_f, w_out, w_gate, w_up, w_down, tables, s_cb, tile):
    b, l, d = x.shape
    nt = l // tile
    const2 = lambda i, s: (0, 0)
    cur = lambda i, s: (i, nt - 1 - jnp.minimum(s, nt - 1), 0)
    prev = lambda i, s: (i, nt - 1 - jnp.maximum(s - 1, 0), 0)
    resident = dict(pipeline_mode=pl.Buffered(1))
    mod_spec = lambda j: pl.BlockSpec((None, 1, d), lambda i, s: (i, 0, j))
    return pl.pallas_call(
        functools.partial(_bwd_kernel, n_chunks=tile // CHUNK),
        grid=(b, nt + 1),
        in_specs=[pl.BlockSpec((None, tile, d), prev),
                  pl.BlockSpec((None, tile, ya.shape[-1]), prev),
                  pl.BlockSpec((None, tile, qkvg.shape[-1]), cur),
                  pl.BlockSpec((None, tile // CHUNK, HEADS, HEAD_DIM, HEAD_DIM),
                               lambda i, s: (i, nt - 1 - jnp.minimum(s, nt - 1), 0, 0, 0)),
                  mod_spec(2), mod_spec(3), mod_spec(4), mod_spec(5),
                  pl.BlockSpec((1, d), const2),
                  pl.BlockSpec((1, d), const2),
                  pl.BlockSpec(w_out.shape, const2, **resident),
                  pl.BlockSpec(w_gate.shape, const2, **resident),
                  pl.BlockSpec(w_up.shape, const2, **resident),
                  pl.BlockSpec(w_down.shape, const2, **resident),
                  pl.BlockSpec(tables.shape, lambda i, s: (0, 0, 0)),
                  pl.BlockSpec((None, HEADS, HEAD_DIM, HEAD_DIM), lambda i, s: (i, 0, 0, 0))],
        out_specs=pl.BlockSpec((None, tile, d), prev),
        out_shape=jax.ShapeDtypeStruct((b, l, d), x.dtype),
        scratch_shapes=[pltpu.VMEM((HEADS, HEAD_DIM, HEAD_DIM), _F32),
                        pltpu.VMEM((2, tile, HEADS * HEAD_DIM), _F32)],
        compiler_params=pltpu.CompilerParams(dimension_semantics=("arbitrary", "arbitrary"),
                                             vmem_limit_bytes=VMEM_LIMIT_BYTES),
        name="bwd",
    )(x, ya, qkvg, sfin, mod3, mod3, mod3, mod3, norm2, norm_f, w_out, w_gate, w_up, w_down, tables, s_cb)


def _rope_tables(l, k_scale):
    n_freq = HEAD_DIM // 4
    inv = np.float32(ROPE_BASE) ** (-np.arange(n_freq, dtype=np.float32) / np.float32(n_freq))
    rows = np.repeat(np.arange(l // GRID_W, dtype=np.float32), GRID_W)
    cols = np.tile(np.arange(GRID_W, dtype=np.float32), l // GRID_W)
    ang_r, ang_c = rows[:, None] * inv[None, :], cols[:, None] * inv[None, :]
    cos = np.concatenate([np.cos(ang_r), np.cos(ang_c)], axis=-1)
    sin = np.concatenate([np.sin(ang_r), np.sin(ang_c)], axis=-1)
    cos2, sin2 = np.concatenate([cos, cos], axis=-1), np.concatenate([-sin, sin], axis=-1)
    return np.concatenate([cos2, sin2, cos2 * k_scale, sin2 * k_scale], axis=-1).astype(np.float32)


def kernel(x, c, ctx, c_ctx, w_mod, b_mod, norm1, w_in, sg_gain, sg_w, sg_b, ret_logit_f, ret_logit_b,
           w_out, norm2, w_gate, w_up, w_down, norm_f):
    b, l, d = x.shape
    a_width, r_width = GROUPS * LANES, HEADS * HEAD_DIM
    tile = min(TILE, l)
    assert w_mod.shape[0] == 1 and b + 1 <= MOD_ROWS and l % tile == 0 and tile % CHUNK == 0
    assert ctx.shape[1] % CHUNK == 0 and l // tile >= 2
    assert a_width == r_width and d == a_width + r_width
    k_scale = HEAD_DIM ** -0.5

    mod3, w16, tables = _prepare(c, c_ctx, w_mod[0], b_mod[0][None, :], w_in[0], ret_logit_f[0], ret_logit_b[0])
    sg_bias = jnp.broadcast_to(jnp.transpose(sg_b[0])[:, :, None], (CHUNK, GROUPS, LANES)).reshape(CHUNK, a_width)
    ya, qkvg, sfin, s_cb, w_out_b, w_gate_b, w_up_b, w_down_b = _forward_pass(
        x, ctx, mod3, norm1, w16, sg_gain, sg_w[0], sg_bias, _rope_tables(l, k_scale),
        tables, (w_out[0], w_gate[0], w_up[0], w_down[0]), tile, k_scale)
    return _backward_pass(x, ya, qkvg, sfin, mod3, norm2, norm_f[None, :], w_out_b, w_gate_b, w_up_b, w_down_b,
                          tables, s_cb, BWD_TILE)
```

```python
import functools

import numpy as np
import jax
import jax.numpy as jnp
from jax import lax
from jax.experimental import pallas as pl
from jax.experimental.pallas import tpu as pltpu

CHUNK = 128
GRID_W = 64
GROUPS = 4
HEADS = 4
HEAD_DIM = 128
RMS_EPS = 1e-6
ROPE_BASE = 10000.0
LANES = 128
MOD_ROWS = 8
TILE = 512
BWD_TILE = 256
PROJ_CHUNKS = 1
BWD_RETAIN_GROUPS = 2
FFN_BLOCK = 1024
VMEM_LIMIT_BYTES = 56 * 1024 * 1024

IN_BLOCKS = 7
U_BLOCK, V_BLOCK, Q_BLOCK, K_BLOCK, VR_BLOCK, GF_BLOCK, GB_BLOCK = range(IN_BLOCKS)

T_DMASK_F, T_DMASK_B, T_XI_F, T_XI_B, T_ZETA_F, T_ZETA_B, T_DECAY_F, T_DECAY_B = range(8)

_BF16 = jnp.bfloat16
_F32 = jnp.float32


def _dot(a, b):
    return jnp.dot(a, b, preferred_element_type=_F32)


def _dot_nt(a, b):
    return lax.dot_general(a, b, (((1,), (1,)), ((), ())), preferred_element_type=_F32)


def _dot_tn(a, b):
    return lax.dot_general(a, b, (((0,), (0,)), ((), ())), preferred_element_type=_F32)


def _silu(x):
    return x * (1.0 / (1.0 + jnp.exp(-x)))


def _gelu_tanh(x):
    c = float(np.sqrt(2.0 / np.pi))
    half = 0.5 * x
    return half + half * jnp.tanh(x * (c + (0.044715 * c) * (x * x)))


def _rms(x):
    return x * lax.rsqrt(jnp.mean(x * x, axis=-1, keepdims=True) + RMS_EPS)


def _head(a, h):
    return a[:, h * HEAD_DIM:(h + 1) * HEAD_DIM]


def _prep_kernel(c_ref, cctx_ref, wmod_ref, bmod_ref, win_ref, lf_ref, lb_ref, mod_ref, w16_ref, tab_ref):
    j = pl.program_id(0)

    n_batch, d = c_ref.shape
    cc = jnp.concatenate([c_ref[...], cctx_ref[...], jnp.zeros((MOD_ROWS - n_batch - 1, d), _F32)], axis=0)
    mod = _dot(_silu(cc), wmod_ref[...]) + bmod_ref[...]
    for r in range(MOD_ROWS):
        mod_ref[r] = mod[r:r + 1, :]

    def reorder_head_dims(t):
        quarter = HEAD_DIM // 4
        lane = lax.broadcasted_iota(jnp.int32, (t.shape[0], HEAD_DIM), 1)
        heads = []
        for h in range(t.shape[1] // HEAD_DIM):
            th = _head(t, h)
            from_right = pltpu.roll(th, HEAD_DIM - quarter, 1)
            from_left = pltpu.roll(th, quarter, 1)
            heads.append(jnp.where((lane >= quarter) & (lane < 2 * quarter), from_right,
                                   jnp.where((lane >= 2 * quarter) & (lane < 3 * quarter), from_left, th)))
        return jnp.concatenate(heads, axis=-1)

    is_qk = (j == Q_BLOCK) | (j == K_BLOCK)

    @pl.when(is_qk)
    def _():
        w16_ref[...] = reorder_head_dims(win_ref[...]).astype(_BF16)

    @pl.when(jnp.logical_not(is_qk))
    def _():
        w16_ref[...] = win_ref[...].astype(_BF16)

    @pl.when(j == 0)
    def _():
        def log_sigmoid(x):
            return -(jnp.maximum(-x, 0.0) + jnp.log1p(jnp.exp(-jnp.abs(x))))

        shape = (CHUNK, HEADS * HEAD_DIM)
        head = lax.broadcasted_iota(jnp.int32, shape, 1) // HEAD_DIM

        def per_head(logit_ref):
            t = jnp.zeros(shape, _F32)
            for h in range(HEADS):
                t = jnp.where(head == h, logit_ref[h], t)
            return t

        lg_f = log_sigmoid(per_head(lf_ref))
        lg_b = log_sigmoid(per_head(lb_ref))
        i = lax.broadcasted_iota(jnp.int32, shape, 0).astype(_F32)
        jj = (lax.broadcasted_iota(jnp.int32, shape, 1) & (HEAD_DIM - 1)).astype(_F32)
        tab_ref[T_DMASK_F] = jnp.where(i >= jj, jnp.exp(lg_f * jnp.maximum(i - jj, 0.0)), 0.0)
        tab_ref[T_DMASK_B] = jnp.where(jj >= i, jnp.exp(lg_b * jnp.maximum(jj - i, 0.0)), 0.0)
        tab_ref[T_XI_F] = jnp.exp(lg_f * (i + 1.0))
        tab_ref[T_XI_B] = jnp.exp(lg_b * (CHUNK - i))
        tab_ref[T_ZETA_F] = jnp.exp(lg_f * (CHUNK - 1.0 - i))
        tab_ref[T_ZETA_B] = jnp.exp(lg_b * i)
        tab_ref[T_DECAY_F] = jnp.exp(lg_f * CHUNK)
        tab_ref[T_DECAY_B] = jnp.exp(lg_b * CHUNK)


def _prepare(c, c_ctx, w_mod, b_mod, w_in, logit_f, logit_b):
    d, n_mod = w_mod.shape
    r_width = HEADS * HEAD_DIM
    n_mod_blocks = n_mod // d
    assert w_in.shape == (d, IN_BLOCKS * r_width) and n_mod_blocks <= IN_BLOCKS
    mod_blk = lambda j: jnp.minimum(j, n_mod_blocks - 1)
    return pl.pallas_call(
        _prep_kernel,
        grid=(IN_BLOCKS,),
        in_specs=[pl.BlockSpec(c.shape, lambda j: (0, 0)),
                  pl.BlockSpec((1, d), lambda j: (0, 0)),
                  pl.BlockSpec((d, d), lambda j: (0, mod_blk(j))),
                  pl.BlockSpec((1, d), lambda j: (0, mod_blk(j))),
                  pl.BlockSpec((d, r_width), lambda j: (0, j)),
                  pl.BlockSpec(memory_space=pltpu.SMEM),
                  pl.BlockSpec(memory_space=pltpu.SMEM)],
        out_specs=[pl.BlockSpec((MOD_ROWS, 1, d), lambda j: (0, 0, mod_blk(j))),
                   pl.BlockSpec((d, r_width), lambda j: (0, j)),
                   pl.BlockSpec((8, CHUNK, r_width), lambda j: (0, 0, 0))],
        out_shape=[jax.ShapeDtypeStruct((MOD_ROWS, 1, n_mod), _F32),
                   jax.ShapeDtypeStruct(w_in.shape, _BF16),
                   jax.ShapeDtypeStruct((8, CHUNK, r_width), _F32)],
        compiler_params=pltpu.CompilerParams(dimension_semantics=("arbitrary",),
                                             vmem_limit_bytes=VMEM_LIMIT_BYTES),
        name="prep",
    )(c, c_ctx[None, :], w_mod, b_mod, w_in, logit_f.astype(_F32), logit_b.astype(_F32))


def _context_states(ctx, shift, scale, gain, w_k, w_v, tab_ref, k_scale):
    hc = ((_rms(ctx) * gain) * (1.0 + scale) + shift).astype(_BF16)
    k = (_dot(hc, w_k) * k_scale).astype(_BF16)
    v = _dot(hc, w_v)
    n_chunks = ctx.shape[0] // CHUNK
    s_fwd, s_bwd = [], []
    for h in range(HEADS):
        s_f = jnp.zeros((HEAD_DIM, HEAD_DIM), _F32)
        s_b = jnp.zeros((HEAD_DIM, HEAD_DIM), _F32)
        for n in range(n_chunks):
            rows = slice(n * CHUNK, (n + 1) * CHUNK)
            upd = _dot_tn(_head(k[rows], h), (_head(v[rows], h) * _head(tab_ref[T_ZETA_F], h)).astype(_BF16))
            s_f = _head(tab_ref[T_DECAY_F], h) * s_f + upd
        for n in reversed(range(n_chunks)):
            rows = slice(n * CHUNK, (n + 1) * CHUNK)
            upd = _dot_tn(_head(k[rows], h), (_head(v[rows], h) * _head(tab_ref[T_ZETA_B], h)).astype(_BF16))
            s_b = _head(tab_ref[T_DECAY_B], h) * s_b + upd
        s_fwd.append(s_f)
        s_bwd.append(s_b)
    return s_fwd, s_bwd


def _retention_tile(load, chunk_order, state, tab_ref, emit):
    keys = [(n, h) for n in chunk_order for h in range(HEADS)]
    c = {n: load(n) for n in chunk_order}
    scores = {(n, h): _dot_nt(_head(c[n]["q"], h), _head(c[n]["k"], h)) for n, h in keys}
    upd = {(n, h): _dot_tn(_head(c[n]["k"], h), _head(c[n]["vz_b"], h)) for n, h in keys}
    yield
    lhs_f, lhs_b, rhs_f, rhs_b = {}, {}, {}, {}
    for n, h in keys:
        v = _head(c[n]["v"], h)
        lhs_f[n, h] = jnp.concatenate([(scores[n, h] * _head(tab_ref[T_DMASK_F], h)).astype(_BF16),
                                       _head(c[n]["xq_f"], h)], axis=1)
        lhs_b[n, h] = jnp.concatenate([(scores[n, h] * _head(tab_ref[T_DMASK_B], h)).astype(_BF16),
                                       _head(c[n]["xq_b"], h)], axis=1)
        rhs_f[n, h] = jnp.concatenate([v, c[n]["s_f"][h]], axis=0)
        rhs_b[n, h] = jnp.concatenate([v, state[h].astype(_BF16)], axis=0)
        state[h] = _head(tab_ref[T_DECAY_B], h) * state[h] + upd[n, h]
    yield
    o_f = {key: _dot(lhs_f[key], rhs_f[key]) for key in keys}
    o_b = {key: _dot(lhs_b[key], rhs_b[key]) for key in keys}
    yield
    for n in chunk_order:
        emit(n, jnp.concatenate([_silu(_head(c[n]["gate_f"], h)) * _rms(o_f[n, h])
                                 + _silu(_head(c[n]["gate_b"], h)) * _rms(o_b[n, h]) for h in range(HEADS)],
                                axis=-1))


def _retention_stages(load, chunk_order, n_groups, s_ref, tab_ref, emit, out):
    state = [None] * HEADS
    order = list(chunk_order)
    per = len(order) // n_groups
    groups = [order[i * per:(i + 1) * per] for i in range(n_groups)]

    def group_stage(i):
        if i:
            for _ in range(i):
                yield
        inner = _retention_tile(load, groups[i], state, tab_ref, emit)
        next(inner)
        yield
        if i == 0:
            for h in range(HEADS):
                state[h] = s_ref[h]
        next(inner)
        yield
        for _ in inner:
            yield
        if i == n_groups - 1:
            for h in range(HEADS):
                out.add(s_ref, h, state[h])

    return [group_stage(i) for i in range(n_groups)]


def _interleave(*stages):
    stages = list(stages)
    while stages:
        for g in list(stages):
            try:
                next(g)
            except StopIteration:
                stages.remove(g)


class _DeferredStores:
    def __init__(self):
        self._items = []

    def add(self, ref, idx, value):
        self._items.append((ref, idx, value))

    def commit(self):
        for ref, idx, value in self._items:
            ref[idx] = value


def _skewed_steps(first, middle, last):
    s = pl.program_id(1)
    n_tiles = pl.num_programs(1) - 1

    @pl.when(s == 0)
    def _():
        first(0)

    @pl.when((s > 0) & (s < n_tiles))
    def _():
        middle(s & 1)

    @pl.when(s == n_tiles)
    def _():
        last((s - 1) & 1)


def _fwd_kernel(x_ref, sh_ref, sc_ref, n1_ref, wuv_ref, wqk_ref, wvr_ref, wgf_ref, wgb_ref, gain_ref, sgw_ref, sgb_ref,
                rope_ref, tab_ref, ctx_ref, csh_ref, csc_ref, wo32_ref, wg32_ref, wu32_ref, wd32_ref,
                ya_ref, qkvg_ref, sfin_ref, scb_ref, wo16_ref, wg16_ref, wu16_ref, wd16_ref, s_ref, f32_ref, b16_ref, *,
                n_chunks, k_scale):
    a_width = GROUPS * LANES
    r_width = HEADS * HEAD_DIM
    U = 0
    VG, K, VZ = (i * r_width for i in range(3))
    rows_all = slice(None)

    def convert(out):
        for src, dst in ((wo32_ref, wo16_ref), (wg32_ref, wg16_ref), (wu32_ref, wu16_ref), (wd32_ref, wd16_ref)):
            out.add(dst, (rows_all, rows_all), src[...].astype(_BF16))

    def project(slot, out):
        n_sub = n_chunks // PROJ_CHUNKS
        for blk in range(n_sub):
            rows = slice(blk * PROJ_CHUNKS * CHUNK, (blk + 1) * PROJ_CHUNKS * CHUNK)

            def per_chunk(t, table):
                return jnp.concatenate([t[n * CHUNK:(n + 1) * CHUNK] * table for n in range(PROJ_CHUNKS)], axis=0)

            x = x_ref[rows, :]
            r = lax.rsqrt(jnp.mean(x * x, axis=-1, keepdims=True) + RMS_EPS)
            hx = ((x * r) * (n1_ref[...] * (1.0 + sc_ref[...])) + sh_ref[...]).astype(_BF16)

            def proj(w_ref, lo, width):
                return _dot(hx, w_ref[:, lo:lo + width])

            yield
            out.add(f32_ref, (slot, rows, slice(U, U + a_width)), _gelu_tanh(proj(wuv_ref, 0, a_width)))
            v = _gelu_tanh(proj(wuv_ref, a_width, a_width))
            vg = jnp.concatenate([(_rms(_head(v, g)) * _head(gain_ref[...], g)).astype(_BF16)
                                  for g in range(GROUPS)], axis=-1)
            out.add(b16_ref, (slot, rows, slice(VG, VG + a_width)), vg)
            yield

            def rope(t, table):
                cos = rope_ref[rows, 2 * table * HEAD_DIM:(2 * table + 1) * HEAD_DIM]
                sin = rope_ref[rows, (2 * table + 1) * HEAD_DIM:(2 * table + 2) * HEAD_DIM]
                return jnp.concatenate(
                    [_head(t, h) * cos + pltpu.roll(_head(t, h), HEAD_DIM // 2, 1) * sin for h in range(HEADS)],
                    axis=-1)

            q16 = rope(proj(wqk_ref, 0, r_width), 0).astype(_BF16)
            k16 = rope(proj(wqk_ref, r_width, r_width), 1).astype(_BF16)
            out.add(b16_ref, (slot, rows, slice(K, K + r_width)), k16)
            out.add(qkvg_ref, (rows, slice(0, r_width)), q16)
            out.add(qkvg_ref, (rows, slice(r_width, 2 * r_width)), k16)
            yield
            vr = proj(wvr_ref, 0, r_width)
            out.add(b16_ref, (slot, rows, slice(VZ, VZ + r_width)), per_chunk(vr, tab_ref[T_ZETA_F]).astype(_BF16))
            out.add(qkvg_ref, (rows, slice(2 * r_width, 3 * r_width)), vr.astype(_BF16))
            out.add(qkvg_ref, (rows, slice(3 * r_width, 4 * r_width)), proj(wgb_ref, 0, r_width).astype(_BF16))
            out.add(qkvg_ref, (rows, slice(4 * r_width, 5 * r_width)), proj(wgf_ref, 0, r_width).astype(_BF16))

    def gate_mix(slot, out):
        mixed = []
        for g in range(GROUPS):
            vg = jnp.concatenate([b16_ref[slot, n * CHUNK:(n + 1) * CHUNK, VG + g * LANES:VG + (g + 1) * LANES]
                                  for n in range(n_chunks)], axis=1)
            mixed.append(_dot(sgw_ref[g].astype(_BF16), vg))
        yield
        for g in range(GROUPS):
            cols = slice(g * LANES, (g + 1) * LANES)
            for n in range(n_chunks):
                rows = slice(n * CHUNK, (n + 1) * CHUNK)
                m = mixed[g][:, n * LANES:(n + 1) * LANES] + sgb_ref[:, cols]
                u = f32_ref[slot, rows, U + g * LANES:U + (g + 1) * LANES]
                out.add(ya_ref, (rows, cols), (u * m).astype(_BF16))

    def scan(slot, out):
        upd = {(n, h): _dot_tn(b16_ref[slot, n * CHUNK:(n + 1) * CHUNK, K + h * HEAD_DIM:K + (h + 1) * HEAD_DIM],
                               b16_ref[slot, n * CHUNK:(n + 1) * CHUNK, VZ + h * HEAD_DIM:VZ + (h + 1) * HEAD_DIM])
               for n in range(n_chunks) for h in range(HEADS)}
        yield
        for h in range(HEADS):
            state = s_ref[h]
            for n in range(n_chunks):
                out.add(sfin_ref, (n, h), state.astype(_BF16))
                state = _head(tab_ref[T_DECAY_F], h) * state + upd[n, h]
            out.add(s_ref, h, state)

    def first(slot):
        s_fwd, s_bwd = _context_states(ctx_ref[...], csh_ref[...], csc_ref[...], n1_ref[...],
                                       wqk_ref[:, r_width:], wvr_ref[...], tab_ref, k_scale)
        for h in range(HEADS):
            s_ref[h] = s_fwd[h]
            scb_ref[h] = s_bwd[h]
        out = _DeferredStores()
        convert(out)
        _interleave(project(slot, out))
        out.commit()

    def middle(slot):
        out = _DeferredStores()
        convert(out)
        _interleave(scan(1 - slot, out), gate_mix(1 - slot, out), project(slot, out))
        out.commit()

    def last(slot):
        out = _DeferredStores()
        convert(out)
        _interleave(scan(slot, out), gate_mix(slot, out))
        out.commit()

    _skewed_steps(first, middle, last)


def _forward_pass(x, ctx, mod3, norm1, w16, sg_gain, sg_w, sg_bias, rope, tables, bwd_weights, tile, k_scale):
    b, l, d = x.shape
    nt = l // tile
    a_width, r_width = GROUPS * LANES, HEADS * HEAD_DIM
    ctx_row = b
    state_spec = pl.BlockSpec((None, HEADS, HEAD_DIM, HEAD_DIM), lambda i, s: (i, 0, 0, 0))
    const2 = lambda i, s: (0, 0)
    cur = lambda i, s: (i, jnp.minimum(s, nt - 1), 0)
    prev = lambda i, s: (i, jnp.maximum(s - 1, 0), 0)
    resident = dict(pipeline_mode=pl.Buffered(1))
    n_blk = max(n for n in (1, 2, 4, 8, 16) if n <= b * (nt + 1))
    blk = lambda i, s: (jnp.minimum(i * (nt + 1) + s, n_blk - 1), 0)
    w_specs = [pl.BlockSpec((w.shape[0] // n_blk, w.shape[1]), blk) for w in bwd_weights]
    return pl.pallas_call(
        functools.partial(_fwd_kernel, n_chunks=tile // CHUNK, k_scale=k_scale),
        grid=(b, nt + 1),
        in_specs=[pl.BlockSpec((None, tile, d), cur),
                  pl.BlockSpec((None, 1, d), lambda i, s: (i, 0, 0)),
                  pl.BlockSpec((None, 1, d), lambda i, s: (i, 0, 1)),
                  pl.BlockSpec((1, d), const2),
                  pl.BlockSpec((d, 2 * r_width), lambda i, s: (0, U_BLOCK // 2), **resident),
                  pl.BlockSpec((d, 2 * r_width), lambda i, s: (0, Q_BLOCK // 2), **resident),
                  pl.BlockSpec((d, r_width), lambda i, s: (0, VR_BLOCK), **resident),
                  pl.BlockSpec((d, r_width), lambda i, s: (0, GF_BLOCK), **resident),
                  pl.BlockSpec((d, r_width), lambda i, s: (0, GB_BLOCK), **resident),
                  pl.BlockSpec((1, a_width), const2),
                  pl.BlockSpec(sg_w.shape, lambda i, s: (0, 0, 0)),
                  pl.BlockSpec(sg_bias.shape, const2),
                  pl.BlockSpec((tile, rope.shape[1]), lambda i, s: (jnp.minimum(s, nt - 1), 0)),
                  pl.BlockSpec(tables.shape, lambda i, s: (0, 0, 0)),
                  pl.BlockSpec((None,) + ctx.shape[1:], lambda i, s: (i, 0, 0)),
                  pl.BlockSpec((None, 1, d), lambda i, s: (ctx_row, 0, 0)),
                  pl.BlockSpec((None, 1, d), lambda i, s: (ctx_row, 0, 1))]
                 + w_specs,
        out_specs=[pl.BlockSpec((None, tile, a_width), prev),
                   pl.BlockSpec((None, tile, 5 * r_width), cur),
                   pl.BlockSpec((None, tile // CHUNK, HEADS, HEAD_DIM, HEAD_DIM),
                                lambda i, s: (i, jnp.maximum(s - 1, 0), 0, 0, 0)),
                   state_spec] + w_specs,
        out_shape=[jax.ShapeDtypeStruct((b, l, a_width), _BF16),
                   jax.ShapeDtypeStruct((b, l, 5 * r_width), _BF16),
                   jax.ShapeDtypeStruct((b, l // CHUNK, HEADS, HEAD_DIM, HEAD_DIM), _BF16),
                   jax.ShapeDtypeStruct((b, HEADS, HEAD_DIM, HEAD_DIM), _F32)]
                  + [jax.ShapeDtypeStruct(w.shape, _BF16) for w in bwd_weights],
        scratch_shapes=[pltpu.VMEM((HEADS, HEAD_DIM, HEAD_DIM), _F32),
                        pltpu.VMEM((2, tile, a_width), _F32),
                        pltpu.VMEM((2, tile, 3 * r_width), _BF16)],
        compiler_params=pltpu.CompilerParams(dimension_semantics=("arbitrary", "arbitrary"),
                                             vmem_limit_bytes=VMEM_LIMIT_BYTES),
        name="fwd",
    )(x, mod3, mod3, norm1, w16, w16, w16, w16, w16, sg_gain, sg_w, sg_bias, rope, tables, ctx, mod3, mod3,
      *bwd_weights)


def _bwd_kernel(x_ref, ya_ref, qkvg_ref, sfin_ref, g1_ref, sh_ref, sc_ref, g2_ref, n2_ref, nf_ref,
                wout_ref, wg_ref, wu_ref, wd_ref, tab_ref, s0_ref, o_ref, s_ref, yb_ref, *, n_chunks):
    a_width = GROUPS * LANES
    r_width = HEADS * HEAD_DIM

    def retain(slot, out):
        def load(n):
            rows = slice(n * CHUNK, (n + 1) * CHUNK)
            q, k, v, gate_b, gate_f = (qkvg_ref[rows, i * r_width:(i + 1) * r_width] for i in range(5))
            q32, v32 = q.astype(_F32), v.astype(_F32)
            return dict(q=q, k=k, v=v, gate_f=gate_f.astype(_F32), gate_b=gate_b.astype(_F32),
                        xq_f=(q32 * tab_ref[T_XI_F]).astype(_BF16), xq_b=(q32 * tab_ref[T_XI_B]).astype(_BF16),
                        vz_b=(v32 * tab_ref[T_ZETA_B]).astype(_BF16),
                        s_f=[sfin_ref[n, h] for h in range(HEADS)])

        def emit(n, y):
            out.add(yb_ref, (slot, slice(n * CHUNK, (n + 1) * CHUNK), slice(None)), y)

        return _retention_stages(load, list(reversed(range(n_chunks))), BWD_RETAIN_GROUPS, s_ref, tab_ref, emit, out)

    def finish(slot, out):
        y = _dot(ya_ref[...], wout_ref[:a_width, :]) + _dot(yb_ref[slot].astype(_BF16), wout_ref[a_width:, :])
        x1 = x_ref[...] + g1_ref[...] * y
        h2 = ((_rms(x1) * n2_ref[...]) * (1.0 + sc_ref[...]) + sh_ref[...]).astype(_BF16)
        acc = None
        d_ff = wg_ref.shape[1]
        for lo in range(0, d_ff, FFN_BLOCK):
            yield
            cols = slice(lo, min(lo + FFN_BLOCK, d_ff))
            act = (_silu(_dot(h2, wg_ref[:, cols])) * _dot(h2, wu_ref[:, cols])).astype(_BF16)
            part = _dot(act, wd_ref[cols, :])
            acc = part if acc is None else acc + part
        x2 = x1 + g2_ref[...] * acc
        out.add(o_ref, (slice(None), slice(None)), _rms(x2) * nf_ref[...])

    def first(slot):
        s_ref[...] = s0_ref[...]
        out = _DeferredStores()
        _interleave(*retain(slot, out))
        out.commit()

    def middle(slot):
        out = _DeferredStores()
        _interleave(*retain(slot, out), finish(1 - slot, out))
        out.commit()

    def last(slot):
        out = _DeferredStores()
        _interleave(finish(slot, out))
        out.commit()

    _skewed_steps(first, middle, last)


def _backward_pass(x, ya, qkvg, sfin, mod3, norm2, norm_f, w_out, w_gate, w_up, w_down, tables, s_cb, tile):
    b, l, d = x.shape
    nt = l // tile
    const2 = lambda i, s: (0, 0)
    cur = lambda i, s: (i, nt - 1 - jnp.minimum(s, nt - 1), 0)
    prev = lambda i, s: (i, nt - 1 - jnp.maximum(s - 1, 0), 0)
    resident = dict(pipeline_mode=pl.Buffered(1))
    mod_spec = lambda j: pl.BlockSpec((None, 1, d), lambda i, s: (i, 0, j))
    return pl.pallas_call(
        functools.partial(_bwd_kernel, n_chunks=tile // CHUNK),
        grid=(b, nt + 1),
        in_specs=[pl.BlockSpec((None, tile, d), prev),
                  pl.BlockSpec((None, tile, ya.shape[-1]), prev),
                  pl.BlockSpec((None, tile, qkvg.shape[-1]), cur),
                  pl.BlockSpec((None, tile // CHUNK, HEADS, HEAD_DIM, HEAD_DIM),
                               lambda i, s: (i, nt - 1 - jnp.minimum(s, nt - 1), 0, 0, 0)),
                  mod_spec(2), mod_spec(3), mod_spec(4), mod_spec(5),
                  pl.BlockSpec((1, d), const2),
                  pl.BlockSpec((1, d), const2),
                  pl.BlockSpec(w_out.shape, const2, **resident),
                  pl.BlockSpec(w_gate.shape, const2, **resident),
                  pl.BlockSpec(w_up.shape, const2, **resident),
                  pl.BlockSpec(w_down.shape, const2, **resident),
                  pl.BlockSpec(tables.shape, lambda i, s: (0, 0, 0)),
                  pl.BlockSpec((None, HEADS, HEAD_DIM, HEAD_DIM), lambda i, s: (i, 0, 0, 0))],
        out_specs=pl.BlockSpec((None, tile, d), prev),
        out_shape=jax.ShapeDtypeStruct((b, l, d), x.dtype),
        scratch_shapes=[pltpu.VMEM((HEADS, HEAD_DIM, HEAD_DIM), _F32),
                        pltpu.VMEM((2, tile, HEADS * HEAD_DIM), _F32)],
        compiler_params=pltpu.CompilerParams(dimension_semantics=("arbitrary", "arbitrary"),
                                             vmem_limit_bytes=VMEM_LIMIT_BYTES),
        name="bwd",
    )(x, ya, qkvg, sfin, mod3, mod3, mod3, mod3, norm2, norm_f, w_out, w_gate, w_up, w_down, tables, s_cb)


def _rope_tables(l, k_scale):
    n_freq = HEAD_DIM // 4
    inv = np.float32(ROPE_BASE) ** (-np.arange(n_freq, dtype=np.float32) / np.float32(n_freq))
    rows = np.repeat(np.arange(l // GRID_W, dtype=np.float32), GRID_W)
    cols = np.tile(np.arange(GRID_W, dtype=np.float32), l // GRID_W)
    ang_r, ang_c = rows[:, None] * inv[None, :], cols[:, None] * inv[None, :]
    cos = np.concatenate([np.cos(ang_r), np.cos(ang_c)], axis=-1)
    sin = np.concatenate([np.sin(ang_r), np.sin(ang_c)], axis=-1)
    cos2, sin2 = np.concatenate([cos, cos], axis=-1), np.concatenate([-sin, sin], axis=-1)
    return np.concatenate([cos2, sin2, cos2 * k_scale, sin2 * k_scale], axis=-1).astype(np.float32)


def kernel(x, c, ctx, c_ctx, w_mod, b_mod, norm1, w_in, sg_gain, sg_w, sg_b, ret_logit_f, ret_logit_b,
           w_out, norm2, w_gate, w_up, w_down, norm_f):
    b, l, d = x.shape
    a_width, r_width = GROUPS * LANES, HEADS * HEAD_DIM
    tile = min(TILE, l)
    assert w_mod.shape[0] == 1 and b + 1 <= MOD_ROWS and l % tile == 0 and tile % CHUNK == 0
    assert ctx.shape[1] % CHUNK == 0 and l // tile >= 2
    assert a_width == r_width and d == a_width + r_width
    k_scale = HEAD_DIM ** -0.5

    mod3, w16, tables = _prepare(c, c_ctx, w_mod[0], b_mod[0][None, :], w_in[0], ret_logit_f[0], ret_logit_b[0])
    sg_bias = jnp.broadcast_to(jnp.transpose(sg_b[0])[:, :, None], (CHUNK, GROUPS, LANES)).reshape(CHUNK, a_width)
    ya, qkvg, sfin, s_cb, w_out_b, w_gate_b, w_up_b, w_down_b = _forward_pass(
        x, ctx, mod3, norm1, w16, sg_gain, sg_w[0], sg_bias, _rope_tables(l, k_scale),
        tables, (w_out[0], w_gate[0], w_up[0], w_down[0]), tile, k_scale)
    return _backward_pass(x, ya, qkvg, sfin, mod3, norm2, norm_f[None, :], w_out_b, w_gate_b, w_up_b, w_down_b,
                          tables, s_cb, BWD_TILE)
```

```python
import functools

import numpy as np
import jax
import jax.numpy as jnp
from jax import lax
from jax.experimental import pallas as pl
from jax.experimental.pallas import tpu as pltpu

CHUNK = 128
GRID_W = 64
GROUPS = 4
HEADS = 4
HEAD_DIM = 128
RMS_EPS = 1e-6
ROPE_BASE = 10000.0
LANES = 128
MOD_ROWS = 8
TILE = 512
BWD_TILE = 256
PROJ_CHUNKS = 1
BWD_RETAIN_GROUPS = 2
FFN_BLOCK = 1024
VMEM_LIMIT_BYTES = 56 * 1024 * 1024

IN_BLOCKS = 7
U_BLOCK, V_BLOCK, Q_BLOCK, K_BLOCK, VR_BLOCK, GF_BLOCK, GB_BLOCK = range(IN_BLOCKS)

T_DMASK_F, T_DMASK_B, T_XI_F, T_XI_B, T_ZETA_F, T_ZETA_B, T_DECAY_F, T_DECAY_B = range(8)

_BF16 = jnp.bfloat16
_F32 = jnp.float32


def _dot(a, b):
    return jnp.dot(a, b, preferred_element_type=_F32)


def _dot_nt(a, b):
    return lax.dot_general(a, b, (((1,), (1,)), ((), ())), preferred_element_type=_F32)


def _dot_tn(a, b):
    return lax.dot_general(a, b, (((0,), (0,)), ((), ())), preferred_element_type=_F32)


def _silu(x):
    return x * (1.0 / (1.0 + jnp.exp(-x)))


def _gelu_tanh(x):
    c = float(np.sqrt(2.0 / np.pi))
    half = 0.5 * x
    return half + half * jnp.tanh(x * (c + (0.044715 * c) * (x * x)))


def _rms(x):
    return x * lax.rsqrt(jnp.mean(x * x, axis=-1, keepdims=True) + RMS_EPS)


def _head(a, h):
    return a[:, h * HEAD_DIM:(h + 1) * HEAD_DIM]


def _prep_kernel(c_ref, cctx_ref, wmod_ref, bmod_ref, win_ref, lf_ref, lb_ref, mod_ref, w16_ref, tab_ref):
    j = pl.program_id(0)

    n_batch, d = c_ref.shape
    cc = jnp.concatenate([c_ref[...], cctx_ref[...], jnp.zeros((MOD_ROWS - n_batch - 1, d), _F32)], axis=0)
    mod = _dot(_silu(cc), wmod_ref[...]) + bmod_ref[...]
    for r in range(MOD_ROWS):
        mod_ref[r] = mod[r:r + 1, :]

    def reorder_head_dims(t):
        quarter = HEAD_DIM // 4
        lane = lax.broadcasted_iota(jnp.int32, (t.shape[0], HEAD_DIM), 1)
        heads = []
        for h in range(t.shape[1] // HEAD_DIM):
            th = _head(t, h)
            from_right = pltpu.roll(th, HEAD_DIM - quarter, 1)
            from_left = pltpu.roll(th, quarter, 1)
            heads.append(jnp.where((lane >= quarter) & (lane < 2 * quarter), from_right,
                                   jnp.where((lane >= 2 * quarter) & (lane < 3 * quarter), from_left, th)))
        return jnp.concatenate(heads, axis=-1)

    is_qk = (j == Q_BLOCK) | (j == K_BLOCK)

    @pl.when(is_qk)
    def _():
        w16_ref[...] = reorder_head_dims(win_ref[...]).astype(_BF16)

    @pl.when(jnp.logical_not(is_qk))
    def _():
        w16_ref[...] = win_ref[...].astype(_BF16)

    @pl.when(j == 0)
    def _():
        def log_sigmoid(x):
            return -(jnp.maximum(-x, 0.0) + jnp.log1p(jnp.exp(-jnp.abs(x))))

        shape = (CHUNK, HEADS * HEAD_DIM)
        head = lax.broadcasted_iota(jnp.int32, shape, 1) // HEAD_DIM

        def per_head(logit_ref):
            t = jnp.zeros(shape, _F32)
            for h in range(HEADS):
                t = jnp.where(head == h, logit_ref[h], t)
            return t

        lg_f = log_sigmoid(per_head(lf_ref))
        lg_b = log_sigmoid(per_head(lb_ref))
        i = lax.broadcasted_iota(jnp.int32, shape, 0).astype(_F32)
        jj = (lax.broadcasted_iota(jnp.int32, shape, 1) & (HEAD_DIM - 1)).astype(_F32)
        tab_ref[T_DMASK_F] = jnp.where(i >= jj, jnp.exp(lg_f * jnp.maximum(i - jj, 0.0)), 0.0)
        tab_ref[T_DMASK_B] = jnp.where(jj >= i, jnp.exp(lg_b * jnp.maximum(jj - i, 0.0)), 0.0)
        tab_ref[T_XI_F] = jnp.exp(lg_f * (i + 1.0))
        tab_ref[T_XI_B] = jnp.exp(lg_b * (CHUNK - i))
        tab_ref[T_ZETA_F] = jnp.exp(lg_f * (CHUNK - 1.0 - i))
        tab_ref[T_ZETA_B] = jnp.exp(lg_b * i)
        tab_ref[T_DECAY_F] = jnp.exp(lg_f * CHUNK)
        tab_ref[T_DECAY_B] = jnp.exp(lg_b * CHUNK)


def _prepare(c, c_ctx, w_mod, b_mod, w_in, logit_f, logit_b):
    d, n_mod = w_mod.shape
    r_width = HEADS * HEAD_DIM
    n_mod_blocks = n_mod // d
    assert w_in.shape == (d, IN_BLOCKS * r_width) and n_mod_blocks <= IN_BLOCKS
    mod_blk = lambda j: jnp.minimum(j, n_mod_blocks - 1)
    return pl.pallas_call(
        _prep_kernel,
        grid=(IN_BLOCKS,),
        in_specs=[pl.BlockSpec(c.shape, lambda j: (0, 0)),
                  pl.BlockSpec((1, d), lambda j: (0, 0)),
                  pl.BlockSpec((d, d), lambda j: (0, mod_blk(j))),
                  pl.BlockSpec((1, d), lambda j: (0, mod_blk(j))),
                  pl.BlockSpec((d, r_width), lambda j: (0, j)),
                  pl.BlockSpec(memory_space=pltpu.SMEM),
                  pl.BlockSpec(memory_space=pltpu.SMEM)],
        out_specs=[pl.BlockSpec((MOD_ROWS, 1, d), lambda j: (0, 0, mod_blk(j))),
                   pl.BlockSpec((d, r_width), lambda j: (0, j)),
                   pl.BlockSpec((8, CHUNK, r_width), lambda j: (0, 0, 0))],
        out_shape=[jax.ShapeDtypeStruct((MOD_ROWS, 1, n_mod), _F32),
                   jax.ShapeDtypeStruct(w_in.shape, _BF16),
                   jax.ShapeDtypeStruct((8, CHUNK, r_width), _F32)],
        compiler_params=pltpu.CompilerParams(dimension_semantics=("arbitrary",),
                                             vmem_limit_bytes=VMEM_LIMIT_BYTES),
        name="prep",
    )(c, c_ctx[None, :], w_mod, b_mod, w_in, logit_f.astype(_F32), logit_b.astype(_F32))


def _context_states(ctx, shift, scale, gain, w_k, w_v, tab_ref, k_scale):
    hc = ((_rms(ctx) * gain) * (1.0 + scale) + shift).astype(_BF16)
    k = (_dot(hc, w_k) * k_scale).astype(_BF16)
    v = _dot(hc, w_v)
    n_chunks = ctx.shape[0] // CHUNK
    s_fwd, s_bwd = [], []
    for h in range(HEADS):
        s_f = jnp.zeros((HEAD_DIM, HEAD_DIM), _F32)
        s_b = jnp.zeros((HEAD_DIM, HEAD_DIM), _F32)
        for n in range(n_chunks):
            rows = slice(n * CHUNK, (n + 1) * CHUNK)
            upd = _dot_tn(_head(k[rows], h), (_head(v[rows], h) * _head(tab_ref[T_ZETA_F], h)).astype(_BF16))
            s_f = _head(tab_ref[T_DECAY_F], h) * s_f + upd
        for n in reversed(range(n_chunks)):
            rows = slice(n * CHUNK, (n + 1) * CHUNK)
            upd = _dot_tn(_head(k[rows], h), (_head(v[rows], h) * _head(tab_ref[T_ZETA_B], h)).astype(_BF16))
            s_b = _head(tab_ref[T_DECAY_B], h) * s_b + upd
        s_fwd.append(s_f)
        s_bwd.append(s_b)
    return s_fwd, s_bwd


def _retention_tile(load, chunk_order, state, tab_ref, emit):
    keys = [(n, h) for n in chunk_order for h in range(HEADS)]
    c = {n: load(n) for n in chunk_order}
    scores = {(n, h): _dot_nt(_head(c[n]["q"], h), _head(c[n]["k"], h)) for n, h in keys}
    upd = {(n, h): _dot_tn(_head(c[n]["k"], h), _head(c[n]["vz_b"], h)) for n, h in keys}
    yield
    lhs_f, lhs_b, rhs_f, rhs_b = {}, {}, {}, {}
    for n, h in keys:
        v = _head(c[n]["v"], h)
        lhs_f[n, h] = jnp.concatenate([(scores[n, h] * _head(tab_ref[T_DMASK_F], h)).astype(_BF16),
                                       _head(c[n]["xq_f"], h)], axis=1)
        lhs_b[n, h] = jnp.concatenate([(scores[n, h] * _head(tab_ref[T_DMASK_B], h)).astype(_BF16),
                                       _head(c[n]["xq_b"], h)], axis=1)
        rhs_f[n, h] = jnp.concatenate([v, c[n]["s_f"][h]], axis=0)
        rhs_b[n, h] = jnp.concatenate([v, state[h].astype(_BF16)], axis=0)
        state[h] = _head(tab_ref[T_DECAY_B], h) * state[h] + upd[n, h]
    yield
    o_f = {key: _dot(lhs_f[key], rhs_f[key]) for key in keys}
    o_b = {key: _dot(lhs_b[key], rhs_b[key]) for key in keys}
    yield
    for n in chunk_order:
        emit(n, jnp.concatenate([_silu(_head(c[n]["gate_f"], h)) * _rms(o_f[n, h])
                                 + _silu(_head(c[n]["gate_b"], h)) * _rms(o_b[n, h]) for h in range(HEADS)],
                                axis=-1))


def _retention_stages(load, chunk_order, n_groups, s_ref, tab_ref, emit, out):
    state = [None] * HEADS
    order = list(chunk_order)
    per = len(order) // n_groups
    groups = [order[i * per:(i + 1) * per] for i in range(n_groups)]

    def group_stage(i):
        if i:
            for _ in range(i):
                yield
        inner = _retention_tile(load, groups[i], state, tab_ref, emit)
        next(inner)
        yield
        if i == 0:
            for h in range(HEADS):
                state[h] = s_ref[h]
        next(inner)
        yield
        for _ in inner:
            yield
        if i == n_groups - 1:
            for h in range(HEADS):
                out.add(s_ref, h, state[h])

    return [group_stage(i) for i in range(n_groups)]


def _interleave(*stages):
    stages = list(stages)
    while stages:
        for g in list(stages):
            try:
                next(g)
            except StopIteration:
                stages.remove(g)


class _DeferredStores:
    def __init__(self):
        self._items = []

    def add(self, ref, idx, value):
        self._items.append((ref, idx, value))

    def commit(self):
        for ref, idx, value in self._items:
            ref[idx] = value


def _skewed_steps(first, middle, last):
    s = pl.program_id(1)
    n_tiles = pl.num_programs(1) - 1

    @pl.when(s == 0)
    def _():
        first(0)

    @pl.when((s > 0) & (s < n_tiles))
    def _():
        middle(s & 1)

    @pl.when(s == n_tiles)
    def _():
        last((s - 1) & 1)


def _fwd_kernel(x_ref, sh_ref, sc_ref, n1_ref, wuv_ref, wqk_ref, wvr_ref, wgf_ref, wgb_ref, gain_ref, sgw_ref, sgb_ref,
                rope_ref, tab_ref, ctx_ref, csh_ref, csc_ref, wo32_ref, wg32_ref, wu32_ref, wd32_ref,
                ya_ref, qkvg_ref, sfin_ref, scb_ref, wo16_ref, wg16_ref, wu16_ref, wd16_ref, s_ref, f32_ref, b16_ref, *,
                n_chunks, k_scale):
    a_width = GROUPS * LANES
    r_width = HEADS * HEAD_DIM
    U = 0
    VG, K, VZ = (i * r_width for i in range(3))
    rows_all = slice(None)

    def convert(out):
        for src, dst in ((wo32_ref, wo16_ref), (wg32_ref, wg16_ref), (wu32_ref, wu16_ref), (wd32_ref, wd16_ref)):
            out.add(dst, (rows_all, rows_all), src[...].astype(_BF16))

    def project(slot, out):
        n_sub = n_chunks // PROJ_CHUNKS
        for blk in range(n_sub):
            rows = slice(blk * PROJ_CHUNKS * CHUNK, (blk + 1) * PROJ_CHUNKS * CHUNK)

            def per_chunk(t, table):
                return jnp.concatenate([t[n * CHUNK:(n + 1) * CHUNK] * table for n in range(PROJ_CHUNKS)], axis=0)

            x = x_ref[rows, :]
            r = lax.rsqrt(jnp.mean(x * x, axis=-1, keepdims=True) + RMS_EPS)
            hx = ((x * r) * (n1_ref[...] * (1.0 + sc_ref[...])) + sh_ref[...]).astype(_BF16)

            def proj(w_ref, lo, width):
                return _dot(hx, w_ref[:, lo:lo + width])

            yield
            out.add(f32_ref, (slot, rows, slice(U, U + a_width)), _gelu_tanh(proj(wuv_ref, 0, a_width)))
            v = _gelu_tanh(proj(wuv_ref, a_width, a_width))
            vg = jnp.concatenate([(_rms(_head(v, g)) * _head(gain_ref[...], g)).astype(_BF16)
                                  for g in range(GROUPS)], axis=-1)
            out.add(b16_ref, (slot, rows, slice(VG, VG + a_width)), vg)
            yield

            def rope(t, table):
                cos = rope_ref[rows, 2 * table * HEAD_DIM:(2 * table + 1) * HEAD_DIM]
                sin = rope_ref[rows, (2 * table + 1) * HEAD_DIM:(2 * table + 2) * HEAD_DIM]
                return jnp.concatenate(
                    [_head(t, h) * cos + pltpu.roll(_head(t, h), HEAD_DIM // 2, 1) * sin for h in range(HEADS)],
                    axis=-1)

            q16 = rope(proj(wqk_ref, 0, r_width), 0).astype(_BF16)
            k16 = rope(proj(wqk_ref, r_width, r_width), 1).astype(_BF16)
            out.add(b16_ref, (slot, rows, slice(K, K + r_width)), k16)
            out.add(qkvg_ref, (rows, slice(0, r_width)), q16)
            out.add(qkvg_ref, (rows, slice(r_width, 2 * r_width)), k16)
            yield
            vr = proj(wvr_ref, 0, r_width)
            out.add(b16_ref, (slot, rows, slice(VZ, VZ + r_width)), per_chunk(vr, tab_ref[T_ZETA_F]).astype(_BF16))
            out.add(qkvg_ref, (rows, slice(2 * r_width, 3 * r_width)), vr.astype(_BF16))
            out.add(qkvg_ref, (rows, slice(3 * r_width, 4 * r_width)), proj(wgb_ref, 0, r_width).astype(_BF16))
            out.add(qkvg_ref, (rows, slice(4 * r_width, 5 * r_width)), proj(wgf_ref, 0, r_width).astype(_BF16))

    def gate_mix(slot, out):
        mixed = []
        for g in range(GROUPS):
            vg = jnp.concatenate([b16_ref[slot, n * CHUNK:(n + 1) * CHUNK, VG + g * LANES:VG + (g + 1) * LANES]
                                  for n in range(n_chunks)], axis=1)
            mixed.append(_dot(sgw_ref[g].astype(_BF16), vg))
        yield
        for g in range(GROUPS):
            cols = slice(g * LANES, (g + 1) * LANES)
            for n in range(n_chunks):
                rows = slice(n * CHUNK, (n + 1) * CHUNK)
                m = mixed[g][:, n * LANES:(n + 1) * LANES] + sgb_ref[:, cols]
                u = f32_ref[slot, rows, U + g * LANES:U + (g + 1) * LANES]
                out.add(ya_ref, (rows, cols), (u * m).astype(_BF16))

    def scan(slot, out):
        upd = {(n, h): _dot_tn(b16_ref[slot, n * CHUNK:(n + 1) * CHUNK, K + h * HEAD_DIM:K + (h + 1) * HEAD_DIM],
                               b16_ref[slot, n * CHUNK:(n + 1) * CHUNK, VZ + h * HEAD_DIM:VZ + (h + 1) * HEAD_DIM])
               for n in range(n_chunks) for h in range(HEADS)}
        yield
        for h in range(HEADS):
            state = s_ref[h]
            for n in range(n_chunks):
                out.add(sfin_ref, (n, h), state.astype(_BF16))
                state = _head(tab_ref[T_DECAY_F], h) * state + upd[n, h]
            out.add(s_ref, h, state)

    def first(slot):
        s_fwd, s_bwd = _context_states(ctx_ref[...], csh_ref[...], csc_ref[...], n1_ref[...],
                                       wqk_ref[:, r_width:], wvr_ref[...], tab_ref, k_scale)
        for h in range(HEADS):
            s_ref[h] = s_fwd[h]
            scb_ref[h] = s_bwd[h]
        out = _DeferredStores()
        convert(out)
        _interleave(project(slot, out))
        out.commit()

    def middle(slot):
        out = _DeferredStores()
        convert(out)
        _interleave(scan(1 - slot, out), gate_mix(1 - slot, out), project(slot, out))
        out.commit()

    def last(slot):
        out = _DeferredStores()
        convert(out)
        _interleave(scan(slot, out), gate_mix(slot, out))
        out.commit()

    _skewed_steps(first, middle, last)


def _forward_pass(x, ctx, mod3, norm1, w16, sg_gain, sg_w, sg_bias, rope, tables, bwd_weights, tile, k_scale):
    b, l, d = x.shape
    nt = l // tile
    a_width, r_width = GROUPS * LANES, HEADS * HEAD_DIM
    ctx_row = b
    state_spec = pl.BlockSpec((None, HEADS, HEAD_DIM, HEAD_DIM), lambda i, s: (i, 0, 0, 0))
    const2 = lambda i, s: (0, 0)
    cur = lambda i, s: (i, jnp.minimum(s, nt - 1), 0)
    prev = lambda i, s: (i, jnp.maximum(s - 1, 0), 0)
    resident = dict(pipeline_mode=pl.Buffered(1))
    n_blk = max(n for n in (1, 2, 4, 8, 16) if n <= b * (nt + 1))
    blk = lambda i, s: (jnp.minimum(i * (nt + 1) + s, n_blk - 1), 0)
    w_specs = [pl.BlockSpec((w.shape[0] // n_blk, w.shape[1]), blk) for w in bwd_weights]
    return pl.pallas_call(
        functools.partial(_fwd_kernel, n_chunks=tile // CHUNK, k_scale=k_scale),
        grid=(b, nt + 1),
        in_specs=[pl.BlockSpec((None, tile, d), cur),
                  pl.BlockSpec((None, 1, d), lambda i, s: (i, 0, 0)),
                  pl.BlockSpec((None, 1, d), lambda i, s: (i, 0, 1)),
                  pl.BlockSpec((1, d), const2),
                  pl.BlockSpec((d, 2 * r_width), lambda i, s: (0, U_BLOCK // 2), **resident),
                  pl.BlockSpec((d, 2 * r_width), lambda i, s: (0, Q_BLOCK // 2), **resident),
                  pl.BlockSpec((d, r_width), lambda i, s: (0, VR_BLOCK), **resident),
                  pl.BlockSpec((d, r_width), lambda i, s: (0, GF_BLOCK), **resident),
                  pl.BlockSpec((d, r_width), lambda i, s: (0, GB_BLOCK), **resident),
                  pl.BlockSpec((1, a_width), const2),
                  pl.BlockSpec(sg_w.shape, lambda i, s: (0, 0, 0)),
                  pl.BlockSpec(sg_bias.shape, const2),
                  pl.BlockSpec((tile, rope.shape[1]), lambda i, s: (jnp.minimum(s, nt - 1), 0)),
                  pl.BlockSpec(tables.shape, lambda i, s: (0, 0, 0)),
                  pl.BlockSpec((None,) + ctx.shape[1:], lambda i, s: (i, 0, 0)),
                  pl.BlockSpec((None, 1, d), lambda i, s: (ctx_row, 0, 0)),
                  pl.BlockSpec((None, 1, d), lambda i, s: (ctx_row, 0, 1))]
                 + w_specs,
        out_specs=[pl.BlockSpec((None, tile, a_width), prev),
                   pl.BlockSpec((None, tile, 5 * r_width), cur),
                   pl.BlockSpec((None, tile // CHUNK, HEADS, HEAD_DIM, HEAD_DIM),
                                lambda i, s: (i, jnp.maximum(s - 1, 0), 0, 0, 0)),
                   state_spec] + w_specs,
        out_shape=[jax.ShapeDtypeStruct((b, l, a_width), _BF16),
                   jax.ShapeDtypeStruct((b, l, 5 * r_width), _BF16),
                   jax.ShapeDtypeStruct((b, l // CHUNK, HEADS, HEAD_DIM, HEAD_DIM), _BF16),
                   jax.ShapeDtypeStruct((b, HEADS, HEAD_DIM, HEAD_DIM), _F32)]
                  + [jax.ShapeDtypeStruct(w.shape, _BF16) for w in bwd_weights],
        scratch_shapes=[pltpu.VMEM((HEADS, HEAD_DIM, HEAD_DIM), _F32),
                        pltpu.VMEM((2, tile, a_width), _F32),
                        pltpu.VMEM((2, tile, 3 * r_width), _BF16)],
        compiler_params=pltpu.CompilerParams(dimension_semantics=("arbitrary", "arbitrary"),
                                             vmem_limit_bytes=VMEM_LIMIT_BYTES),
        name="fwd",
    )(x, mod3, mod3, norm1, w16, w16, w16, w16, w16, sg_gain, sg_w, sg_bias, rope, tables, ctx, mod3, mod3,
      *bwd_weights)


def _bwd_kernel(x_ref, ya_ref, qkvg_ref, sfin_ref, g1_ref, sh_ref, sc_ref, g2_ref, n2_ref, nf_ref,
                wout_hbm, wg_hbm, wu_hbm, wd_hbm, tab_ref, s0_ref, o_ref, s_ref, yb_ref,
                wout_ref, wg_ref, wu_ref, wd_ref, w_sem, *, n_chunks):
    a_width = GROUPS * LANES
    r_width = HEADS * HEAD_DIM
    weight_copies = [pltpu.make_async_copy(src, dst, w_sem.at[i]) for i, (src, dst) in enumerate(
        ((wout_hbm, wout_ref), (wg_hbm, wg_ref), (wu_hbm, wu_ref), (wd_hbm, wd_ref)))]
    very_first_batch = pl.program_id(0) == 0

    @pl.when(very_first_batch & (pl.program_id(1) == 0))
    def _():
        for copy in weight_copies:
            copy.start()

    @pl.when(very_first_batch & (pl.program_id(1) == 1))
    def _():
        for copy in weight_copies:
            copy.wait()

    def retain(slot, out):
        def load(n):
            rows = slice(n * CHUNK, (n + 1) * CHUNK)
            q, k, v, gate_b, gate_f = (qkvg_ref[rows, i * r_width:(i + 1) * r_width] for i in range(5))
            q32, v32 = q.astype(_F32), v.astype(_F32)
            return dict(q=q, k=k, v=v, gate_f=gate_f.astype(_F32), gate_b=gate_b.astype(_F32),
                        xq_f=(q32 * tab_ref[T_XI_F]).astype(_BF16), xq_b=(q32 * tab_ref[T_XI_B]).astype(_BF16),
                        vz_b=(v32 * tab_ref[T_ZETA_B]).astype(_BF16),
                        s_f=[sfin_ref[n, h] for h in range(HEADS)])

        def emit(n, y):
            out.add(yb_ref, (slot, slice(n * CHUNK, (n + 1) * CHUNK), slice(None)), y)

        return _retention_stages(load, list(reversed(range(n_chunks))), BWD_RETAIN_GROUPS, s_ref, tab_ref, emit, out)

    def finish(slot, out):
        y = _dot(ya_ref[...], wout_ref[:a_width, :]) + _dot(yb_ref[slot].astype(_BF16), wout_ref[a_width:, :])
        x1 = x_ref[...] + g1_ref[...] * y
        h2 = ((_rms(x1) * n2_ref[...]) * (1.0 + sc_ref[...]) + sh_ref[...]).astype(_BF16)
        acc = None
        d_ff = wg_ref.shape[1]
        for lo in range(0, d_ff, FFN_BLOCK):
            yield
            cols = slice(lo, min(lo + FFN_BLOCK, d_ff))
            act = (_silu(_dot(h2, wg_ref[:, cols])) * _dot(h2, wu_ref[:, cols])).astype(_BF16)
            part = _dot(act, wd_ref[cols, :])
            acc = part if acc is None else acc + part
        x2 = x1 + g2_ref[...] * acc
        out.add(o_ref, (slice(None), slice(None)), _rms(x2) * nf_ref[...])

    def first(slot):
        s_ref[...] = s0_ref[...]
        out = _DeferredStores()
        _interleave(*retain(slot, out))
        out.commit()

    def middle(slot):
        out = _DeferredStores()
        _interleave(*retain(slot, out), finish(1 - slot, out))
        out.commit()

    def last(slot):
        out = _DeferredStores()
        _interleave(finish(slot, out))
        out.commit()

    _skewed_steps(first, middle, last)


def _backward_pass(x, ya, qkvg, sfin, mod3, norm2, norm_f, w_out, w_gate, w_up, w_down, tables, s_cb, tile):
    b, l, d = x.shape
    nt = l // tile
    const2 = lambda i, s: (0, 0)
    cur = lambda i, s: (i, nt - 1 - jnp.minimum(s, nt - 1), 0)
    prev = lambda i, s: (i, nt - 1 - jnp.maximum(s - 1, 0), 0)
    in_hbm = pl.BlockSpec(memory_space=pl.ANY)
    mod_spec = lambda j: pl.BlockSpec((None, 1, d), lambda i, s: (i, 0, j))
    return pl.pallas_call(
        functools.partial(_bwd_kernel, n_chunks=tile // CHUNK),
        grid=(b, nt + 1),
        in_specs=[pl.BlockSpec((None, tile, d), prev),
                  pl.BlockSpec((None, tile, ya.shape[-1]), prev),
                  pl.BlockSpec((None, tile, qkvg.shape[-1]), cur),
                  pl.BlockSpec((None, tile // CHUNK, HEADS, HEAD_DIM, HEAD_DIM),
                               lambda i, s: (i, nt - 1 - jnp.minimum(s, nt - 1), 0, 0, 0)),
                  mod_spec(2), mod_spec(3), mod_spec(4), mod_spec(5),
                  pl.BlockSpec((1, d), const2),
                  pl.BlockSpec((1, d), const2),
                  in_hbm, in_hbm, in_hbm, in_hbm,
                  pl.BlockSpec(tables.shape, lambda i, s: (0, 0, 0)),
                  pl.BlockSpec((None, HEADS, HEAD_DIM, HEAD_DIM), lambda i, s: (i, 0, 0, 0))],
        out_specs=pl.BlockSpec((None, tile, d), prev),
        out_shape=jax.ShapeDtypeStruct((b, l, d), x.dtype),
        scratch_shapes=[pltpu.VMEM((HEADS, HEAD_DIM, HEAD_DIM), _F32),
                        pltpu.VMEM((2, tile, HEADS * HEAD_DIM), _F32),
                        pltpu.VMEM(w_out.shape, w_out.dtype),
                        pltpu.VMEM(w_gate.shape, w_gate.dtype),
                        pltpu.VMEM(w_up.shape, w_up.dtype),
                        pltpu.VMEM(w_down.shape, w_down.dtype),
                        pltpu.SemaphoreType.DMA((4,))],
        compiler_params=pltpu.CompilerParams(dimension_semantics=("arbitrary", "arbitrary"),
                                             vmem_limit_bytes=VMEM_LIMIT_BYTES),
        name="bwd",
    )(x, ya, qkvg, sfin, mod3, mod3, mod3, mod3, norm2, norm_f, w_out, w_gate, w_up, w_down, tables, s_cb)


def _rope_tables(l, k_scale):
    n_freq = HEAD_DIM // 4
    inv = np.float32(ROPE_BASE) ** (-np.arange(n_freq, dtype=np.float32) / np.float32(n_freq))
    rows = np.repeat(np.arange(l // GRID_W, dtype=np.float32), GRID_W)
    cols = np.tile(np.arange(GRID_W, dtype=np.float32), l // GRID_W)
    ang_r, ang_c = rows[:, None] * inv[None, :], cols[:, None] * inv[None, :]
    cos = np.concatenate([np.cos(ang_r), np.cos(ang_c)], axis=-1)
    sin = np.concatenate([np.sin(ang_r), np.sin(ang_c)], axis=-1)
    cos2, sin2 = np.concatenate([cos, cos], axis=-1), np.concatenate([-sin, sin], axis=-1)
    return np.concatenate([cos2, sin2, cos2 * k_scale, sin2 * k_scale], axis=-1).astype(np.float32)


def kernel(x, c, ctx, c_ctx, w_mod, b_mod, norm1, w_in, sg_gain, sg_w, sg_b, ret_logit_f, ret_logit_b,
           w_out, norm2, w_gate, w_up, w_down, norm_f):
    b, l, d = x.shape
    a_width, r_width = GROUPS * LANES, HEADS * HEAD_DIM
    tile = min(TILE, l)
    assert w_mod.shape[0] == 1 and b + 1 <= MOD_ROWS and l % tile == 0 and tile % CHUNK == 0
    assert ctx.shape[1] % CHUNK == 0 and l // tile >= 2
    assert a_width == r_width and d == a_width + r_width
    k_scale = HEAD_DIM ** -0.5

    mod3, w16, tables = _prepare(c, c_ctx, w_mod[0], b_mod[0][None, :], w_in[0], ret_logit_f[0], ret_logit_b[0])
    sg_bias = jnp.broadcast_to(jnp.transpose(sg_b[0])[:, :, None], (CHUNK, GROUPS, LANES)).reshape(CHUNK, a_width)
    ya, qkvg, sfin, s_cb, w_out_b, w_gate_b, w_up_b, w_down_b = _forward_pass(
        x, ctx, mod3, norm1, w16, sg_gain, sg_w[0], sg_bias, _rope_tables(l, k_scale),
        tables, (w_out[0], w_gate[0], w_up[0], w_down[0]), tile, k_scale)
    return _backward_pass(x, ya, qkvg, sfin, mod3, norm2, norm_f[None, :], w_out_b, w_gate_b, w_up_b, w_down_b,
                          tables, s_cb, BWD_TILE)
```

```python
import functools

import numpy as np
import jax
import jax.numpy as jnp
from jax import lax
from jax.experimental import pallas as pl
from jax.experimental.pallas import tpu as pltpu

CHUNK = 128
GRID_W = 64
GROUPS = 4
HEADS = 4
HEAD_DIM = 128
RMS_EPS = 1e-6
ROPE_BASE = 10000.0
LANES = 128
MOD_ROWS = 8
TILE = 512
BWD_TILE = 256
PROJ_CHUNKS = 1
BWD_RETAIN_GROUPS = 2
FFN_BLOCK = 1024
VMEM_LIMIT_BYTES = 56 * 1024 * 1024

IN_BLOCKS = 7
U_BLOCK, V_BLOCK, Q_BLOCK, K_BLOCK, VR_BLOCK, GF_BLOCK, GB_BLOCK = range(IN_BLOCKS)

T_DMASK_F, T_DMASK_B, T_XI_F, T_XI_B, T_ZETA_F, T_ZETA_B, T_DECAY_F, T_DECAY_B = range(8)

_BF16 = jnp.bfloat16
_F32 = jnp.float32


def _dot(a, b):
    return jnp.dot(a, b, preferred_element_type=_F32)


def _dot_nt(a, b):
    return lax.dot_general(a, b, (((1,), (1,)), ((), ())), preferred_element_type=_F32)


def _dot_tn(a, b):
    return lax.dot_general(a, b, (((0,), (0,)), ((), ())), preferred_element_type=_F32)


def _silu(x):
    return x * (1.0 / (1.0 + jnp.exp(-x)))


def _gelu_tanh(x):
    c = float(np.sqrt(2.0 / np.pi))
    half = 0.5 * x
    return half + half * jnp.tanh(x * (c + (0.044715 * c) * (x * x)))


def _rms(x):
    return x * lax.rsqrt(jnp.mean(x * x, axis=-1, keepdims=True) + RMS_EPS)


def _head(a, h):
    return a[:, h * HEAD_DIM:(h + 1) * HEAD_DIM]


def _prep_kernel(c_ref, cctx_ref, wmod_ref, bmod_ref, win_ref, lf_ref, lb_ref, mod_ref, w16_ref, tab_ref):
    j = pl.program_id(0)

    n_batch, d = c_ref.shape
    cc = jnp.concatenate([c_ref[...], cctx_ref[...], jnp.zeros((MOD_ROWS - n_batch - 1, d), _F32)], axis=0)
    mod = _dot(_silu(cc), wmod_ref[...]) + bmod_ref[...]
    for r in range(MOD_ROWS):
        mod_ref[r] = mod[r:r + 1, :]

    def reorder_head_dims(t):
        quarter = HEAD_DIM // 4
        lane = lax.broadcasted_iota(jnp.int32, (t.shape[0], HEAD_DIM), 1)
        heads = []
        for h in range(t.shape[1] // HEAD_DIM):
            th = _head(t, h)
            from_right = pltpu.roll(th, HEAD_DIM - quarter, 1)
            from_left = pltpu.roll(th, quarter, 1)
            heads.append(jnp.where((lane >= quarter) & (lane < 2 * quarter), from_right,
                                   jnp.where((lane >= 2 * quarter) & (lane < 3 * quarter), from_left, th)))
        return jnp.concatenate(heads, axis=-1)

    is_qk = (j == Q_BLOCK) | (j == K_BLOCK)

    @pl.when(is_qk)
    def _():
        w16_ref[...] = reorder_head_dims(win_ref[...]).astype(_BF16)

    @pl.when(jnp.logical_not(is_qk))
    def _():
        w16_ref[...] = win_ref[...].astype(_BF16)

    @pl.when(j == 0)
    def _():
        def log_sigmoid(x):
            return -(jnp.maximum(-x, 0.0) + jnp.log1p(jnp.exp(-jnp.abs(x))))

        shape = (CHUNK, HEADS * HEAD_DIM)
        head = lax.broadcasted_iota(jnp.int32, shape, 1) // HEAD_DIM

        def per_head(logit_ref):
            t = jnp.zeros(shape, _F32)
            for h in range(HEADS):
                t = jnp.where(head == h, logit_ref[h], t)
            return t

        lg_f = log_sigmoid(per_head(lf_ref))
        lg_b = log_sigmoid(per_head(lb_ref))
        i = lax.broadcasted_iota(jnp.int32, shape, 0).astype(_F32)
        jj = (lax.broadcasted_iota(jnp.int32, shape, 1) & (HEAD_DIM - 1)).astype(_F32)
        tab_ref[T_DMASK_F] = jnp.where(i >= jj, jnp.exp(lg_f * jnp.maximum(i - jj, 0.0)), 0.0)
        tab_ref[T_DMASK_B] = jnp.where(jj >= i, jnp.exp(lg_b * jnp.maximum(jj - i, 0.0)), 0.0)
        tab_ref[T_XI_F] = jnp.exp(lg_f * (i + 1.0))
        tab_ref[T_XI_B] = jnp.exp(lg_b * (CHUNK - i))
        tab_ref[T_ZETA_F] = jnp.exp(lg_f * (CHUNK - 1.0 - i))
        tab_ref[T_ZETA_B] = jnp.exp(lg_b * i)
        tab_ref[T_DECAY_F] = jnp.exp(lg_f * CHUNK)
        tab_ref[T_DECAY_B] = jnp.exp(lg_b * CHUNK)


def _prepare(c, c_ctx, w_mod, b_mod, w_in, logit_f, logit_b):
    d, n_mod = w_mod.shape
    r_width = HEADS * HEAD_DIM
    n_mod_blocks = n_mod // d
    assert w_in.shape == (d, IN_BLOCKS * r_width) and n_mod_blocks <= IN_BLOCKS
    mod_blk = lambda j: jnp.minimum(j, n_mod_blocks - 1)
    return pl.pallas_call(
        _prep_kernel,
        grid=(IN_BLOCKS,),
        in_specs=[pl.BlockSpec(c.shape, lambda j: (0, 0)),
                  pl.BlockSpec((1, d), lambda j: (0, 0)),
                  pl.BlockSpec((d, d), lambda j: (0, mod_blk(j))),
                  pl.BlockSpec((1, d), lambda j: (0, mod_blk(j))),
                  pl.BlockSpec((d, r_width), lambda j: (0, j)),
                  pl.BlockSpec(memory_space=pltpu.SMEM),
                  pl.BlockSpec(memory_space=pltpu.SMEM)],
        out_specs=[pl.BlockSpec((MOD_ROWS, 1, d), lambda j: (0, 0, mod_blk(j))),
                   pl.BlockSpec((d, r_width), lambda j: (0, j)),
                   pl.BlockSpec((8, CHUNK, r_width), lambda j: (0, 0, 0))],
        out_shape=[jax.ShapeDtypeStruct((MOD_ROWS, 1, n_mod), _F32),
                   jax.ShapeDtypeStruct(w_in.shape, _BF16),
                   jax.ShapeDtypeStruct((8, CHUNK, r_width), _F32)],
        compiler_params=pltpu.CompilerParams(dimension_semantics=("arbitrary",),
                                             vmem_limit_bytes=VMEM_LIMIT_BYTES),
        name="prep",
    )(c, c_ctx[None, :], w_mod, b_mod, w_in, logit_f.astype(_F32), logit_b.astype(_F32))


def _context_states(ctx, shift, scale, gain, w_k, w_v, tab_ref, k_scale):
    hc = ((_rms(ctx) * gain) * (1.0 + scale) + shift).astype(_BF16)
    k = (_dot(hc, w_k) * k_scale).astype(_BF16)
    v = _dot(hc, w_v)
    n_chunks = ctx.shape[0] // CHUNK
    s_fwd, s_bwd = [], []
    for h in range(HEADS):
        s_f = jnp.zeros((HEAD_DIM, HEAD_DIM), _F32)
        s_b = jnp.zeros((HEAD_DIM, HEAD_DIM), _F32)
        for n in range(n_chunks):
            rows = slice(n * CHUNK, (n + 1) * CHUNK)
            upd = _dot_tn(_head(k[rows], h), (_head(v[rows], h) * _head(tab_ref[T_ZETA_F], h)).astype(_BF16))
            s_f = _head(tab_ref[T_DECAY_F], h) * s_f + upd
        for n in reversed(range(n_chunks)):
            rows = slice(n * CHUNK, (n + 1) * CHUNK)
            upd = _dot_tn(_head(k[rows], h), (_head(v[rows], h) * _head(tab_ref[T_ZETA_B], h)).astype(_BF16))
            s_b = _head(tab_ref[T_DECAY_B], h) * s_b + upd
        s_fwd.append(s_f)
        s_bwd.append(s_b)
    return s_fwd, s_bwd


def _retention_tile(load, chunk_order, state, tab_ref, emit):
    keys = [(n, h) for n in chunk_order for h in range(HEADS)]
    c = {n: load(n) for n in chunk_order}
    scores = {(n, h): _dot_nt(_head(c[n]["q"], h), _head(c[n]["k"], h)) for n, h in keys}
    upd = {(n, h): _dot_tn(_head(c[n]["k"], h), _head(c[n]["vz_b"], h)) for n, h in keys}
    yield
    lhs_f, lhs_b, rhs_f, rhs_b = {}, {}, {}, {}
    for n, h in keys:
        v = _head(c[n]["v"], h)
        lhs_f[n, h] = jnp.concatenate([(scores[n, h] * _head(tab_ref[T_DMASK_F], h)).astype(_BF16),
                                       _head(c[n]["xq_f"], h)], axis=1)
        lhs_b[n, h] = jnp.concatenate([(scores[n, h] * _head(tab_ref[T_DMASK_B], h)).astype(_BF16),
                                       _head(c[n]["xq_b"], h)], axis=1)
        rhs_f[n, h] = jnp.concatenate([v, c[n]["s_f"][h]], axis=0)
        rhs_b[n, h] = jnp.concatenate([v, state[h].astype(_BF16)], axis=0)
        state[h] = _head(tab_ref[T_DECAY_B], h) * state[h] + upd[n, h]
    yield
    o_f = {key: _dot(lhs_f[key], rhs_f[key]) for key in keys}
    o_b = {key: _dot(lhs_b[key], rhs_b[key]) for key in keys}
    yield
    for n in chunk_order:
        emit(n, jnp.concatenate([_silu(_head(c[n]["gate_f"], h)) * _rms(o_f[n, h])
                                 + _silu(_head(c[n]["gate_b"], h)) * _rms(o_b[n, h]) for h in range(HEADS)],
                                axis=-1))


def _retention_stages(load, chunk_order, n_groups, s_ref, tab_ref, emit, out):
    state = [None] * HEADS
    order = list(chunk_order)
    per = len(order) // n_groups
    groups = [order[i * per:(i + 1) * per] for i in range(n_groups)]

    def group_stage(i):
        if i:
            for _ in range(i):
                yield
        inner = _retention_tile(load, groups[i], state, tab_ref, emit)
        next(inner)
        yield
        if i == 0:
            for h in range(HEADS):
                state[h] = s_ref[h]
        next(inner)
        yield
        for _ in inner:
            yield
        if i == n_groups - 1:
            for h in range(HEADS):
                out.add(s_ref, h, state[h])

    return [group_stage(i) for i in range(n_groups)]


def _interleave(*stages):
    stages = list(stages)
    while stages:
        for g in list(stages):
            try:
                next(g)
            except StopIteration:
                stages.remove(g)


class _DeferredStores:
    def __init__(self):
        self._items = []

    def add(self, ref, idx, value):
        self._items.append((ref, idx, value))

    def commit(self):
        for ref, idx, value in self._items:
            ref[idx] = value


def _skewed_steps(first, middle, last, axis=1):
    s = pl.program_id(axis)
    n_tiles = pl.num_programs(axis) - 1

    @pl.when(s == 0)
    def _():
        first(0)

    @pl.when((s > 0) & (s < n_tiles))
    def _():
        middle(s & 1)

    @pl.when(s == n_tiles)
    def _():
        last((s - 1) & 1)


def _fwd_kernel(x_ref, sh_ref, sc_ref, n1_ref, wuv_ref, wqk_ref, wvr_ref, wgf_ref, wgb_ref, gain_ref, sgw_ref, sgb_ref,
                rope_ref, tab_ref, ctx_ref, csh_ref, csc_ref, wo32_ref, wg32_ref, wu32_ref, wd32_ref,
                ya_ref, qkvg_ref, sfin_ref, scb_ref, wo16_ref, wg16_ref, wu16_ref, wd16_ref, s_ref, f32_ref, b16_ref, *,
                n_chunks, k_scale):
    a_width = GROUPS * LANES
    r_width = HEADS * HEAD_DIM
    U = 0
    VG, K, VZ = (i * r_width for i in range(3))
    rows_all = slice(None)

    def convert(out):
        for src, dst in ((wo32_ref, wo16_ref), (wg32_ref, wg16_ref), (wu32_ref, wu16_ref), (wd32_ref, wd16_ref)):
            out.add(dst, (rows_all, rows_all), src[...].astype(_BF16))

    def project(slot, out):
        n_sub = n_chunks // PROJ_CHUNKS
        for blk in range(n_sub):
            rows = slice(blk * PROJ_CHUNKS * CHUNK, (blk + 1) * PROJ_CHUNKS * CHUNK)

            def per_chunk(t, table):
                return jnp.concatenate([t[n * CHUNK:(n + 1) * CHUNK] * table for n in range(PROJ_CHUNKS)], axis=0)

            x = x_ref[rows, :]
            r = lax.rsqrt(jnp.mean(x * x, axis=-1, keepdims=True) + RMS_EPS)
            hx = ((x * r) * (n1_ref[...] * (1.0 + sc_ref[...])) + sh_ref[...]).astype(_BF16)

            def proj(w_ref, lo, width):
                return _dot(hx, w_ref[:, lo:lo + width])

            yield
            out.add(f32_ref, (slot, rows, slice(U, U + a_width)), _gelu_tanh(proj(wuv_ref, 0, a_width)))
            v = _gelu_tanh(proj(wuv_ref, a_width, a_width))
            vg = jnp.concatenate([(_rms(_head(v, g)) * _head(gain_ref[...], g)).astype(_BF16)
                                  for g in range(GROUPS)], axis=-1)
            out.add(b16_ref, (slot, rows, slice(VG, VG + a_width)), vg)
            yield

            def rope(t, table):
                cos = rope_ref[rows, 2 * table * HEAD_DIM:(2 * table + 1) * HEAD_DIM]
                sin = rope_ref[rows, (2 * table + 1) * HEAD_DIM:(2 * table + 2) * HEAD_DIM]
                return jnp.concatenate(
                    [_head(t, h) * cos + pltpu.roll(_head(t, h), HEAD_DIM // 2, 1) * sin for h in range(HEADS)],
                    axis=-1)

            q16 = rope(proj(wqk_ref, 0, r_width), 0).astype(_BF16)
            k16 = rope(proj(wqk_ref, r_width, r_width), 1).astype(_BF16)
            out.add(b16_ref, (slot, rows, slice(K, K + r_width)), k16)
            out.add(qkvg_ref, (rows, slice(0, r_width)), q16)
            out.add(qkvg_ref, (rows, slice(r_width, 2 * r_width)), k16)
            yield
            vr = proj(wvr_ref, 0, r_width)
            out.add(b16_ref, (slot, rows, slice(VZ, VZ + r_width)), per_chunk(vr, tab_ref[T_ZETA_F]).astype(_BF16))
            out.add(qkvg_ref, (rows, slice(2 * r_width, 3 * r_width)), vr.astype(_BF16))
            out.add(qkvg_ref, (rows, slice(3 * r_width, 4 * r_width)), proj(wgb_ref, 0, r_width).astype(_BF16))
            out.add(qkvg_ref, (rows, slice(4 * r_width, 5 * r_width)), proj(wgf_ref, 0, r_width).astype(_BF16))

    def gate_mix(slot, out):
        mixed = []
        for g in range(GROUPS):
            vg = jnp.concatenate([b16_ref[slot, n * CHUNK:(n + 1) * CHUNK, VG + g * LANES:VG + (g + 1) * LANES]
                                  for n in range(n_chunks)], axis=1)
            mixed.append(_dot(sgw_ref[g].astype(_BF16), vg))
        yield
        for g in range(GROUPS):
            cols = slice(g * LANES, (g + 1) * LANES)
            for n in range(n_chunks):
                rows = slice(n * CHUNK, (n + 1) * CHUNK)
                m = mixed[g][:, n * LANES:(n + 1) * LANES] + sgb_ref[:, cols]
                u = f32_ref[slot, rows, U + g * LANES:U + (g + 1) * LANES]
                out.add(ya_ref, (rows, cols), (u * m).astype(_BF16))

    def scan(slot, out):
        upd = {(n, h): _dot_tn(b16_ref[slot, n * CHUNK:(n + 1) * CHUNK, K + h * HEAD_DIM:K + (h + 1) * HEAD_DIM],
                               b16_ref[slot, n * CHUNK:(n + 1) * CHUNK, VZ + h * HEAD_DIM:VZ + (h + 1) * HEAD_DIM])
               for n in range(n_chunks) for h in range(HEADS)}
        yield
        for h in range(HEADS):
            state = s_ref[h]
            for n in range(n_chunks):
                out.add(sfin_ref, (n, h), state.astype(_BF16))
                state = _head(tab_ref[T_DECAY_F], h) * state + upd[n, h]
            out.add(s_ref, h, state)

    def first(slot):
        s_fwd, s_bwd = _context_states(ctx_ref[...], csh_ref[...], csc_ref[...], n1_ref[...],
                                       wqk_ref[:, r_width:], wvr_ref[...], tab_ref, k_scale)
        for h in range(HEADS):
            s_ref[h] = s_fwd[h]
            scb_ref[h] = s_bwd[h]
        out = _DeferredStores()
        convert(out)
        _interleave(project(slot, out))
        out.commit()

    def middle(slot):
        out = _DeferredStores()
        convert(out)
        _interleave(scan(1 - slot, out), gate_mix(1 - slot, out), project(slot, out))
        out.commit()

    def last(slot):
        out = _DeferredStores()
        convert(out)
        _interleave(scan(slot, out), gate_mix(slot, out))
        out.commit()

    _skewed_steps(first, middle, last)


def _forward_pass(x, ctx, mod3, norm1, w16, sg_gain, sg_w, sg_bias, rope, tables, bwd_weights, tile, k_scale):
    b, l, d = x.shape
    nt = l // tile
    a_width, r_width = GROUPS * LANES, HEADS * HEAD_DIM
    ctx_row = b
    state_spec = pl.BlockSpec((None, HEADS, HEAD_DIM, HEAD_DIM), lambda i, s: (i, 0, 0, 0))
    const2 = lambda i, s: (0, 0)
    cur = lambda i, s: (i, jnp.minimum(s, nt - 1), 0)
    prev = lambda i, s: (i, jnp.maximum(s - 1, 0), 0)
    resident = dict(pipeline_mode=pl.Buffered(1))
    n_blk = max(n for n in (1, 2, 4, 8, 16) if n <= b * (nt + 1))
    blk = lambda i, s: (jnp.minimum(i * (nt + 1) + s, n_blk - 1), 0)
    w_specs = [pl.BlockSpec((w.shape[0] // n_blk, w.shape[1]), blk) for w in bwd_weights]
    return pl.pallas_call(
        functools.partial(_fwd_kernel, n_chunks=tile // CHUNK, k_scale=k_scale),
        grid=(b, nt + 1),
        in_specs=[pl.BlockSpec((None, tile, d), cur),
                  pl.BlockSpec((None, 1, d), lambda i, s: (i, 0, 0)),
                  pl.BlockSpec((None, 1, d), lambda i, s: (i, 0, 1)),
                  pl.BlockSpec((1, d), const2),
                  pl.BlockSpec((d, 2 * r_width), lambda i, s: (0, U_BLOCK // 2), **resident),
                  pl.BlockSpec((d, 2 * r_width), lambda i, s: (0, Q_BLOCK // 2), **resident),
                  pl.BlockSpec((d, r_width), lambda i, s: (0, VR_BLOCK), **resident),
                  pl.BlockSpec((d, r_width), lambda i, s: (0, GF_BLOCK), **resident),
                  pl.BlockSpec((d, r_width), lambda i, s: (0, GB_BLOCK), **resident),
                  pl.BlockSpec((1, a_width), const2),
                  pl.BlockSpec(sg_w.shape, lambda i, s: (0, 0, 0)),
                  pl.BlockSpec(sg_bias.shape, const2),
                  pl.BlockSpec((tile, rope.shape[1]), lambda i, s: (jnp.minimum(s, nt - 1), 0)),
                  pl.BlockSpec(tables.shape, lambda i, s: (0, 0, 0)),
                  pl.BlockSpec((None,) + ctx.shape[1:], lambda i, s: (i, 0, 0)),
                  pl.BlockSpec((None, 1, d), lambda i, s: (ctx_row, 0, 0)),
                  pl.BlockSpec((None, 1, d), lambda i, s: (ctx_row, 0, 1))]
                 + w_specs,
        out_specs=[pl.BlockSpec((None, tile, a_width), prev),
                   pl.BlockSpec((None, tile, 5 * r_width), cur),
                   pl.BlockSpec((None, tile // CHUNK, HEADS, HEAD_DIM, HEAD_DIM),
                                lambda i, s: (i, jnp.maximum(s - 1, 0), 0, 0, 0)),
                   state_spec] + w_specs,
        out_shape=[jax.ShapeDtypeStruct((b, l, a_width), _BF16),
                   jax.ShapeDtypeStruct((b, l, 5 * r_width), _BF16),
                   jax.ShapeDtypeStruct((b, l // CHUNK, HEADS, HEAD_DIM, HEAD_DIM), _BF16),
                   jax.ShapeDtypeStruct((b, HEADS, HEAD_DIM, HEAD_DIM), _F32)]
                  + [jax.ShapeDtypeStruct(w.shape, _BF16) for w in bwd_weights],
        scratch_shapes=[pltpu.VMEM((HEADS, HEAD_DIM, HEAD_DIM), _F32),
                        pltpu.VMEM((2, tile, a_width), _F32),
                        pltpu.VMEM((2, tile, 3 * r_width), _BF16)],
        compiler_params=pltpu.CompilerParams(dimension_semantics=("arbitrary", "arbitrary"),
                                             vmem_limit_bytes=VMEM_LIMIT_BYTES),
        name="fwd",
    )(x, mod3, mod3, norm1, w16, w16, w16, w16, w16, sg_gain, sg_w, sg_bias, rope, tables, ctx, mod3, mod3,
      *bwd_weights)


def _bwd_kernel(x_ref, ya_ref, qkvg_ref, sfin_ref, g1_ref, sh_ref, sc_ref, g2_ref, n2_ref, nf_ref,
                wout_ref, wg_ref, wu_ref, wd_ref, tab_ref, s0_ref, o_ref, s_ref, yb_ref, *, n_chunks, tiles_per_batch):
    a_width = GROUPS * LANES
    r_width = HEADS * HEAD_DIM

    @pl.when(pl.program_id(0) % tiles_per_batch == 0)
    def _():
        s_ref[...] = s0_ref[...]

    def retain(slot, out):
        def load(n):
            rows = slice(n * CHUNK, (n + 1) * CHUNK)
            q, k, v, gate_b, gate_f = (qkvg_ref[rows, i * r_width:(i + 1) * r_width] for i in range(5))
            q32, v32 = q.astype(_F32), v.astype(_F32)
            return dict(q=q, k=k, v=v, gate_f=gate_f.astype(_F32), gate_b=gate_b.astype(_F32),
                        xq_f=(q32 * tab_ref[T_XI_F]).astype(_BF16), xq_b=(q32 * tab_ref[T_XI_B]).astype(_BF16),
                        vz_b=(v32 * tab_ref[T_ZETA_B]).astype(_BF16),
                        s_f=[sfin_ref[n, h] for h in range(HEADS)])

        def emit(n, y):
            out.add(yb_ref, (slot, slice(n * CHUNK, (n + 1) * CHUNK), slice(None)), y)

        return _retention_stages(load, list(reversed(range(n_chunks))), BWD_RETAIN_GROUPS, s_ref, tab_ref, emit, out)

    def finish(slot, out):
        y = _dot(ya_ref[...], wout_ref[:a_width, :]) + _dot(yb_ref[slot].astype(_BF16), wout_ref[a_width:, :])
        x1 = x_ref[...] + g1_ref[...] * y
        h2 = ((_rms(x1) * n2_ref[...]) * (1.0 + sc_ref[...]) + sh_ref[...]).astype(_BF16)
        acc = None
        d_ff = wg_ref.shape[1]
        for lo in range(0, d_ff, FFN_BLOCK):
            yield
            cols = slice(lo, min(lo + FFN_BLOCK, d_ff))
            act = (_silu(_dot(h2, wg_ref[:, cols])) * _dot(h2, wu_ref[:, cols])).astype(_BF16)
            part = _dot(act, wd_ref[cols, :])
            acc = part if acc is None else acc + part
        x2 = x1 + g2_ref[...] * acc
        out.add(o_ref, (slice(None), slice(None)), _rms(x2) * nf_ref[...])

    def first(slot):
        out = _DeferredStores()
        _interleave(*retain(slot, out))
        out.commit()

    def middle(slot):
        out = _DeferredStores()
        _interleave(*retain(slot, out), finish(1 - slot, out))
        out.commit()

    def last(slot):
        out = _DeferredStores()
        _interleave(finish(slot, out))
        out.commit()

    _skewed_steps(first, middle, last, axis=0)


def _backward_pass(x, ya, qkvg, sfin, mod3, norm2, norm_f, w_out, w_gate, w_up, w_down, tables, s_cb, tile):
    b, l, d = x.shape
    nt = l // tile
    total = b * nt
    const2 = lambda t: (0, 0)
    retained = lambda t: jnp.minimum(t, total - 1)
    finished = lambda t: jnp.maximum(t - 1, 0)
    place = lambda f: (f // nt, nt - 1 - f % nt)
    cur = lambda t: (*place(retained(t)), 0)
    prev = lambda t: (*place(finished(t)), 0)
    resident = dict(pipeline_mode=pl.Buffered(1))
    mod_spec = lambda j: pl.BlockSpec((None, 1, d), lambda t: (finished(t) // nt, 0, j))
    return pl.pallas_call(
        functools.partial(_bwd_kernel, n_chunks=tile // CHUNK, tiles_per_batch=nt),
        grid=(total + 1,),
        in_specs=[pl.BlockSpec((None, tile, d), prev),
                  pl.BlockSpec((None, tile, ya.shape[-1]), prev),
                  pl.BlockSpec((None, tile, qkvg.shape[-1]), cur),
                  pl.BlockSpec((None, tile // CHUNK, HEADS, HEAD_DIM, HEAD_DIM),
                               lambda t: (*place(retained(t)), 0, 0, 0)),
                  mod_spec(2), mod_spec(3), mod_spec(4), mod_spec(5),
                  pl.BlockSpec((1, d), const2),
                  pl.BlockSpec((1, d), const2),
                  pl.BlockSpec(w_out.shape, const2, **resident),
                  pl.BlockSpec(w_gate.shape, const2, **resident),
                  pl.BlockSpec(w_up.shape, const2, **resident),
                  pl.BlockSpec(w_down.shape, const2, **resident),
                  pl.BlockSpec(tables.shape, lambda t: (0, 0, 0)),
                  pl.BlockSpec((None, HEADS, HEAD_DIM, HEAD_DIM), lambda t: (retained(t) // nt, 0, 0, 0))],
        out_specs=pl.BlockSpec((None, tile, d), prev),
        out_shape=jax.ShapeDtypeStruct((b, l, d), x.dtype),
        scratch_shapes=[pltpu.VMEM((HEADS, HEAD_DIM, HEAD_DIM), _F32),
                        pltpu.VMEM((2, tile, HEADS * HEAD_DIM), _F32)],
        compiler_params=pltpu.CompilerParams(dimension_semantics=("arbitrary",),
                                             vmem_limit_bytes=VMEM_LIMIT_BYTES),
        name="bwd",
    )(x, ya, qkvg, sfin, mod3, mod3, mod3, mod3, norm2, norm_f, w_out, w_gate, w_up, w_down, tables, s_cb)


def _rope_tables(l, k_scale):
    n_freq = HEAD_DIM // 4
    inv = np.float32(ROPE_BASE) ** (-np.arange(n_freq, dtype=np.float32) / np.float32(n_freq))
    rows = np.repeat(np.arange(l // GRID_W, dtype=np.float32), GRID_W)
    cols = np.tile(np.arange(GRID_W, dtype=np.float32), l // GRID_W)
    ang_r, ang_c = rows[:, None] * inv[None, :], cols[:, None] * inv[None, :]
    cos = np.concatenate([np.cos(ang_r), np.cos(ang_c)], axis=-1)
    sin = np.concatenate([np.sin(ang_r), np.sin(ang_c)], axis=-1)
    cos2, sin2 = np.concatenate([cos, cos], axis=-1), np.concatenate([-sin, sin], axis=-1)
    return np.concatenate([cos2, sin2, cos2 * k_scale, sin2 * k_scale], axis=-1).astype(np.float32)


def kernel(x, c, ctx, c_ctx, w_mod, b_mod, norm1, w_in, sg_gain, sg_w, sg_b, ret_logit_f, ret_logit_b,
           w_out, norm2, w_gate, w_up, w_down, norm_f):
    b, l, d = x.shape
    a_width, r_width = GROUPS * LANES, HEADS * HEAD_DIM
    tile = min(TILE, l)
    assert w_mod.shape[0] == 1 and b + 1 <= MOD_ROWS and l % tile == 0 and tile % CHUNK == 0
    assert ctx.shape[1] % CHUNK == 0 and l // tile >= 2
    assert a_width == r_width and d == a_width + r_width
    k_scale = HEAD_DIM ** -0.5

    mod3, w16, tables = _prepare(c, c_ctx, w_mod[0], b_mod[0][None, :], w_in[0], ret_logit_f[0], ret_logit_b[0])
    sg_bias = jnp.broadcast_to(jnp.transpose(sg_b[0])[:, :, None], (CHUNK, GROUPS, LANES)).reshape(CHUNK, a_width)
    ya, qkvg, sfin, s_cb, w_out_b, w_gate_b, w_up_b, w_down_b = _forward_pass(
        x, ctx, mod3, norm1, w16, sg_gain, sg_w[0], sg_bias, _rope_tables(l, k_scale),
        tables, (w_out[0], w_gate[0], w_up[0], w_down[0]), tile, k_scale)
    return _backward_pass(x, ya, qkvg, sfin, mod3, norm2, norm_f[None, :], w_out_b, w_gate_b, w_up_b, w_down_b,
                          tables, s_cb, BWD_TILE)
```

```python
import functools

import numpy as np
import jax
import jax.numpy as jnp
from jax import lax
from jax.experimental import pallas as pl
from jax.experimental.pallas import tpu as pltpu

CHUNK = 128
GRID_W = 64
GROUPS = 4
HEADS = 4
HEAD_DIM = 128
RMS_EPS = 1e-6
ROPE_BASE = 10000.0
LANES = 128
MOD_ROWS = 8
TILE = 512
BWD_TILE = 256
PROJ_CHUNKS = 1
BWD_RETAIN_GROUPS = 2
FFN_BLOCK = 1024
VMEM_LIMIT_BYTES = 56 * 1024 * 1024

IN_BLOCKS = 7
U_BLOCK, V_BLOCK, Q_BLOCK, K_BLOCK, VR_BLOCK, GF_BLOCK, GB_BLOCK = range(IN_BLOCKS)

T_DMASK_F, T_DMASK_B, T_XI_F, T_XI_B, T_ZETA_F, T_ZETA_B, T_DECAY_F, T_DECAY_B = range(8)

_BF16 = jnp.bfloat16
_F32 = jnp.float32


def _dot(a, b):
    return jnp.dot(a, b, preferred_element_type=_F32)


def _dot_nt(a, b):
    return lax.dot_general(a, b, (((1,), (1,)), ((), ())), preferred_element_type=_F32)


def _dot_tn(a, b):
    return lax.dot_general(a, b, (((0,), (0,)), ((), ())), preferred_element_type=_F32)


def _silu(x):
    return x * (1.0 / (1.0 + jnp.exp(-x)))


def _gelu_tanh(x):
    c = float(np.sqrt(2.0 / np.pi))
    half = 0.5 * x
    return half + half * jnp.tanh(x * (c + (0.044715 * c) * (x * x)))


def _rms(x):
    return x * lax.rsqrt(jnp.mean(x * x, axis=-1, keepdims=True) + RMS_EPS)


def _head(a, h):
    return a[:, h * HEAD_DIM:(h + 1) * HEAD_DIM]


def _prep_kernel(c_ref, cctx_ref, wmod_ref, bmod_ref, win_ref, lf_ref, lb_ref, mod_ref, w16_ref, tab_ref):
    j = pl.program_id(0)

    n_batch, d = c_ref.shape
    cc = jnp.concatenate([c_ref[...], cctx_ref[...], jnp.zeros((MOD_ROWS - n_batch - 1, d), _F32)], axis=0)
    mod = _dot(_silu(cc), wmod_ref[...]) + bmod_ref[...]
    for r in range(MOD_ROWS):
        mod_ref[r] = mod[r:r + 1, :]

    def reorder_head_dims(t):
        quarter = HEAD_DIM // 4
        lane = lax.broadcasted_iota(jnp.int32, (t.shape[0], HEAD_DIM), 1)
        heads = []
        for h in range(t.shape[1] // HEAD_DIM):
            th = _head(t, h)
            from_right = pltpu.roll(th, HEAD_DIM - quarter, 1)
            from_left = pltpu.roll(th, quarter, 1)
            heads.append(jnp.where((lane >= quarter) & (lane < 2 * quarter), from_right,
                                   jnp.where((lane >= 2 * quarter) & (lane < 3 * quarter), from_left, th)))
        return jnp.concatenate(heads, axis=-1)

    is_qk = (j == Q_BLOCK) | (j == K_BLOCK)

    @pl.when(is_qk)
    def _():
        w16_ref[...] = reorder_head_dims(win_ref[...]).astype(_BF16)

    @pl.when(jnp.logical_not(is_qk))
    def _():
        w16_ref[...] = win_ref[...].astype(_BF16)

    @pl.when(j == 0)
    def _():
        def log_sigmoid(x):
            return -(jnp.maximum(-x, 0.0) + jnp.log1p(jnp.exp(-jnp.abs(x))))

        shape = (CHUNK, HEADS * HEAD_DIM)
        head = lax.broadcasted_iota(jnp.int32, shape, 1) // HEAD_DIM

        def per_head(logit_ref):
            t = jnp.zeros(shape, _F32)
            for h in range(HEADS):
                t = jnp.where(head == h, logit_ref[h], t)
            return t

        lg_f = log_sigmoid(per_head(lf_ref))
        lg_b = log_sigmoid(per_head(lb_ref))
        i = lax.broadcasted_iota(jnp.int32, shape, 0).astype(_F32)
        jj = (lax.broadcasted_iota(jnp.int32, shape, 1) & (HEAD_DIM - 1)).astype(_F32)
        tab_ref[T_DMASK_F] = jnp.where(i >= jj, jnp.exp(lg_f * jnp.maximum(i - jj, 0.0)), 0.0)
        tab_ref[T_DMASK_B] = jnp.where(jj >= i, jnp.exp(lg_b * jnp.maximum(jj - i, 0.0)), 0.0)
        tab_ref[T_XI_F] = jnp.exp(lg_f * (i + 1.0))
        tab_ref[T_XI_B] = jnp.exp(lg_b * (CHUNK - i))
        tab_ref[T_ZETA_F] = jnp.exp(lg_f * (CHUNK - 1.0 - i))
        tab_ref[T_ZETA_B] = jnp.exp(lg_b * i)
        tab_ref[T_DECAY_F] = jnp.exp(lg_f * CHUNK)
        tab_ref[T_DECAY_B] = jnp.exp(lg_b * CHUNK)


def _prepare(c, c_ctx, w_mod, b_mod, w_in, logit_f, logit_b):
    d, n_mod = w_mod.shape
    r_width = HEADS * HEAD_DIM
    n_mod_blocks = n_mod // d
    assert w_in.shape == (d, IN_BLOCKS * r_width) and n_mod_blocks <= IN_BLOCKS
    mod_blk = lambda j: jnp.minimum(j, n_mod_blocks - 1)
    return pl.pallas_call(
        _prep_kernel,
        grid=(IN_BLOCKS,),
        in_specs=[pl.BlockSpec(c.shape, lambda j: (0, 0)),
                  pl.BlockSpec((1, d), lambda j: (0, 0)),
                  pl.BlockSpec((d, d), lambda j: (0, mod_blk(j))),
                  pl.BlockSpec((1, d), lambda j: (0, mod_blk(j))),
                  pl.BlockSpec((d, r_width), lambda j: (0, j)),
                  pl.BlockSpec(memory_space=pltpu.SMEM),
                  pl.BlockSpec(memory_space=pltpu.SMEM)],
        out_specs=[pl.BlockSpec((MOD_ROWS, 1, d), lambda j: (0, 0, mod_blk(j))),
                   pl.BlockSpec((d, r_width), lambda j: (0, j)),
                   pl.BlockSpec((8, CHUNK, r_width), lambda j: (0, 0, 0))],
        out_shape=[jax.ShapeDtypeStruct((MOD_ROWS, 1, n_mod), _F32),
                   jax.ShapeDtypeStruct(w_in.shape, _BF16),
                   jax.ShapeDtypeStruct((8, CHUNK, r_width), _F32)],
        compiler_params=pltpu.CompilerParams(dimension_semantics=("arbitrary",),
                                             vmem_limit_bytes=VMEM_LIMIT_BYTES),
        name="prep",
    )(c, c_ctx[None, :], w_mod, b_mod, w_in, logit_f.astype(_F32), logit_b.astype(_F32))


def _context_states(ctx, shift, scale, gain, w_k, w_v, tab_ref, k_scale):
    hc = ((_rms(ctx) * gain) * (1.0 + scale) + shift).astype(_BF16)
    k = (_dot(hc, w_k) * k_scale).astype(_BF16)
    v = _dot(hc, w_v)
    n_chunks = ctx.shape[0] // CHUNK
    s_fwd, s_bwd = [], []
    for h in range(HEADS):
        s_f = jnp.zeros((HEAD_DIM, HEAD_DIM), _F32)
        s_b = jnp.zeros((HEAD_DIM, HEAD_DIM), _F32)
        for n in range(n_chunks):
            rows = slice(n * CHUNK, (n + 1) * CHUNK)
            upd = _dot_tn(_head(k[rows], h), (_head(v[rows], h) * _head(tab_ref[T_ZETA_F], h)).astype(_BF16))
            s_f = _head(tab_ref[T_DECAY_F], h) * s_f + upd
        for n in reversed(range(n_chunks)):
            rows = slice(n * CHUNK, (n + 1) * CHUNK)
            upd = _dot_tn(_head(k[rows], h), (_head(v[rows], h) * _head(tab_ref[T_ZETA_B], h)).astype(_BF16))
            s_b = _head(tab_ref[T_DECAY_B], h) * s_b + upd
        s_fwd.append(s_f)
        s_bwd.append(s_b)
    return s_fwd, s_bwd


def _retention_tile(load, chunk_order, state, tab_ref, emit):
    keys = [(n, h) for n in chunk_order for h in range(HEADS)]
    c = {n: load(n) for n in chunk_order}
    scores = {(n, h): _dot_nt(_head(c[n]["q"], h), _head(c[n]["k"], h)) for n, h in keys}
    upd = {(n, h): _dot_tn(_head(c[n]["k"], h), _head(c[n]["vz_b"], h)) for n, h in keys}
    yield
    lhs_f, lhs_b, rhs_f, rhs_b = {}, {}, {}, {}
    for n, h in keys:
        v = _head(c[n]["v"], h)
        lhs_f[n, h] = jnp.concatenate([(scores[n, h] * _head(tab_ref[T_DMASK_F], h)).astype(_BF16),
                                       _head(c[n]["xq_f"], h)], axis=1)
        lhs_b[n, h] = jnp.concatenate([(scores[n, h] * _head(tab_ref[T_DMASK_B], h)).astype(_BF16),
                                       _head(c[n]["xq_b"], h)], axis=1)
        rhs_f[n, h] = jnp.concatenate([v, c[n]["s_f"][h]], axis=0)
        rhs_b[n, h] = jnp.concatenate([v, state[h].astype(_BF16)], axis=0)
        state[h] = _head(tab_ref[T_DECAY_B], h) * state[h] + upd[n, h]
    yield
    o_f = {key: _dot(lhs_f[key], rhs_f[key]) for key in keys}
    o_b = {key: _dot(lhs_b[key], rhs_b[key]) for key in keys}
    yield
    for n in chunk_order:
        emit(n, jnp.concatenate([_silu(_head(c[n]["gate_f"], h)) * _rms(o_f[n, h])
                                 + _silu(_head(c[n]["gate_b"], h)) * _rms(o_b[n, h]) for h in range(HEADS)],
                                axis=-1))


def _retention_stages(load, chunk_order, n_groups, s_ref, tab_ref, emit, out):
    state = [None] * HEADS
    order = list(chunk_order)
    per = len(order) // n_groups
    groups = [order[i * per:(i + 1) * per] for i in range(n_groups)]

    def group_stage(i):
        if i:
            for _ in range(i):
                yield
        inner = _retention_tile(load, groups[i], state, tab_ref, emit)
        next(inner)
        yield
        if i == 0:
            for h in range(HEADS):
                state[h] = s_ref[h]
        next(inner)
        yield
        for _ in inner:
            yield
        if i == n_groups - 1:
            for h in range(HEADS):
                out.add(s_ref, h, state[h])

    return [group_stage(i) for i in range(n_groups)]


def _interleave(*stages):
    stages = list(stages)
    while stages:
        for g in list(stages):
            try:
                next(g)
            except StopIteration:
                stages.remove(g)


class _DeferredStores:
    def __init__(self):
        self._items = []

    def add(self, ref, idx, value):
        self._items.append((ref, idx, value))

    def commit(self):
        for ref, idx, value in self._items:
            ref[idx] = value


def _skewed_steps(first, middle, last, axis=1):
    s = pl.program_id(axis)
    n_tiles = pl.num_programs(axis) - 1

    @pl.when(s == 0)
    def _():
        first(0)

    @pl.when((s > 0) & (s < n_tiles))
    def _():
        middle(s & 1)

    @pl.when(s == n_tiles)
    def _():
        last((s - 1) & 1)


def _fwd_kernel(x_ref, sh_ref, sc_ref, n1_ref, wuv_ref, wqk_ref, wvr_ref, wgf_ref, wgb_ref, gain_ref, sgw_ref, sgb_ref,
                rope_ref, tab_ref, ctx_ref, csh_ref, csc_ref, wo32_ref, wg32_ref, wu32_ref, wd32_ref,
                ya_ref, qkvg_ref, sfin_ref, scb_ref, wo16_ref, wg16_ref, wu16_ref, wd16_ref, s_ref, f32_ref, b16_ref, *,
                n_chunks, k_scale, tiles_per_batch):
    a_width = GROUPS * LANES
    r_width = HEADS * HEAD_DIM
    U = 0
    VG, K, VZ = (i * r_width for i in range(3))
    rows_all = slice(None)

    def convert(out):
        for src, dst in ((wo32_ref, wo16_ref), (wg32_ref, wg16_ref), (wu32_ref, wu16_ref), (wd32_ref, wd16_ref)):
            out.add(dst, (rows_all, rows_all), src[...].astype(_BF16))

    def project(slot, out):
        n_sub = n_chunks // PROJ_CHUNKS
        for blk in range(n_sub):
            rows = slice(blk * PROJ_CHUNKS * CHUNK, (blk + 1) * PROJ_CHUNKS * CHUNK)

            def per_chunk(t, table):
                return jnp.concatenate([t[n * CHUNK:(n + 1) * CHUNK] * table for n in range(PROJ_CHUNKS)], axis=0)

            x = x_ref[rows, :]
            r = lax.rsqrt(jnp.mean(x * x, axis=-1, keepdims=True) + RMS_EPS)
            hx = ((x * r) * (n1_ref[...] * (1.0 + sc_ref[...])) + sh_ref[...]).astype(_BF16)

            def proj(w_ref, lo, width):
                return _dot(hx, w_ref[:, lo:lo + width])

            yield
            out.add(f32_ref, (slot, rows, slice(U, U + a_width)), _gelu_tanh(proj(wuv_ref, 0, a_width)))
            v = _gelu_tanh(proj(wuv_ref, a_width, a_width))
            vg = jnp.concatenate([(_rms(_head(v, g)) * _head(gain_ref[...], g)).astype(_BF16)
                                  for g in range(GROUPS)], axis=-1)
            out.add(b16_ref, (slot, rows, slice(VG, VG + a_width)), vg)
            yield

            def rope(t, table):
                cos = rope_ref[rows, 2 * table * HEAD_DIM:(2 * table + 1) * HEAD_DIM]
                sin = rope_ref[rows, (2 * table + 1) * HEAD_DIM:(2 * table + 2) * HEAD_DIM]
                return jnp.concatenate(
                    [_head(t, h) * cos + pltpu.roll(_head(t, h), HEAD_DIM // 2, 1) * sin for h in range(HEADS)],
                    axis=-1)

            q16 = rope(proj(wqk_ref, 0, r_width), 0).astype(_BF16)
            k16 = rope(proj(wqk_ref, r_width, r_width), 1).astype(_BF16)
            out.add(b16_ref, (slot, rows, slice(K, K + r_width)), k16)
            out.add(qkvg_ref, (rows, slice(0, r_width)), q16)
            out.add(qkvg_ref, (rows, slice(r_width, 2 * r_width)), k16)
            yield
            vr = proj(wvr_ref, 0, r_width)
            out.add(b16_ref, (slot, rows, slice(VZ, VZ + r_width)), per_chunk(vr, tab_ref[T_ZETA_F]).astype(_BF16))
            out.add(qkvg_ref, (rows, slice(2 * r_width, 3 * r_width)), vr.astype(_BF16))
            out.add(qkvg_ref, (rows, slice(3 * r_width, 4 * r_width)), proj(wgb_ref, 0, r_width).astype(_BF16))
            out.add(qkvg_ref, (rows, slice(4 * r_width, 5 * r_width)), proj(wgf_ref, 0, r_width).astype(_BF16))

    def gate_mix(slot, out):
        mixed = []
        for g in range(GROUPS):
            vg = jnp.concatenate([b16_ref[slot, n * CHUNK:(n + 1) * CHUNK, VG + g * LANES:VG + (g + 1) * LANES]
                                  for n in range(n_chunks)], axis=1)
            mixed.append(_dot(sgw_ref[g].astype(_BF16), vg))
        yield
        for g in range(GROUPS):
            cols = slice(g * LANES, (g + 1) * LANES)
            for n in range(n_chunks):
                rows = slice(n * CHUNK, (n + 1) * CHUNK)
                m = mixed[g][:, n * LANES:(n + 1) * LANES] + sgb_ref[:, cols]
                u = f32_ref[slot, rows, U + g * LANES:U + (g + 1) * LANES]
                out.add(ya_ref, (rows, cols), (u * m).astype(_BF16))

    def scan(slot, out):
        upd = {(n, h): _dot_tn(b16_ref[slot, n * CHUNK:(n + 1) * CHUNK, K + h * HEAD_DIM:K + (h + 1) * HEAD_DIM],
                               b16_ref[slot, n * CHUNK:(n + 1) * CHUNK, VZ + h * HEAD_DIM:VZ + (h + 1) * HEAD_DIM])
               for n in range(n_chunks) for h in range(HEADS)}
        yield
        for h in range(HEADS):
            state = s_ref[h]
            for n in range(n_chunks):
                out.add(sfin_ref, (n, h), state.astype(_BF16))
                state = _head(tab_ref[T_DECAY_F], h) * state + upd[n, h]
            out.add(s_ref, h, state)

    def first(slot):
        out = _DeferredStores()
        convert(out)
        _interleave(project(slot, out))
        out.commit()

    def middle(slot):
        out = _DeferredStores()
        convert(out)
        _interleave(scan(1 - slot, out), gate_mix(1 - slot, out), project(slot, out))
        out.commit()

    def last(slot):
        out = _DeferredStores()
        convert(out)
        _interleave(scan(slot, out), gate_mix(slot, out))
        out.commit()

    _skewed_steps(first, middle, last, axis=0)

    step = pl.program_id(0)

    @pl.when((step % tiles_per_batch == 0) & (step < pl.num_programs(0) - 1))
    def _():
        s_fwd, s_bwd = _context_states(ctx_ref[...], csh_ref[...], csc_ref[...], n1_ref[...],
                                       wqk_ref[:, r_width:], wvr_ref[...], tab_ref, k_scale)
        for h in range(HEADS):
            s_ref[h] = s_fwd[h]
            scb_ref[h] = s_bwd[h]


def _forward_pass(x, ctx, mod3, norm1, w16, sg_gain, sg_w, sg_bias, rope, tables, bwd_weights, tile, k_scale):
    b, l, d = x.shape
    nt = l // tile
    a_width, r_width = GROUPS * LANES, HEADS * HEAD_DIM
    ctx_row = b
    total = b * nt
    projected = lambda t: jnp.minimum(t, total - 1)
    mixed = lambda t: jnp.maximum(t - 1, 0)
    state_spec = pl.BlockSpec((None, HEADS, HEAD_DIM, HEAD_DIM), lambda t: (projected(t) // nt, 0, 0, 0))
    const2 = lambda t: (0, 0)
    cur = lambda t: (projected(t) // nt, projected(t) % nt, 0)
    prev = lambda t: (mixed(t) // nt, mixed(t) % nt, 0)
    resident = dict(pipeline_mode=pl.Buffered(1))
    n_blk = max(n for n in (1, 2, 4, 8, 16) if n <= total + 1)
    blk = lambda t: (jnp.minimum(t, n_blk - 1), 0)
    w_specs = [pl.BlockSpec((w.shape[0] // n_blk, w.shape[1]), blk) for w in bwd_weights]
    return pl.pallas_call(
        functools.partial(_fwd_kernel, n_chunks=tile // CHUNK, k_scale=k_scale, tiles_per_batch=nt),
        grid=(total + 1,),
        in_specs=[pl.BlockSpec((None, tile, d), cur),
                  pl.BlockSpec((None, 1, d), lambda t: (projected(t) // nt, 0, 0)),
                  pl.BlockSpec((None, 1, d), lambda t: (projected(t) // nt, 0, 1)),
                  pl.BlockSpec((1, d), const2),
                  pl.BlockSpec((d, 2 * r_width), lambda t: (0, U_BLOCK // 2), **resident),
                  pl.BlockSpec((d, 2 * r_width), lambda t: (0, Q_BLOCK // 2), **resident),
                  pl.BlockSpec((d, r_width), lambda t: (0, VR_BLOCK), **resident),
                  pl.BlockSpec((d, r_width), lambda t: (0, GF_BLOCK), **resident),
                  pl.BlockSpec((d, r_width), lambda t: (0, GB_BLOCK), **resident),
                  pl.BlockSpec((1, a_width), const2),
                  pl.BlockSpec(sg_w.shape, lambda t: (0, 0, 0)),
                  pl.BlockSpec(sg_bias.shape, const2),
                  pl.BlockSpec((tile, rope.shape[1]), lambda t: (projected(t) % nt, 0)),
                  pl.BlockSpec(tables.shape, lambda t: (0, 0, 0)),
                  pl.BlockSpec((None,) + ctx.shape[1:], lambda t: (projected(t) // nt, 0, 0)),
                  pl.BlockSpec((None, 1, d), lambda t: (ctx_row, 0, 0)),
                  pl.BlockSpec((None, 1, d), lambda t: (ctx_row, 0, 1))]
                 + w_specs,
        out_specs=[pl.BlockSpec((None, tile, a_width), prev),
                   pl.BlockSpec((None, tile, 5 * r_width), cur),
                   pl.BlockSpec((None, tile // CHUNK, HEADS, HEAD_DIM, HEAD_DIM),
                                lambda t: (mixed(t) // nt, mixed(t) % nt, 0, 0, 0)),
                   state_spec] + w_specs,
        out_shape=[jax.ShapeDtypeStruct((b, l, a_width), _BF16),
                   jax.ShapeDtypeStruct((b, l, 5 * r_width), _BF16),
                   jax.ShapeDtypeStruct((b, l // CHUNK, HEADS, HEAD_DIM, HEAD_DIM), _BF16),
                   jax.ShapeDtypeStruct((b, HEADS, HEAD_DIM, HEAD_DIM), _F32)]
                  + [jax.ShapeDtypeStruct(w.shape, _BF16) for w in bwd_weights],
        scratch_shapes=[pltpu.VMEM((HEADS, HEAD_DIM, HEAD_DIM), _F32),
                        pltpu.VMEM((2, tile, a_width), _F32),
                        pltpu.VMEM((2, tile, 3 * r_width), _BF16)],
        compiler_params=pltpu.CompilerParams(dimension_semantics=("arbitrary",),
                                             vmem_limit_bytes=VMEM_LIMIT_BYTES),
        name="fwd",
    )(x, mod3, mod3, norm1, w16, w16, w16, w16, w16, sg_gain, sg_w, sg_bias, rope, tables, ctx, mod3, mod3,
      *bwd_weights)


def _bwd_kernel(x_ref, ya_ref, qkvg_ref, sfin_ref, g1_ref, sh_ref, sc_ref, g2_ref, n2_ref, nf_ref,
                wout_ref, wg_ref, wu_ref, wd_ref, tab_ref, s0_ref, o_ref, s_ref, yb_ref, *, n_chunks, tiles_per_batch):
    a_width = GROUPS * LANES
    r_width = HEADS * HEAD_DIM

    @pl.when(pl.program_id(0) % tiles_per_batch == 0)
    def _():
        s_ref[...] = s0_ref[...]

    def retain(slot, out):
        def load(n):
            rows = slice(n * CHUNK, (n + 1) * CHUNK)
            q, k, v, gate_b, gate_f = (qkvg_ref[rows, i * r_width:(i + 1) * r_width] for i in range(5))
            q32, v32 = q.astype(_F32), v.astype(_F32)
            return dict(q=q, k=k, v=v, gate_f=gate_f.astype(_F32), gate_b=gate_b.astype(_F32),
                        xq_f=(q32 * tab_ref[T_XI_F]).astype(_BF16), xq_b=(q32 * tab_ref[T_XI_B]).astype(_BF16),
                        vz_b=(v32 * tab_ref[T_ZETA_B]).astype(_BF16),
                        s_f=[sfin_ref[n, h] for h in range(HEADS)])

        def emit(n, y):
            out.add(yb_ref, (slot, slice(n * CHUNK, (n + 1) * CHUNK), slice(None)), y)

        return _retention_stages(load, list(reversed(range(n_chunks))), BWD_RETAIN_GROUPS, s_ref, tab_ref, emit, out)

    def finish(slot, out):
        y = _dot(ya_ref[...], wout_ref[:a_width, :]) + _dot(yb_ref[slot].astype(_BF16), wout_ref[a_width:, :])
        x1 = x_ref[...] + g1_ref[...] * y
        h2 = ((_rms(x1) * n2_ref[...]) * (1.0 + sc_ref[...]) + sh_ref[...]).astype(_BF16)
        acc = None
        d_ff = wg_ref.shape[1]
        for lo in range(0, d_ff, FFN_BLOCK):
            yield
            cols = slice(lo, min(lo + FFN_BLOCK, d_ff))
            act = (_silu(_dot(h2, wg_ref[:, cols])) * _dot(h2, wu_ref[:, cols])).astype(_BF16)
            part = _dot(act, wd_ref[cols, :])
            acc = part if acc is None else acc + part
        x2 = x1 + g2_ref[...] * acc
        out.add(o_ref, (slice(None), slice(None)), _rms(x2) * nf_ref[...])

    def first(slot):
        out = _DeferredStores()
        _interleave(*retain(slot, out))
        out.commit()

    def middle(slot):
        out = _DeferredStores()
        _interleave(*retain(slot, out), finish(1 - slot, out))
        out.commit()

    def last(slot):
        out = _DeferredStores()
        _interleave(finish(slot, out))
        out.commit()

    _skewed_steps(first, middle, last, axis=0)


def _backward_pass(x, ya, qkvg, sfin, mod3, norm2, norm_f, w_out, w_gate, w_up, w_down, tables, s_cb, tile):
    b, l, d = x.shape
    nt = l // tile
    total = b * nt
    const2 = lambda t: (0, 0)
    retained = lambda t: jnp.minimum(t, total - 1)
    finished = lambda t: jnp.maximum(t - 1, 0)
    place = lambda f: (f // nt, nt - 1 - f % nt)
    cur = lambda t: (*place(retained(t)), 0)
    prev = lambda t: (*place(finished(t)), 0)
    resident = dict(pipeline_mode=pl.Buffered(1))
    mod_spec = lambda j: pl.BlockSpec((None, 1, d), lambda t: (finished(t) // nt, 0, j))
    return pl.pallas_call(
        functools.partial(_bwd_kernel, n_chunks=tile // CHUNK, tiles_per_batch=nt),
        grid=(total + 1,),
        in_specs=[pl.BlockSpec((None, tile, d), prev),
                  pl.BlockSpec((None, tile, ya.shape[-1]), prev),
                  pl.BlockSpec((None, tile, qkvg.shape[-1]), cur),
                  pl.BlockSpec((None, tile // CHUNK, HEADS, HEAD_DIM, HEAD_DIM),
                               lambda t: (*place(retained(t)), 0, 0, 0)),
                  mod_spec(2), mod_spec(3), mod_spec(4), mod_spec(5),
                  pl.BlockSpec((1, d), const2),
                  pl.BlockSpec((1, d), const2),
                  pl.BlockSpec(w_out.shape, const2, **resident),
                  pl.BlockSpec(w_gate.shape, const2, **resident),
                  pl.BlockSpec(w_up.shape, const2, **resident),
                  pl.BlockSpec(w_down.shape, const2, **resident),
                  pl.BlockSpec(tables.shape, lambda t: (0, 0, 0)),
                  pl.BlockSpec((None, HEADS, HEAD_DIM, HEAD_DIM), lambda t: (retained(t) // nt, 0, 0, 0))],
        out_specs=pl.BlockSpec((None, tile, d), prev),
        out_shape=jax.ShapeDtypeStruct((b, l, d), x.dtype),
        scratch_shapes=[pltpu.VMEM((HEADS, HEAD_DIM, HEAD_DIM), _F32),
                        pltpu.VMEM((2, tile, HEADS * HEAD_DIM), _F32)],
        compiler_params=pltpu.CompilerParams(dimension_semantics=("arbitrary",),
                                             vmem_limit_bytes=VMEM_LIMIT_BYTES),
        name="bwd",
    )(x, ya, qkvg, sfin, mod3, mod3, mod3, mod3, norm2, norm_f, w_out, w_gate, w_up, w_down, tables, s_cb)


def _rope_tables(l, k_scale):
    n_freq = HEAD_DIM // 4
    inv = np.float32(ROPE_BASE) ** (-np.arange(n_freq, dtype=np.float32) / np.float32(n_freq))
    rows = np.repeat(np.arange(l // GRID_W, dtype=np.float32), GRID_W)
    cols = np.tile(np.arange(GRID_W, dtype=np.float32), l // GRID_W)
    ang_r, ang_c = rows[:, None] * inv[None, :], cols[:, None] * inv[None, :]
    cos = np.concatenate([np.cos(ang_r), np.cos(ang_c)], axis=-1)
    sin = np.concatenate([np.sin(ang_r), np.sin(ang_c)], axis=-1)
    cos2, sin2 = np.concatenate([cos, cos], axis=-1), np.concatenate([-sin, sin], axis=-1)
    return np.concatenate([cos2, sin2, cos2 * k_scale, sin2 * k_scale], axis=-1).astype(np.float32)


def kernel(x, c, ctx, c_ctx, w_mod, b_mod, norm1, w_in, sg_gain, sg_w, sg_b, ret_logit_f, ret_logit_b,
           w_out, norm2, w_gate, w_up, w_down, norm_f):
    b, l, d = x.shape
    a_width, r_width = GROUPS * LANES, HEADS * HEAD_DIM
    tile = min(TILE, l)
    assert w_mod.shape[0] == 1 and b + 1 <= MOD_ROWS and l % tile == 0 and tile % CHUNK == 0
    assert ctx.shape[1] % CHUNK == 0 and l // tile >= 2
    assert a_width == r_width and d == a_width + r_width
    k_scale = HEAD_DIM ** -0.5

    mod3, w16, tables = _prepare(c, c_ctx, w_mod[0], b_mod[0][None, :], w_in[0], ret_logit_f[0], ret_logit_b[0])
    sg_bias = jnp.broadcast_to(jnp.transpose(sg_b[0])[:, :, None], (CHUNK, GROUPS, LANES)).reshape(CHUNK, a_width)
    ya, qkvg, sfin, s_cb, w_out_b, w_gate_b, w_up_b, w_down_b = _forward_pass(
        x, ctx, mod3, norm1, w16, sg_gain, sg_w[0], sg_bias, _rope_tables(l, k_scale),
        tables, (w_out[0], w_gate[0], w_up[0], w_down[0]), tile, k_scale)
    return _backward_pass(x, ya, qkvg, sfin, mod3, norm2, norm_f[None, :], w_out_b, w_gate_b, w_up_b, w_down_b,
                          tables, s_cb, BWD_TILE)
```
